```python
import math
import jax, jax.numpy as jnp
from jax import lax
import numpy as np

D_MODEL = 1024
BATCH = 16
SEQ = 2048
DEPTH = 2
DEC_BATCH = 128
DEC_SEQ = 8
PAST_LEN = 16384
PAGE_SIZE = 128

N_A_LAYERS = DEPTH // 2
N_B_LAYERS = DEPTH - N_A_LAYERS
N_DENSE_LAYERS = (DEPTH + 1) // 2
N_MOE_LAYERS = DEPTH // 2
D_PLE = 256
D_RNN = 1536
RNN_BLOCK = 128
N_RNN_BLOCKS = D_RNN // RNN_BLOCK
CONV_WIDTH = 4
LRU_C = 8.0
N_HEADS = 16
Q_LORA_RANK = 384
KV_LORA_RANK = 256
QK_NOPE_DIM = 64
QK_ROPE_DIM = 32
V_HEAD_DIM = 64
ROPE_THETA = 10000.0
Q_BLOCK = 128
SOFTMAX_SCALE = (QK_NOPE_DIM + QK_ROPE_DIM) ** -0.5
D_FF = 2816
N_EXPERTS = 8
TOP_K = 2
D_FF_EXPERT = 3584
LN_EPS = 1e-5
RMS_EPS = 1e-6
DEEPNORM_ALPHA = (2.0 * DEPTH) ** 0.25
DEEPNORM_BETA = (8.0 * DEPTH) ** -0.25

kernel_name = 'hybrid_rglru_mla_yoco_step'


def layer_norm(x, g, b):
    xf = x.astype(jnp.float32)
    mu = jnp.mean(xf, -1, keepdims=True)
    var = jnp.mean(jnp.square(xf - mu), -1, keepdims=True)
    return ((xf - mu) * lax.rsqrt(var + LN_EPS) * g + b).astype(x.dtype)


def rms_norm(x, g):
    xf = x.astype(jnp.float32)
    return (xf * lax.rsqrt(jnp.mean(jnp.square(xf), -1, keepdims=True) + RMS_EPS) * g).astype(x.dtype)


def rope_angles(pos):
    inv = ROPE_THETA ** (-jnp.arange(0, QK_ROPE_DIM, 2, dtype=jnp.float32) / QK_ROPE_DIM)
    ang = pos[:, None] * inv[None, :]
    return jnp.cos(ang), jnp.sin(ang)


def apply_rope(x, cos, sin):
    x1, x2 = jnp.split(x.astype(jnp.float32), 2, axis=-1)
    return jnp.concatenate([x1 * cos - x2 * sin, x1 * sin + x2 * cos], axis=-1).astype(x.dtype)


def recurrent_block(x, conv_state, h0, w_gate, w_x, conv_w, conv_b, w_a, b_a, w_i, b_i, lam, w_out):
    B, S, _ = x.shape
    gate = jax.nn.gelu(x @ w_gate)
    u = x @ w_x
    u_ext = jnp.concatenate([conv_state.astype(u.dtype), u], axis=1)
    conv = conv_b
    for k in range(CONV_WIDTH):
        conv = conv + u_ext[:, k:k + S] * conv_w[k]
    new_conv_state = u_ext[:, -(CONV_WIDTH - 1):].astype(conv_state.dtype)
    cb = conv.reshape(B, S, N_RNN_BLOCKS, RNN_BLOCK)
    r = jax.nn.sigmoid((jnp.einsum('bsnd,nde->bsne', cb, w_a).reshape(B, S, D_RNN) + b_a).astype(jnp.float32))
    i = jax.nn.sigmoid((jnp.einsum('bsnd,nde->bsne', cb, w_i).reshape(B, S, D_RNN) + b_i).astype(jnp.float32))
    log_a = -LRU_C * r * jax.nn.softplus(-lam.astype(jnp.float32))
    a = jnp.exp(log_a)
    bterm = jnp.sqrt(1.0 - jnp.exp(2.0 * log_a)) * (i * conv.astype(jnp.float32))
    bterm = bterm.at[:, 0].add(a[:, 0] * h0.astype(jnp.float32))

    def combine(lhs, rhs):
        a1, b1 = lhs
        a2, b2 = rhs
        return a1 * a2, a2 * b1 + b2

    _, h = lax.associative_scan(combine, (a, bterm), axis=1)
    y = (h.astype(x.dtype) * gate) @ w_out
    return y, new_conv_state, h[:, -1].astype(h0.dtype)


def swiglu(x, w_gate, w_up, w_down):
    return (jax.nn.silu(x @ w_gate) * (x @ w_up)) @ w_down


def moe_swiglu(x, w_router, w_gate, w_up, w_down):
    logits = (x @ w_router).astype(jnp.float32)
    top_val, top_idx = lax.top_k(logits, TOP_K)
    top_w = jax.nn.softmax(top_val, axis=-1)
    gates = jnp.sum(jax.nn.one_hot(top_idx, N_EXPERTS, dtype=jnp.float32) * top_w[..., None], axis=-2)
    out = jnp.zeros_like(x)
    for e in range(N_EXPERTS):
        out = out + gates[..., e:e + 1].astype(x.dtype) * swiglu(x, w_gate[e], w_up[e], w_down[e])
    return out


def mla_shared_kv(h, w_kv_a, kv_norm_g, cos, sin):
    kv = h @ w_kv_a
    c_kv = rms_norm(kv[..., :KV_LORA_RANK], kv_norm_g)
    k_pe = apply_rope(kv[..., KV_LORA_RANK:], cos[None], sin[None])
    return c_kv, k_pe


def mla_queries(x, w_q_a, q_norm_g, w_q_b, w_uk, cos, sin):
    B, S, _ = x.shape
    cq = rms_norm(x @ w_q_a, q_norm_g)
    q = (cq @ w_q_b).reshape(B, S, N_HEADS, QK_NOPE_DIM + QK_ROPE_DIM)
    q_nope = q[..., :QK_NOPE_DIM]
    q_pe = apply_rope(q[..., QK_NOPE_DIM:], cos[None, :, None, :], sin[None, :, None, :])
    q_lat = jnp.einsum('bshd,chd->bshc', q_nope, w_uk)
    return q_lat * SOFTMAX_SCALE, q_pe * SOFTMAX_SCALE


def mla_output(o_lat, w_uv, w_o):
    B, S = o_lat.shape[:2]
    v = jnp.einsum('bshc,chv->bshv', o_lat, w_uv)
    return v.reshape(B, S, N_HEADS * V_HEAD_DIM) @ w_o


def latent_attention_prompt(q_lat, q_pe, c_kv, k_pe):
    B, S = q_lat.shape[:2]
    nb = S // Q_BLOCK

    def blockify(t):
        return jnp.moveaxis(t.reshape((B, nb, Q_BLOCK) + t.shape[2:]), 1, 0)

    key_pos = jnp.arange(S)

    def one_block(args):
        ql, qp, start = args
        s = (jnp.einsum('bqhc,bkc->bhqk', ql, c_kv) + jnp.einsum('bqhr,bkr->bhqk', qp, k_pe)).astype(jnp.float32)
        q_pos = start + jnp.arange(Q_BLOCK)
        s = jnp.where(key_pos[None, None, None, :] <= q_pos[None, None, :, None], s, -jnp.inf)
        pr = jax.nn.softmax(s, axis=-1).astype(c_kv.dtype)
        return jnp.einsum('bhqk,bkc->bqhc', pr, c_kv)

    o = lax.map(one_block, (blockify(q_lat), blockify(q_pe), jnp.arange(nb) * Q_BLOCK))
    return jnp.moveaxis(o, 0, 1).reshape(B, S, N_HEADS, KV_LORA_RANK)


def latent_attention_sample(q_lat, q_pe, c_new, kpe_new, cache_ckv, cache_kpe, page_table):
    S = q_lat.shape[1]
    s = (jnp.einsum('bqhc,bkc->bhqk', q_lat, c_new) + jnp.einsum('bqhr,bkr->bhqk', q_pe, kpe_new)).astype(jnp.float32)
    s = jnp.where(jnp.tril(jnp.ones((S, S), dtype=bool)), s, -jnp.inf)
    m = jnp.max(s, axis=-1)
    p = jnp.exp(s - m[..., None])
    l = jnp.sum(p, axis=-1)
    acc = jnp.einsum('bhqk,bkc->bhqc', p, c_new.astype(jnp.float32))

    def step(carry, pages):
        m, l, acc = carry
        ck = cache_ckv[pages]
        kp = cache_kpe[pages]
        s = (jnp.einsum('bqhc,bkc->bhqk', q_lat, ck) + jnp.einsum('bqhr,bkr->bhqk', q_pe, kp)).astype(jnp.float32)
        m_new = jnp.maximum(m, jnp.max(s, axis=-1))
        corr = jnp.exp(m - m_new)
        p = jnp.exp(s - m_new[..., None])
        l = l * corr + jnp.sum(p, axis=-1)
        acc = acc * corr[..., None] + jnp.einsum('bhqk,bkc->bhqc', p, ck.astype(jnp.float32))
        return (m_new, l, acc), None

    (m, l, acc), _ = lax.scan(step, (m, l, acc), page_table.T)
    o = acc / l[..., None]
    return jnp.transpose(o, (0, 2, 1, 3)).astype(q_lat.dtype)


def run_trunk(x, p_emb, conv0, rnn0, pos, attend, W):
    cos, sin = rope_angles(pos)
    conv_out, rnn_out = [], []
    c_kv, k_pe = None, None
    for i in range(DEPTH):
        if i < N_A_LAYERS:
            j = i
            mix, cs, hs = recurrent_block(x, conv0[j], rnn0[j], W['rg_w_gate'][j], W['rg_w_x'][j], W['rg_conv_w'][j],
                                          W['rg_conv_b'][j], W['rg_w_a'][j], W['rg_b_a'][j], W['rg_w_i'][j],
                                          W['rg_b_i'][j], W['rg_lambda'][j], W['rg_w_out'][j])
            conv_out.append(cs)
            rnn_out.append(hs)
        else:
            j = i - N_A_LAYERS
            q_lat, q_pe = mla_queries(x, W['mla_w_q_a'][j], W['mla_q_norm_g'][j], W['mla_w_q_b'][j], W['kv_w_uk'], cos, sin)
            o_lat = attend(q_lat, q_pe, c_kv, k_pe)
            mix = mla_output(o_lat, W['kv_w_uv'], W['mla_w_o'][j])
        x = layer_norm(DEEPNORM_ALPHA * x + mix, W['ln_mix_g'][i], W['ln_mix_b'][i])
        k = i // 2
        if i % 2 == 0:
            ff = swiglu(x, W['ffn_w_gate'][k], W['ffn_w_up'][k], W['ffn_w_down'][k])
        else:
            ff = moe_swiglu(x, W['moe_w_router'][k], W['moe_w_gate'][k], W['moe_w_up'][k], W['moe_w_down'][k])
        x = layer_norm(DEEPNORM_ALPHA * x + ff, W['ln_ffn_g'][i], W['ln_ffn_b'][i])
        x = x + jax.nn.sigmoid(x @ W['ple_w_gate'][i]) * (p_emb[i] @ W['ple_w_proj'][i])
        if i == N_A_LAYERS - 1:
            c_kv, k_pe = mla_shared_kv(x, W['kv_w_a'], W['kv_norm_g'], cos, sin)
    return x, jnp.stack(conv_out), jnp.stack(rnn_out), c_kv, k_pe


def setup_inputs(seed: int = 0) -> dict:
    key = jax.random.key(seed)
    keys = iter(jax.random.split(key, 64))

    def nrm(shape, scale):
        return scale * jax.random.normal(next(keys), shape, jnp.float32)

    n_pages = PAST_LEN // PAGE_SIZE
    n_phys = (DEC_BATCH * n_pages * 5) // 4
    page_table = jax.random.permutation(next(keys), n_phys)[:DEC_BATCH * n_pages].reshape(DEC_BATCH, n_pages).astype(jnp.int32)
    a8 = jax.random.uniform(next(keys), (N_A_LAYERS, D_RNN), jnp.float32, minval=0.9, maxval=0.999)
    a_base = a8 ** (1.0 / LRU_C)
    rg_lambda = jnp.log(a_base) - jnp.log1p(-a_base)
    d = D_MODEL
    return {
        'x_prompt': nrm((BATCH, SEQ, d), 1.0),
        'x_sample': nrm((DEC_BATCH, DEC_SEQ, d), 1.0),
        'p_prompt': nrm((DEPTH, BATCH, SEQ, D_PLE), 1.0),
        'p_sample': nrm((DEPTH, DEC_BATCH, DEC_SEQ, D_PLE), 1.0),
        'state_conv': nrm((N_A_LAYERS, DEC_BATCH, CONV_WIDTH - 1, D_RNN), 1.0),
        'state_rnn': nrm((N_A_LAYERS, DEC_BATCH, D_RNN), 0.5),
        'cache_ckv': nrm((n_phys, PAGE_SIZE, KV_LORA_RANK), 1.0),
        'cache_kpe': nrm((n_phys, PAGE_SIZE, QK_ROPE_DIM), 1.0),
        'page_table': page_table,
        'ln_mix_g': 1.0 + nrm((DEPTH, d), 0.02),
        'ln_mix_b': nrm((DEPTH, d), 0.02),
        'ln_ffn_g': 1.0 + nrm((DEPTH, d), 0.02),
        'ln_ffn_b': nrm((DEPTH, d), 0.02),
        'rg_w_gate': nrm((N_A_LAYERS, d, D_RNN), d ** -0.5),
        'rg_w_x': nrm((N_A_LAYERS, d, D_RNN), d ** -0.5),
        'rg_conv_w': nrm((N_A_LAYERS, CONV_WIDTH, D_RNN), CONV_WIDTH ** -0.5),
        'rg_conv_b': nrm((N_A_LAYERS, D_RNN), 0.02),
        'rg_w_a': nrm((N_A_LAYERS, N_RNN_BLOCKS, RNN_BLOCK, RNN_BLOCK), RNN_BLOCK ** -0.5),
        'rg_b_a': nrm((N_A_LAYERS, D_RNN), 0.1),
        'rg_w_i': nrm((N_A_LAYERS, N_RNN_BLOCKS, RNN_BLOCK, RNN_BLOCK), RNN_BLOCK ** -0.5),
        'rg_b_i': nrm((N_A_LAYERS, D_RNN), 0.1),
        'rg_lambda': rg_lambda,
        'rg_w_out': nrm((N_A_LAYERS, D_RNN, d), DEEPNORM_BETA * D_RNN ** -0.5),
        'mla_w_q_a': nrm((N_B_LAYERS, d, Q_LORA_RANK), d ** -0.5),
        'mla_q_norm_g': 1.0 + nrm((N_B_LAYERS, Q_LORA_RANK), 0.02),
        'mla_w_q_b': nrm((N_B_LAYERS, Q_LORA_RANK, N_HEADS * (QK_NOPE_DIM + QK_ROPE_DIM)), Q_LORA_RANK ** -0.5),
        'mla_w_o': nrm((N_B_LAYERS, N_HEADS * V_HEAD_DIM, d), DEEPNORM_BETA * (N_HEADS * V_HEAD_DIM) ** -0.5),
        'kv_w_a': nrm((d, KV_LORA_RANK + QK_ROPE_DIM), d ** -0.5),
        'kv_norm_g': 1.0 + nrm((KV_LORA_RANK,), 0.02),
        'kv_w_uk': nrm((KV_LORA_RANK, N_HEADS, QK_NOPE_DIM), KV_LORA_RANK ** -0.5),
        'kv_w_uv': nrm((KV_LORA_RANK, N_HEADS, V_HEAD_DIM), KV_LORA_RANK ** -0.5),
        'ffn_w_gate': nrm((N_DENSE_LAYERS, d, D_FF), d ** -0.5),
        'ffn_w_up': nrm((N_DENSE_LAYERS, d, D_FF), d ** -0.5),
        'ffn_w_down': nrm((N_DENSE_LAYERS, D_FF, d), DEEPNORM_BETA * D_FF ** -0.5),
        'moe_w_router': nrm((N_MOE_LAYERS, d, N_EXPERTS), d ** -0.5),
        'moe_w_gate': nrm((N_MOE_LAYERS, N_EXPERTS, d, D_FF_EXPERT), d ** -0.5),
        'moe_w_up': nrm((N_MOE_LAYERS, N_EXPERTS, d, D_FF_EXPERT), d ** -0.5),
        'moe_w_down': nrm((N_MOE_LAYERS, N_EXPERTS, D_FF_EXPERT, d), DEEPNORM_BETA * D_FF_EXPERT ** -0.5),
        'ple_w_gate': nrm((DEPTH, d, d), d ** -0.5),
        'ple_w_proj': nrm((DEPTH, D_PLE, d), D_PLE ** -0.5),
    }


def reference(x_prompt, x_sample, p_prompt, p_sample, state_conv, state_rnn, cache_ckv, cache_kpe, page_table,
              ln_mix_g, ln_mix_b, ln_ffn_g, ln_ffn_b,
              rg_w_gate, rg_w_x, rg_conv_w, rg_conv_b, rg_w_a, rg_b_a, rg_w_i, rg_b_i, rg_lambda, rg_w_out,
              mla_w_q_a, mla_q_norm_g, mla_w_q_b, mla_w_o,
              kv_w_a, kv_norm_g, kv_w_uk, kv_w_uv,
              ffn_w_gate, ffn_w_up, ffn_w_down,
              moe_w_router, moe_w_gate, moe_w_up, moe_w_down,
              ple_w_gate, ple_w_proj):
    W = {
        'ln_mix_g': ln_mix_g, 'ln_mix_b': ln_mix_b, 'ln_ffn_g': ln_ffn_g, 'ln_ffn_b': ln_ffn_b,
        'rg_w_gate': rg_w_gate, 'rg_w_x': rg_w_x, 'rg_conv_w': rg_conv_w, 'rg_conv_b': rg_conv_b,
        'rg_w_a': rg_w_a, 'rg_b_a': rg_b_a, 'rg_w_i': rg_w_i, 'rg_b_i': rg_b_i, 'rg_lambda': rg_lambda,
        'rg_w_out': rg_w_out,
        'mla_w_q_a': mla_w_q_a, 'mla_q_norm_g': mla_q_norm_g, 'mla_w_q_b': mla_w_q_b, 'mla_w_o': mla_w_o,
        'kv_w_a': kv_w_a, 'kv_norm_g': kv_norm_g, 'kv_w_uk': kv_w_uk, 'kv_w_uv': kv_w_uv,
        'ffn_w_gate': ffn_w_gate, 'ffn_w_up': ffn_w_up, 'ffn_w_down': ffn_w_down,
        'moe_w_router': moe_w_router, 'moe_w_gate': moe_w_gate, 'moe_w_up': moe_w_up, 'moe_w_down': moe_w_down,
        'ple_w_gate': ple_w_gate, 'ple_w_proj': ple_w_proj,
    }
    bsz, seq = x_prompt.shape[0], x_prompt.shape[1]
    dec_s = x_sample.shape[1]
    past_len = page_table.shape[1] * cache_ckv.shape[1]
    conv0_p = jnp.zeros((N_A_LAYERS, bsz, CONV_WIDTH - 1, D_RNN), state_conv.dtype)
    rnn0_p = jnp.zeros((N_A_LAYERS, bsz, D_RNN), state_rnn.dtype)
    y_prompt, conv_p, rnn_p, ckv_p, kpe_p = run_trunk(
        x_prompt, p_prompt, conv0_p, rnn0_p, jnp.arange(seq, dtype=jnp.float32), latent_attention_prompt, W)

    def attend_sample(q_lat, q_pe, c_new, kpe_new):
        return latent_attention_sample(q_lat, q_pe, c_new, kpe_new, cache_ckv, cache_kpe, page_table)

    y_sample, conv_s, rnn_s, ckv_s, kpe_s = run_trunk(
        x_sample, p_sample, state_conv, state_rnn, past_len + jnp.arange(dec_s, dtype=jnp.float32), attend_sample, W)
    return (y_prompt, y_sample, conv_p, rnn_p, ckv_p, kpe_p, conv_s, rnn_s, ckv_s, kpe_s)
```

```python
import functools

import jax
import jax.numpy as jnp
from jax import lax
from jax.experimental import pallas as pl
from jax.experimental.pallas import tpu as pltpu

F32 = jnp.float32
BF16 = jnp.bfloat16

D_RNN = 1536
RNN_BLOCK = 128
CONV_WIDTH = 4
LRU_C = 8.0
N_HEADS = 16
KV_LORA_RANK = 256
QK_NOPE_DIM = 64
QK_ROPE_DIM = 32
V_HEAD_DIM = 64
ROPE_THETA = 10000.0
SOFTMAX_SCALE = (QK_NOPE_DIM + QK_ROPE_DIM) ** -0.5
N_EXPERTS = 8
LN_EPS = 1e-5
RMS_EPS = 1e-6
DEPTH = 2
DEEPNORM_ALPHA = (2.0 * DEPTH) ** 0.25

SUBLANES = 8
VMEM_LIMIT = 48 * 1024 * 1024


def _params(*sem):
    return pltpu.CompilerParams(dimension_semantics=sem, vmem_limit_bytes=VMEM_LIMIT)


def _dot(a, b):
    return jnp.dot(a, b, preferred_element_type=F32)


def _dot_nt(a, b):
    return lax.dot_general(a, b, (((1,), (1,)), ((), ())), preferred_element_type=F32)


def _layer_norm(z, g, b):
    mu = jnp.mean(z, -1, keepdims=True)
    d = z - mu
    var = jnp.mean(d * d, -1, keepdims=True)
    return d * lax.rsqrt(var + LN_EPS) * g + b


def _rms_norm(z, g):
    return z * lax.rsqrt(jnp.mean(z * z, -1, keepdims=True) + RMS_EPS) * g


def _full(shape):
    n = len(shape)
    return pl.BlockSpec(shape, lambda *_: (0,) * n)


def _rg_in_kernel(x_ref, wg_ref, wx_ref, gate_ref, u_ref):
    xb = x_ref[...].astype(BF16)
    gate_ref[...] = jax.nn.gelu(_dot(xb, wg_ref[...]))
    u_ref[...] = _dot(xb, wx_ref[...])


def _rg_in(x, wg, wx, tm):
    t, d = x.shape
    n = wg.shape[1]
    return pl.pallas_call(
        _rg_in_kernel,
        grid=(t // tm,),
        in_specs=[pl.BlockSpec((tm, d), lambda i: (i, 0)), _full((d, n)), _full((d, n))],
        out_specs=[pl.BlockSpec((tm, n), lambda i: (i, 0))] * 2,
        out_shape=[jax.ShapeDtypeStruct((t, n), F32)] * 2,
        compiler_params=_params("parallel"),
        name="rg_in",
    )(x, wg, wx)


def _rglru_gates(conv, wa_ref, ba, wi_ref, bi, lam):
    nblk = conv.shape[1] // RNN_BLOCK
    cb = conv.astype(BF16)
    ra = jnp.concatenate(
        [_dot(cb[:, n * RNN_BLOCK:(n + 1) * RNN_BLOCK], wa_ref[n]) for n in range(nblk)], axis=1)
    ia = jnp.concatenate(
        [_dot(cb[:, n * RNN_BLOCK:(n + 1) * RNN_BLOCK], wi_ref[n]) for n in range(nblk)], axis=1)
    r = jax.nn.sigmoid(ra + ba)
    i = jax.nn.sigmoid(ia + bi)
    z = -lam
    softplus = jnp.maximum(z, 0.0) + jnp.log1p(jnp.exp(-jnp.abs(z)))
    log_a = -LRU_C * r * softplus
    a = jnp.exp(log_a)
    b = jnp.sqrt(1.0 - jnp.exp(2.0 * log_a)) * (i * conv)
    return a, b


def _rglru_seq_kernel(gate_ref, u_ref, conv0_ref, h0_ref, cw_ref, cb_ref, wa_ref, ba_ref, wi_ref, bi_ref,
                      lam_ref, hg_ref, hlast_ref, ubuf, hcar, abuf, bbuf):
    c = pl.program_id(2)
    tc, db = u_ref.shape
    halo = SUBLANES

    @pl.when(c == 0)
    def _():
        ubuf[0:halo, :] = conv0_ref[0]
        hcar[...] = jnp.broadcast_to(h0_ref[0], hcar.shape)

    ubuf[halo:halo + tc, :] = u_ref[...]
    cw = cw_ref[...]
    conv = cb_ref[...]
    for k in range(CONV_WIDTH):
        off = halo - (CONV_WIDTH - 1) + k
        conv = conv + ubuf[off:off + tc, :] * cw[k:k + 1, :]
    ubuf[0:halo, :] = ubuf[tc:tc + halo, :]

    a, b = _rglru_gates(conv, wa_ref, ba_ref[...], wi_ref, bi_ref[...], lam_ref[...])

    row = lax.broadcasted_iota(jnp.int32, (tc, db), 0) & (SUBLANES - 1)
    shift = 1
    while shift < SUBLANES:
        a_prev = pltpu.roll(a, shift, 0)
        b_prev = pltpu.roll(b, shift, 0)
        keep = row >= shift
        b = jnp.where(keep, a * b_prev + b, b)
        a = jnp.where(keep, a * a_prev, a)
        shift *= 2
    abuf[...] = a
    bbuf[...] = b

    def group(g, h):
        off = pl.multiple_of(g * SUBLANES, SUBLANES)
        hb = abuf[pl.ds(off, SUBLANES), :] * h + bbuf[pl.ds(off, SUBLANES), :]
        bbuf[pl.ds(off, SUBLANES), :] = hb
        return jnp.broadcast_to(hb[SUBLANES - 1:SUBLANES, :], hb.shape)

    h = lax.fori_loop(0, tc // SUBLANES, group, hcar[...])
    hcar[...] = h
    hg_ref[...] = (bbuf[...] * gate_ref[...]).astype(BF16)

    @pl.when(c == pl.num_programs(2) - 1)
    def _():
        hlast_ref[0] = h[0:1, :]


def _rglru_seq(gate, u, conv0, h0, rw, nb, s, tc, db):
    t, d = u.shape
    nc = s // tc
    kb = db // RNN_BLOCK
    row = lambda b, j, c: (b * nc + c, j)
    vec = pl.BlockSpec((1, db), lambda b, j, c: (0, j))
    blk = pl.BlockSpec((kb, RNN_BLOCK, RNN_BLOCK), lambda b, j, c: (j, 0, 0))
    return pl.pallas_call(
        _rglru_seq_kernel,
        grid=(nb, d // db, nc),
        in_specs=[
            pl.BlockSpec((tc, db), row), pl.BlockSpec((tc, db), row),
            pl.BlockSpec((1, SUBLANES, db), lambda b, j, c: (b, 0, j)),
            pl.BlockSpec((1, 1, db), lambda b, j, c: (b, 0, j)),
            pl.BlockSpec((CONV_WIDTH, db), lambda b, j, c: (0, j)), vec, blk, vec, blk, vec, vec,
        ],
        out_specs=[pl.BlockSpec((tc, db), row), pl.BlockSpec((1, 1, db), lambda b, j, c: (b, 0, j))],
        out_shape=[jax.ShapeDtypeStruct((t, d), BF16), jax.ShapeDtypeStruct((nb, 1, d), F32)],
        scratch_shapes=[pltpu.VMEM((tc + SUBLANES, db), F32), pltpu.VMEM((SUBLANES, db), F32),
                        pltpu.VMEM((tc, db), F32), pltpu.VMEM((tc, db), F32)],
        compiler_params=_params("parallel", "parallel", "arbitrary"),
        name="rglru_seq",
    )(gate, u, conv0, h0, rw["conv_w"], rw["conv_b"], rw["w_a"], rw["b_a"], rw["w_i"], rw["b_i"], rw["lam"])


def _rglru_step_kernel(gate_ref, u_ref, conv0_ref, h0_ref, cw_ref, cb_ref, wa_ref, ba_ref, wi_ref, bi_ref,
                       lam_ref, hg_ref, hlast_ref):
    s, nb, db = u_ref.shape
    ue = jnp.concatenate([conv0_ref[...], u_ref[...]], axis=0)
    cw = cw_ref[...]
    conv = cb_ref[...][None]
    for k in range(CONV_WIDTH):
        conv = conv + ue[k:k + s] * cw[k:k + 1, :][None]
    a, b = _rglru_gates(conv.reshape(s * nb, db), wa_ref, ba_ref[...], wi_ref, bi_ref[...], lam_ref[...])
    a = a.reshape(s, nb, db)
    b = b.reshape(s, nb, db)
    h = h0_ref[...]
    for t in range(s):
        h = a[t] * h + b[t]
        hg_ref[t] = (h * gate_ref[t]).astype(BF16)
    hlast_ref[...] = h


def _rglru_step(gate, u, conv0, h0, rw, db):
    s, nb, d = u.shape
    kb = db // RNN_BLOCK
    cube = pl.BlockSpec((s, nb, db), lambda j: (0, 0, j))
    vec = pl.BlockSpec((1, db), lambda j: (0, j))
    blk = pl.BlockSpec((kb, RNN_BLOCK, RNN_BLOCK), lambda j: (j, 0, 0))
    return pl.pallas_call(
        _rglru_step_kernel,
        grid=(d // db,),
        in_specs=[cube, cube, pl.BlockSpec((CONV_WIDTH - 1, nb, db), lambda j: (0, 0, j)),
                  pl.BlockSpec((nb, db), lambda j: (0, j)),
                  pl.BlockSpec((CONV_WIDTH, db), lambda j: (0, j)), vec, blk, vec, blk, vec, vec],
        out_specs=[cube, pl.BlockSpec((nb, db), lambda j: (0, j))],
        out_shape=[jax.ShapeDtypeStruct((s, nb, d), BF16), jax.ShapeDtypeStruct((nb, d), F32)],
        compiler_params=_params("parallel"),
        name="rglru_step",
    )(gate, u, conv0, h0, rw["conv_w"], rw["conv_b"], rw["w_a"], rw["b_a"], rw["w_i"], rw["b_i"], rw["lam"])


def _mm_ln_kernel(x_ref, w_ref, res_ref, g_ref, b_ref, o_ref):
    y = _dot(x_ref[...], w_ref[...])
    o_ref[...] = _layer_norm(DEEPNORM_ALPHA * res_ref[...] + y, g_ref[...], b_ref[...])


def _mm_ln(x, w, res, g, b, tm):
    t, k = x.shape
    n = w.shape[1]
    return pl.pallas_call(
        _mm_ln_kernel,
        grid=(t // tm,),
        in_specs=[pl.BlockSpec((tm, k), lambda i: (i, 0)), _full((k, n)),
                  pl.BlockSpec((tm, n), lambda i: (i, 0)), _full((1, n)), _full((1, n))],
        out_specs=pl.BlockSpec((tm, n), lambda i: (i, 0)),
        out_shape=jax.ShapeDtypeStruct((t, n), F32),
        compiler_params=_params("parallel"),
        name="mm_ln",
    )(x, w, res, g, b)


def _ffn_kernel(te_ref, x_ref, wg_ref, wu_ref, wd_ref, *rest, post_norm):
    del te_ref
    if post_norm:
        g_ref, b_ref, o_ref, xb_ref, acc_ref = rest
    else:
        o_ref, xb_ref, acc_ref = rest
    c = pl.program_id(1)

    @pl.when(c == 0)
    def _():
        xb_ref[...] = x_ref[...].astype(BF16)
        acc_ref[...] = jnp.zeros_like(acc_ref)

    xb = xb_ref[...]
    h = (jax.nn.silu(_dot(xb, wg_ref[0])) * _dot(xb, wu_ref[0])).astype(BF16)
    acc_ref[...] += _dot(h, wd_ref[0])

    @pl.when(c == pl.num_programs(1) - 1)
    def _():
        if post_norm:
            o_ref[...] = _layer_norm(DEEPNORM_ALPHA * x_ref[...] + acc_ref[...], g_ref[...], b_ref[...])
        else:
            o_ref[...] = acc_ref[...]


def _ffn(tile_expert, x, wg, wu, wd, norm, tm, tf):
    r, d = x.shape
    f = wg.shape[2]
    in_specs = [
        pl.BlockSpec((tm, d), lambda i, c, te: (i, 0)),
        pl.BlockSpec((1, d, tf), lambda i, c, te: (te[i], 0, c)),
        pl.BlockSpec((1, d, tf), lambda i, c, te: (te[i], 0, c)),
        pl.BlockSpec((1, tf, d), lambda i, c, te: (te[i], c, 0)),
    ]
    args = [x, wg, wu, wd]
    if norm is not None:
        in_specs += [pl.BlockSpec((1, d), lambda i, c, te: (0, 0))] * 2
        args += list(norm)
    return pl.pallas_call(
        functools.partial(_ffn_kernel, post_norm=norm is not None),
        grid_spec=pltpu.PrefetchScalarGridSpec(
            num_scalar_prefetch=1,
            grid=(r // tm, f // tf),
            in_specs=in_specs,
            out_specs=pl.BlockSpec((tm, d), lambda i, c, te: (i, 0)),
            scratch_shapes=[pltpu.VMEM((tm, d), BF16), pltpu.VMEM((tm, d), F32)],
        ),
        out_shape=jax.ShapeDtypeStruct((r, d), F32),
        compiler_params=_params("parallel", "arbitrary"),
        name="ffn",
    )(tile_expert, *args)


def _ple_kernel(x_ref, p_ref, wg_ref, wp_ref, o_ref):
    x = x_ref[...]
    a = _dot(x.astype(BF16), wg_ref[...])
    c = _dot(p_ref[...].astype(BF16), wp_ref[...])
    o_ref[...] = x + jax.nn.sigmoid(a) * c


def _ple(x, p, wg, wp, tm):
    t, d = x.shape
    dp = p.shape[1]
    return pl.pallas_call(
        _ple_kernel,
        grid=(t // tm,),
        in_specs=[pl.BlockSpec((tm, d), lambda i: (i, 0)), pl.BlockSpec((tm, dp), lambda i: (i, 0)),
                  _full((d, d)), _full((dp, d))],
        out_specs=pl.BlockSpec((tm, d), lambda i: (i, 0)),
        out_shape=jax.ShapeDtypeStruct((t, d), F32),
        compiler_params=_params("parallel"),
        name="ple",
    )(x, p, wg, wp)


def _kva_kernel(x_ref, wc_ref, wr_ref, wrs_ref, g_ref, cos_ref, sin_ref, ckv_ref, kpe_ref):
    xb = x_ref[...].astype(BF16)
    ckv_ref[...] = _rms_norm(_dot(xb, wc_ref[...]), g_ref[...])
    kpe_ref[...] = _dot(xb, wr_ref[...]) * cos_ref[...] + _dot(xb, wrs_ref[...]) * sin_ref[...]


def _kva(x, wc, wr, wrs, g, cos, sin, tm):
    t, d = x.shape
    nper = cos.shape[0] // tm
    tab = pl.BlockSpec((tm, QK_ROPE_DIM), lambda i: (i % nper, 0))
    return pl.pallas_call(
        _kva_kernel,
        grid=(t // tm,),
        in_specs=[pl.BlockSpec((tm, d), lambda i: (i, 0)), _full(wc.shape), _full(wr.shape), _full(wrs.shape),
                  _full(g.shape), tab, tab],
        out_specs=[pl.BlockSpec((tm, KV_LORA_RANK), lambda i: (i, 0)),
                   pl.BlockSpec((tm, QK_ROPE_DIM), lambda i: (i, 0))],
        out_shape=[jax.ShapeDtypeStruct((t, KV_LORA_RANK), F32), jax.ShapeDtypeStruct((t, QK_ROPE_DIM), F32)],
        compiler_params=_params("parallel"),
        name="kv_latent",
    )(x, wc, wr, wrs, g, cos, sin)


def _q_kernel(x_ref, wqa_ref, qg_ref, wqb_ref, wuk_ref, cos_ref, sin_ref, qlat_ref, qpe_ref):
    nope = N_HEADS * QK_NOPE_DIM
    pe = N_HEADS * QK_ROPE_DIM
    cq = _rms_norm(_dot(x_ref[...].astype(BF16), wqa_ref[...]), qg_ref[...])
    q = _dot(cq.astype(BF16), wqb_ref[...])
    q_pe = q[:, nope:nope + pe] * cos_ref[...] + q[:, nope + pe:nope + 2 * pe] * sin_ref[...]
    qpe_ref[...] = (q_pe * SOFTMAX_SCALE).astype(BF16)
    qn = q[:, :nope].astype(BF16)
    pair = 2 * QK_NOPE_DIM
    wide = 2 * KV_LORA_RANK
    for j in range(N_HEADS // 2):
        ql = _dot(qn[:, j * pair:(j + 1) * pair], wuk_ref[j])
        qlat_ref[:, j * wide:(j + 1) * wide] = (ql * SOFTMAX_SCALE).astype(BF16)


def _q_proj(x, wqa, qg, wqb, wuk, cos, sin, tm):
    t, d = x.shape
    nper = cos.shape[0] // tm
    pe = N_HEADS * QK_ROPE_DIM
    lat = N_HEADS * KV_LORA_RANK
    tab = pl.BlockSpec((tm, pe), lambda i: (i % nper, 0))
    return pl.pallas_call(
        _q_kernel,
        grid=(t // tm,),
        in_specs=[pl.BlockSpec((tm, d), lambda i: (i, 0)), _full(wqa.shape), _full(qg.shape), _full(wqb.shape),
                  _full(wuk.shape), tab, tab],
        out_specs=[pl.BlockSpec((tm, lat), lambda i: (i, 0)), pl.BlockSpec((tm, pe), lambda i: (i, 0))],
        out_shape=[jax.ShapeDtypeStruct((t, lat), BF16), jax.ShapeDtypeStruct((t, pe), BF16)],
        compiler_params=_params("parallel"),
        name="q_proj",
    )(x, wqa, qg, wqb, wuk, cos, sin)


def _attn_kernel(qlat_ref, qpe_ref, ckv_ref, kpe_ref, o_ref, qs, qps, m_s, l_s, acc, *, tq, tk):
    i = pl.program_id(1)
    j = pl.program_id(2)
    c = KV_LORA_RANK
    r = QK_ROPE_DIM

    @pl.when(j == 0)
    def _():
        for h in range(N_HEADS):
            qs[h * tq:(h + 1) * tq, :] = qlat_ref[:, h * c:(h + 1) * c]
            qps[h * tq:(h + 1) * tq, :] = qpe_ref[:, h * r:(h + 1) * r]
        m_s[...] = jnp.full_like(m_s, -jnp.inf)
        l_s[...] = jnp.zeros_like(l_s)
        acc[...] = jnp.zeros_like(acc)

    @pl.when(j * tk <= i * tq + tq - 1)
    def _():
        kc = ckv_ref[...].astype(BF16)
        kp = kpe_ref[...].astype(BF16)
        s = _dot_nt(qs[...], kc) + _dot_nt(qps[...], kp)
        q_pos = i * tq + (lax.broadcasted_iota(jnp.int32, s.shape, 0) & (tq - 1))
        k_pos = j * tk + lax.broadcasted_iota(jnp.int32, s.shape, 1)
        s = jnp.where(k_pos <= q_pos, s, -jnp.inf)
        m_new = jnp.maximum(m_s[...], jnp.max(s, -1, keepdims=True))
        corr = jnp.exp(m_s[...] - m_new)
        p = jnp.exp(s - m_new)
        l_s[...] = l_s[...] * corr + jnp.sum(p, -1, keepdims=True)
        acc[...] = acc[...] * corr + _dot(p.astype(BF16), kc)
        m_s[...] = m_new

    @pl.when(j == pl.num_programs(2) - 1)
    def _():
        o = acc[...] / l_s[...]
        for h in range(N_HEADS):
            o_ref[:, h * c:(h + 1) * c] = o[h * tq:(h + 1) * tq, :].astype(BF16)


def _attn_prompt(qlat, qpe, ckv, kpe, nb, s, tq, tk):
    t = nb * s
    nq, nk = s // tq, s // tk
    assert tq & (tq - 1) == 0
    kv_row = lambda b, i, j: (b * nk + jnp.minimum(j, (i * tq + tq - 1) // tk), 0)
    q_row = lambda b, i, j: (b * nq + i, 0)
    rows = N_HEADS * tq
    return pl.pallas_call(
        functools.partial(_attn_kernel, tq=tq, tk=tk),
        grid=(nb, nq, nk),
        in_specs=[pl.BlockSpec((tq, qlat.shape[1]), q_row), pl.BlockSpec((tq, qpe.shape[1]), q_row),
                  pl.BlockSpec((tk, KV_LORA_RANK), kv_row), pl.BlockSpec((tk, QK_ROPE_DIM), kv_row)],
        out_specs=pl.BlockSpec((tq, qlat.shape[1]), q_row),
        out_shape=jax.ShapeDtypeStruct((t, qlat.shape[1]), BF16),
        scratch_shapes=[pltpu.VMEM((rows, KV_LORA_RANK), BF16), pltpu.VMEM((rows, QK_ROPE_DIM), BF16),
                        pltpu.VMEM((rows, 1), F32), pltpu.VMEM((rows, 1), F32),
                        pltpu.VMEM((rows, KV_LORA_RANK), F32)],
        compiler_params=_params("parallel", "parallel", "arbitrary"),
        name="attn_prompt",
    )(qlat, qpe, ckv, kpe)


def _attn_paged_kernel(pt_ref, qlat_ref, qpe_ref, cnew_ref, knew_ref, *rest, pages, s_new):
    del pt_ref
    ck_refs = rest[:pages]
    kp_refs = rest[pages:2 * pages]
    o_ref, m_s, l_s, acc = rest[2 * pages:]
    c = pl.program_id(1)
    q = qlat_ref[0]
    qp = qpe_ref[0]

    @pl.when(c == 0)
    def _():
        pad = 2 * SUBLANES - s_new
        cn = jnp.concatenate([cnew_ref[0], jnp.zeros((pad, KV_LORA_RANK), F32)], axis=0).astype(BF16)
        kn = jnp.concatenate([knew_ref[0], jnp.zeros((pad, QK_ROPE_DIM), F32)], axis=0).astype(BF16)
        s = _dot_nt(q, cn) + _dot_nt(qp, kn)
        tok = lax.shift_right_logical(lax.broadcasted_iota(jnp.int32, s.shape, 0), N_HEADS.bit_length() - 1)
        key = lax.broadcasted_iota(jnp.int32, s.shape, 1)
        s = jnp.where(key <= tok, s, -jnp.inf)
        m = jnp.max(s, -1, keepdims=True)
        p = jnp.exp(s - m)
        m_s[...] = m
        l_s[...] = jnp.sum(p, -1, keepdims=True)
        acc[...] = _dot(p.astype(BF16), cn)

    cks = [ck_refs[k][0].astype(BF16) for k in range(pages)]
    s = jnp.concatenate(
        [_dot_nt(q, cks[k]) + _dot_nt(qp, kp_refs[k][0].astype(BF16)) for k in range(pages)], axis=1)
    m_new = jnp.maximum(m_s[...], jnp.max(s, -1, keepdims=True))
    corr = jnp.exp(m_s[...] - m_new)
    p = jnp.exp(s - m_new)
    l_s[...] = l_s[...] * corr + jnp.sum(p, -1, keepdims=True)
    p = p.astype(BF16)
    page = cks[0].shape[0]
    pv = _dot(p[:, :page], cks[0])
    for k in range(1, pages):
        pv = pv + _dot(p[:, k * page:(k + 1) * page], cks[k])
    acc[...] = acc[...] * corr + pv
    m_s[...] = m_new

    @pl.when(c == pl.num_programs(1) - 1)
    def _():
        o_ref[0] = (acc[...] / l_s[...]).astype(BF16)


def _attn_paged(qlat, qpe, cnew, knew, cache_ckv, cache_kpe, page_table, pages):
    nb, rows, c = qlat.shape
    s_new = cnew.shape[1]
    n_pages = page_table.shape[1]
    page = cache_ckv.shape[1]
    per_b = lambda b, j, pt: (b, 0, 0)

    def page_spec(width, k):
        return pl.BlockSpec((1, page, width), lambda b, j, pt: (pt[b * n_pages + j * pages + k], 0, 0))

    in_specs = [pl.BlockSpec((1, rows, c), per_b), pl.BlockSpec((1, rows, QK_ROPE_DIM), per_b),
                pl.BlockSpec((1, s_new, c), per_b), pl.BlockSpec((1, s_new, QK_ROPE_DIM), per_b)]
    in_specs += [page_spec(c, k) for k in range(pages)]
    in_specs += [page_spec(QK_ROPE_DIM, k) for k in range(pages)]
    return pl.pallas_call(
        functools.partial(_attn_paged_kernel, pages=pages, s_new=s_new),
        grid_spec=pltpu.PrefetchScalarGridSpec(
            num_scalar_prefetch=1,
            grid=(nb, n_pages // pages),
            in_specs=in_specs,
            out_specs=pl.BlockSpec((1, rows, c), per_b),
            scratch_shapes=[pltpu.VMEM((rows, 1), F32), pltpu.VMEM((rows, 1), F32), pltpu.VMEM((rows, c), F32)],
        ),
        out_shape=jax.ShapeDtypeStruct((nb, rows, c), BF16),
        compiler_params=_params("parallel", "arbitrary"),
        name="attn_paged",
    )(page_table.reshape(-1), qlat, qpe, cnew, knew, *([cache_ckv] * pages), *([cache_kpe] * pages))


def _mla_out_kernel(o_ref, wuv_ref, wo_ref, res_ref, g_ref, b_ref, out_ref):
    wide = 2 * KV_LORA_RANK
    o = o_ref[...]
    v = jnp.concatenate(
        [_dot(o[:, j * wide:(j + 1) * wide], wuv_ref[j]) for j in range(N_HEADS // 2)], axis=1).astype(BF16)
    y = _dot(v, wo_ref[...])
    out_ref[...] = _layer_norm(DEEPNORM_ALPHA * res_ref[...] + y, g_ref[...], b_ref[...])


def _mla_out(o, wuv, wo, res, g, b, tm):
    t, k = o.shape
    n = wo.shape[1]
    return pl.pallas_call(
        _mla_out_kernel,
        grid=(t // tm,),
        in_specs=[pl.BlockSpec((tm, k), lambda i: (i, 0)), _full(wuv.shape), _full(wo.shape),
                  pl.BlockSpec((tm, n), lambda i: (i, 0)), _full((1, n)), _full((1, n))],
        out_specs=pl.BlockSpec((tm, n), lambda i: (i, 0)),
        out_shape=jax.ShapeDtypeStruct((t, n), F32),
        compiler_params=_params("parallel"),
        name="mla_out",
    )(o, wuv, wo, res, g, b)


def _router_kernel(x_ref, w_ref, idx_ref, wt_ref):
    x = x_ref[...]
    w = w_ref[...]
    xh = x.astype(BF16)
    xl = (x - xh.astype(F32)).astype(BF16)
    wh = w.astype(BF16)
    wl = (w - wh.astype(F32)).astype(BF16)
    lg = (_dot_nt(wh, xh) + _dot_nt(wh, xl) + _dot_nt(wl, xh))[:N_EXPERTS]
    e = lax.broadcasted_iota(jnp.int32, lg.shape, 0).astype(F32)
    none = float(N_EXPERTS)
    v1 = jnp.max(lg, 0, keepdims=True)
    i1 = jnp.min(jnp.where(lg == v1, e, none), 0, keepdims=True)
    lg2 = jnp.where(e == i1, -jnp.inf, lg)
    v2 = jnp.max(lg2, 0, keepdims=True)
    i2 = jnp.min(jnp.where(lg2 == v2, e, none), 0, keepdims=True)
    ex = jnp.exp(v2 - v1)
    den = 1.0 + ex
    idx_ref[...] = jnp.concatenate([i1, i2], axis=0).astype(jnp.int32)
    wt_ref[...] = jnp.concatenate([1.0 / den, ex / den], axis=0)


def _router(x, w_t, tm):
    t, d = x.shape
    return pl.pallas_call(
        _router_kernel,
        grid=(t // tm,),
        in_specs=[pl.BlockSpec((tm, d), lambda i: (i, 0)), _full(w_t.shape)],
        out_specs=[pl.BlockSpec((2, tm), lambda i: (0, i))] * 2,
        out_shape=[jax.ShapeDtypeStruct((2, t), jnp.int32), jax.ShapeDtypeStruct((2, t), F32)],
        compiler_params=_params("parallel"),
        name="router",
    )(x, w_t)


def _dispatch_kernel(tok_ref, x_hbm, o_ref, sem):
    rows = o_ref.shape[0]
    base = pl.program_id(0) * rows

    def issue(r, carry):
        t = tok_ref[base + r]
        pltpu.make_async_copy(x_hbm.at[pl.ds(t, 1), :], o_ref.at[pl.ds(r, 1), :], sem).start()
        return carry

    lax.fori_loop(0, rows, issue, 0, unroll=8)
    pltpu.make_async_copy(x_hbm.at[pl.ds(0, rows), :], o_ref, sem).wait()


def _dispatch(slot_token, x, rows):
    ns = slot_token.shape[0]
    d = x.shape[1]
    return pl.pallas_call(
        _dispatch_kernel,
        grid_spec=pltpu.PrefetchScalarGridSpec(
            num_scalar_prefetch=1,
            grid=(ns // rows,),
            in_specs=[pl.BlockSpec(memory_space=pl.ANY)],
            out_specs=pl.BlockSpec((rows, d), lambda i, tok: (i, 0)),
            scratch_shapes=[pltpu.SemaphoreType.DMA],
        ),
        out_shape=jax.ShapeDtypeStruct((ns, d), x.dtype),
        compiler_params=_params("arbitrary"),
        name="moe_dispatch",
    )(slot_token, x)


def _combine_kernel(pos_ref, y_hbm, x_ref, w1_ref, w2_ref, g_ref, b_ref, o_ref, y1, y2, sem):
    rows = o_ref.shape[0]
    t = pl.num_programs(0) * rows
    base = pl.program_id(0) * rows

    def issue(r, carry):
        p1 = pos_ref[base + r]
        p2 = pos_ref[t + base + r]
        pltpu.make_async_copy(y_hbm.at[pl.ds(p1, 1), :], y1.at[pl.ds(r, 1), :], sem.at[0]).start()
        pltpu.make_async_copy(y_hbm.at[pl.ds(p2, 1), :], y2.at[pl.ds(r, 1), :], sem.at[1]).start()
        return carry

    lax.fori_loop(0, rows, issue, 0, unroll=8)
    pltpu.make_async_copy(y_hbm.at[pl.ds(0, rows), :], y1, sem.at[0]).wait()
    pltpu.make_async_copy(y_hbm.at[pl.ds(0, rows), :], y2, sem.at[1]).wait()
    ff = w1_ref[...] * y1[...] + w2_ref[...] * y2[...]
    o_ref[...] = _layer_norm(DEEPNORM_ALPHA * x_ref[...] + ff, g_ref[...], b_ref[...])


def _combine(pos, y, x, w1, w2, g, b, rows):
    t, d = x.shape
    row = pl.BlockSpec((rows, d), lambda i, p: (i, 0))
    col = pl.BlockSpec((rows, 1), lambda i, p: (i, 0))
    vec = pl.BlockSpec((1, d), lambda i, p: (0, 0))
    return pl.pallas_call(
        _combine_kernel,
        grid_spec=pltpu.PrefetchScalarGridSpec(
            num_scalar_prefetch=1,
            grid=(t // rows,),
            in_specs=[pl.BlockSpec(memory_space=pl.ANY), row, col, col, vec, vec],
            out_specs=row,
            scratch_shapes=[pltpu.VMEM((rows, d), F32), pltpu.VMEM((rows, d), F32),
                            pltpu.SemaphoreType.DMA((2,))],
        ),
        out_shape=jax.ShapeDtypeStruct((t, d), F32),
        compiler_params=_params("arbitrary"),
        name="moe_combine",
    )(pos, y, x, w1, w2, g, b)


def _moe(x, w_router_t, wg, wu, wd, g, b, tm, tile, tf):
    t, d = x.shape
    idx, wts = _router(x, w_router_t, tm)
    e_flat = idx.reshape(-1)
    onehot = (e_flat[:, None] == jnp.arange(N_EXPERTS, dtype=jnp.int32)[None, :]).astype(jnp.int32)
    csum = jnp.cumsum(onehot, axis=0)
    rank = jnp.sum(csum * onehot, axis=1) - 1
    counts = csum[-1]
    padded = ((counts + tile - 1) // tile) * tile
    ends = jnp.cumsum(padded)
    starts = ends - padded
    pos = (starts[e_flat] + rank).astype(jnp.int32)
    n_slots = 2 * t + N_EXPERTS * tile
    token = jnp.arange(2 * t, dtype=jnp.int32) % t
    slot_token = jnp.zeros((n_slots,), jnp.int32).at[pos].set(token)
    tile_start = jnp.arange(n_slots // tile, dtype=jnp.int32) * tile
    tile_expert = jnp.minimum(jnp.searchsorted(ends, tile_start, side="right"), N_EXPERTS - 1).astype(jnp.int32)

    xs = _dispatch(slot_token, x, tile)
    ys = _ffn(tile_expert, xs, wg, wu, wd, None, tile, tf)
    return _combine(pos, ys, x, wts[0][:, None], wts[1][:, None], g, b, tm)


def _swap_halves(w):
    half = w.shape[-1] // 2
    return jnp.concatenate([w[..., half:], w[..., :half]], axis=-1)


def _prep_weights(w):
    out = {}
    row = lambda v: v.reshape(1, -1).astype(F32)
    out["ln_mix"] = [(row(w["ln_mix_g"][i]), row(w["ln_mix_b"][i])) for i in range(DEPTH)]
    out["ln_ffn"] = [(row(w["ln_ffn_g"][i]), row(w["ln_ffn_b"][i])) for i in range(DEPTH)]
    out["rg_wg"] = w["rg_w_gate"][0].astype(BF16)
    out["rg_wx"] = w["rg_w_x"][0].astype(BF16)
    out["rg"] = {
        "conv_w": w["rg_conv_w"][0], "conv_b": row(w["rg_conv_b"][0]),
        "w_a": w["rg_w_a"][0].astype(BF16), "b_a": row(w["rg_b_a"][0]),
        "w_i": w["rg_w_i"][0].astype(BF16), "b_i": row(w["rg_b_i"][0]),
        "lam": row(w["rg_lambda"][0]),
    }
    out["rg_wout"] = w["rg_w_out"][0].astype(BF16)
    out["ffn"] = (w["ffn_w_gate"].astype(BF16), w["ffn_w_up"].astype(BF16), w["ffn_w_down"].astype(BF16))
    out["moe"] = (w["moe_w_gate"][0].astype(BF16), w["moe_w_up"][0].astype(BF16), w["moe_w_down"][0].astype(BF16))
    e = w["moe_w_router"].shape[-1]
    out["router_t"] = jnp.concatenate(
        [w["moe_w_router"][0].T, jnp.zeros((2 * SUBLANES - e, w["moe_w_router"].shape[1]), F32)], axis=0)
    out["ple_wg"] = w["ple_w_gate"].astype(BF16)
    out["ple_wp"] = w["ple_w_proj"].astype(BF16)
    kv = w["kv_w_a"]
    out["kv_wc"] = kv[:, :KV_LORA_RANK].astype(BF16)
    out["kv_wr"] = kv[:, KV_LORA_RANK:].astype(BF16)
    out["kv_wrs"] = _swap_halves(kv[:, KV_LORA_RANK:]).astype(BF16)
    out["kv_g"] = row(w["kv_norm_g"])
    out["q_wa"] = w["mla_w_q_a"][0].astype(BF16)
    out["q_g"] = row(w["mla_q_norm_g"][0])
    qb = w["mla_w_q_b"][0].reshape(-1, N_HEADS, QK_NOPE_DIM + QK_ROPE_DIM)
    lora = qb.shape[0]
    q_nope = qb[:, :, :QK_NOPE_DIM].reshape(lora, -1)
    q_rope = qb[:, :, QK_NOPE_DIM:]
    out["q_wb"] = jnp.concatenate(
        [q_nope, q_rope.reshape(lora, -1), _swap_halves(q_rope).reshape(lora, -1)], axis=1).astype(BF16)
    uk = jnp.transpose(w["kv_w_uk"], (1, 2, 0))
    zk = jnp.zeros_like(uk[0])
    out["wuk"] = jnp.stack([
        jnp.concatenate([jnp.concatenate([uk[2 * j], zk], axis=1), jnp.concatenate([zk, uk[2 * j + 1]], axis=1)],
                        axis=0) for j in range(N_HEADS // 2)]).astype(BF16)
    uv = jnp.transpose(w["kv_w_uv"], (1, 0, 2))
    zv = jnp.zeros_like(uv[0])
    out["wuv"] = jnp.stack([
        jnp.concatenate([jnp.concatenate([uv[2 * j], zv], axis=1), jnp.concatenate([zv, uv[2 * j + 1]], axis=1)],
                        axis=0) for j in range(N_HEADS // 2)]).astype(BF16)
    out["wo"] = w["mla_w_o"][0].astype(BF16)
    return out


def _rope_tables(pos, repeat):
    inv = ROPE_THETA ** (-jnp.arange(0, QK_ROPE_DIM, 2, dtype=F32) / QK_ROPE_DIM)
    ang = pos[:, None] * inv[None, :]
    cos, sin = jnp.cos(ang), jnp.sin(ang)
    cos_k = jnp.tile(jnp.concatenate([cos, cos], axis=-1), (repeat, 1))
    sin_k = jnp.tile(jnp.concatenate([-sin, sin], axis=-1), (repeat, 1))
    return cos_k, sin_k, jnp.tile(cos_k, (1, N_HEADS)), jnp.tile(sin_k, (1, N_HEADS))


def _tile_rows(t, want):
    tm = min(want, t)
    assert t % tm == 0
    return tm


def _trunk(x3, p4, conv0, rnn0, pos, pw, paged):
    nb, s, d = x3.shape
    t = nb * s
    x = x3.reshape(t, d)
    p = p4.reshape(DEPTH, t, -1)
    tm = _tile_rows(t, 512)
    z = jnp.zeros((1,), jnp.int32)

    gate, u = _rg_in(x, pw["rg_wg"], pw["rg_wx"], tm)
    if paged is None:
        conv_pad = jnp.concatenate(
            [jnp.zeros((nb, SUBLANES - (CONV_WIDTH - 1), D_RNN), F32), conv0], axis=1)
        hg, h_last = _rglru_seq(gate, u, conv_pad, rnn0[:, None, :], pw["rg"], nb, s, min(s, 256), 512)
        h_last = h_last[:, 0, :]
        tabs = _rope_tables(pos, 1)
    else:
        to_tm = lambda a: jnp.transpose(a.reshape(nb, s, -1), (1, 0, 2))
        hg, h_last = _rglru_step(to_tm(gate), to_tm(u), jnp.transpose(conv0, (1, 0, 2)), rnn0, pw["rg"], 512)
        hg = jnp.transpose(hg, (1, 0, 2)).reshape(t, -1)
        tabs = _rope_tables(pos, nb)
    conv_state = u.reshape(nb, s, -1)[:, s - (CONV_WIDTH - 1):, :]
    x = _mm_ln(hg, pw["rg_wout"], x, *pw["ln_mix"][0], tm)
    x = _ffn(jnp.zeros((t // tm,), jnp.int32), x, *pw["ffn"], pw["ln_ffn"][0], tm, 256)
    x = _ple(x, p[0], pw["ple_wg"][0], pw["ple_wp"][0], tm)
    cos_k, sin_k, cos_q, sin_q = tabs
    ckv, kpe = _kva(x, pw["kv_wc"], pw["kv_wr"], pw["kv_wrs"], pw["kv_g"], cos_k, sin_k, tm)

    tq = _tile_rows(t, 256)
    qlat, qpe = _q_proj(x, pw["q_wa"], pw["q_g"], pw["q_wb"], pw["wuk"], cos_q, sin_q, tq)
    if paged is None:
        o = _attn_prompt(qlat, qpe, ckv, kpe, nb, s, 256, 256)
    else:
        cache_ckv, cache_kpe, page_table = paged
        o = _attn_paged(qlat.reshape(nb, s * N_HEADS, KV_LORA_RANK), qpe.reshape(nb, s * N_HEADS, QK_ROPE_DIM),
                        ckv.reshape(nb, s, -1), kpe.reshape(nb, s, -1), cache_ckv, cache_kpe, page_table, 16)
        o = o.reshape(t, -1)
    x = _mla_out(o, pw["wuv"], pw["wo"], x, *pw["ln_mix"][1], tq)
    x = _moe(x, pw["router_t"], *pw["moe"], *pw["ln_ffn"][1], tq, 512 if paged is None else 256, 512)
    x = _ple(x, p[1], pw["ple_wg"][1], pw["ple_wp"][1], tm)
    del z
    return (x.reshape(nb, s, d), conv_state[None], h_last[None], ckv.reshape(nb, s, -1), kpe.reshape(nb, s, -1))


def kernel(x_prompt, x_sample, p_prompt, p_sample, state_conv, state_rnn, cache_ckv, cache_kpe, page_table, ln_mix_g, ln_mix_b, ln_ffn_g, ln_ffn_b, rg_w_gate, rg_w_x, rg_conv_w, rg_conv_b, rg_w_a, rg_b_a, rg_w_i, rg_b_i, rg_lambda, rg_w_out, mla_w_q_a, mla_q_norm_g, mla_w_q_b, mla_w_o, kv_w_a, kv_norm_g, kv_w_uk, kv_w_uv, ffn_w_gate, ffn_w_up, ffn_w_down, moe_w_router, moe_w_gate, moe_w_up, moe_w_down, ple_w_gate, ple_w_proj):
    w = dict(
        ln_mix_g=ln_mix_g, ln_mix_b=ln_mix_b, ln_ffn_g=ln_ffn_g, ln_ffn_b=ln_ffn_b,
        rg_w_gate=rg_w_gate, rg_w_x=rg_w_x, rg_conv_w=rg_conv_w, rg_conv_b=rg_conv_b, rg_w_a=rg_w_a,
        rg_b_a=rg_b_a, rg_w_i=rg_w_i, rg_b_i=rg_b_i, rg_lambda=rg_lambda, rg_w_out=rg_w_out,
        mla_w_q_a=mla_w_q_a, mla_q_norm_g=mla_q_norm_g, mla_w_q_b=mla_w_q_b, mla_w_o=mla_w_o,
        kv_w_a=kv_w_a, kv_norm_g=kv_norm_g, kv_w_uk=kv_w_uk, kv_w_uv=kv_w_uv,
        ffn_w_gate=ffn_w_gate, ffn_w_up=ffn_w_up, ffn_w_down=ffn_w_down,
        moe_w_router=moe_w_router, moe_w_gate=moe_w_gate, moe_w_up=moe_w_up, moe_w_down=moe_w_down,
        ple_w_gate=ple_w_gate, ple_w_proj=ple_w_proj)
    pw = _prep_weights(w)
    nb, s = x_prompt.shape[:2]
    dec_s = x_sample.shape[1]
    past_len = page_table.shape[1] * cache_ckv.shape[1]
    n_a = state_conv.shape[0]
    assert n_a == 1
    conv0_p = jnp.zeros((nb, CONV_WIDTH - 1, D_RNN), state_conv.dtype)
    rnn0_p = jnp.zeros((nb, D_RNN), state_rnn.dtype)
    y_p, conv_p, rnn_p, ckv_p, kpe_p = _trunk(
        x_prompt, p_prompt, conv0_p, rnn0_p, jnp.arange(s, dtype=F32), pw, None)
    y_s, conv_s, rnn_s, ckv_s, kpe_s = _trunk(
        x_sample, p_sample, state_conv[0], state_rnn[0], past_len + jnp.arange(dec_s, dtype=F32), pw,
        (cache_ckv, cache_kpe, page_table))
    return (y_p, y_s, conv_p, rnn_p, ckv_p, kpe_p, conv_s, rnn_s, ckv_s, kpe_s)
```

```python
import functools

import jax
import jax.numpy as jnp
from jax import lax
from jax.experimental import pallas as pl
from jax.experimental.pallas import tpu as pltpu

F32 = jnp.float32
BF16 = jnp.bfloat16

D_RNN = 1536
RNN_BLOCK = 128
CONV_WIDTH = 4
LRU_C = 8.0
N_HEADS = 16
KV_LORA_RANK = 256
QK_NOPE_DIM = 64
QK_ROPE_DIM = 32
V_HEAD_DIM = 64
ROPE_THETA = 10000.0
SOFTMAX_SCALE = (QK_NOPE_DIM + QK_ROPE_DIM) ** -0.5
N_EXPERTS = 8
LN_EPS = 1e-5
RMS_EPS = 1e-6
DEPTH = 2
DEEPNORM_ALPHA = (2.0 * DEPTH) ** 0.25

QK_WIDTH = KV_LORA_RANK + QK_ROPE_DIM
QK_SCALE = SOFTMAX_SCALE * 1.4426950408889634

SUBLANES = 8
LANES = 128
VMEM_LIMIT = 48 * 1024 * 1024


def _params(*sem):
    return pltpu.CompilerParams(dimension_semantics=sem, vmem_limit_bytes=VMEM_LIMIT)


def _dot(a, b):
    return jnp.dot(a, b, preferred_element_type=F32)


def _dot_nt(a, b):
    return lax.dot_general(a, b, (((1,), (1,)), ((), ())), preferred_element_type=F32)


def _layer_norm(z, g, b):
    mu = jnp.mean(z, -1, keepdims=True)
    d = z - mu
    var = jnp.mean(d * d, -1, keepdims=True)
    return d * lax.rsqrt(var + LN_EPS) * g + b


def _rms_norm(z, g):
    return z * lax.rsqrt(jnp.mean(z * z, -1, keepdims=True) + RMS_EPS) * g


def _full(shape):
    n = len(shape)
    return pl.BlockSpec(shape, lambda *_: (0,) * n)


def _rg_in_kernel(x_ref, wg_ref, wx_ref, gate_ref, u_ref):
    xb = x_ref[...].astype(BF16)
    gate_ref[...] = jax.nn.gelu(_dot(xb, wg_ref[...]))
    u_ref[...] = _dot(xb, wx_ref[...])


def _rg_in(x, wg, wx, tm):
    t, d = x.shape
    n = wg.shape[1]
    return pl.pallas_call(
        _rg_in_kernel,
        grid=(t // tm,),
        in_specs=[pl.BlockSpec((tm, d), lambda i: (i, 0)), _full((d, n)), _full((d, n))],
        out_specs=[pl.BlockSpec((tm, n), lambda i: (i, 0))] * 2,
        out_shape=[jax.ShapeDtypeStruct((t, n), F32)] * 2,
        compiler_params=_params("parallel"),
        name="rg_in",
    )(x, wg, wx)


def _rglru_gates(conv, wa_ref, ba, wi_ref, bi, lam):
    nblk = conv.shape[1] // RNN_BLOCK
    cb = conv.astype(BF16)
    ra = jnp.concatenate(
        [_dot(cb[:, n * RNN_BLOCK:(n + 1) * RNN_BLOCK], wa_ref[n]) for n in range(nblk)], axis=1)
    ia = jnp.concatenate(
        [_dot(cb[:, n * RNN_BLOCK:(n + 1) * RNN_BLOCK], wi_ref[n]) for n in range(nblk)], axis=1)
    r = jax.nn.sigmoid(ra + ba)
    i = jax.nn.sigmoid(ia + bi)
    z = -lam
    softplus = jnp.maximum(z, 0.0) + jnp.log1p(jnp.exp(-jnp.abs(z)))
    log_a = -LRU_C * r * softplus
    a = jnp.exp(log_a)
    v = 1.0 - a * a
    b = jnp.where(v > 0.0, v * lax.rsqrt(v), 0.0) * (i * conv)
    return a, b


def _rglru_seq_kernel(gate_ref, u_ref, conv0_ref, h0_ref, cw_ref, cb_ref, wa_ref, ba_ref, wi_ref, bi_ref,
                      lam_ref, hg_ref, hlast_ref, ubuf, hcar, abuf, bbuf):
    c = pl.program_id(2)
    tc, db = u_ref.shape
    halo = SUBLANES

    @pl.when(c == 0)
    def _():
        ubuf[0:halo, :] = conv0_ref[0]
        hcar[...] = jnp.broadcast_to(h0_ref[0], hcar.shape)

    ubuf[halo:halo + tc, :] = u_ref[...]
    cw = cw_ref[...]
    conv = cb_ref[...]
    for k in range(CONV_WIDTH):
        off = halo - (CONV_WIDTH - 1) + k
        conv = conv + ubuf[off:off + tc, :] * cw[k:k + 1, :]
    ubuf[0:halo, :] = ubuf[tc:tc + halo, :]

    a, b = _rglru_gates(conv, wa_ref, ba_ref[...], wi_ref, bi_ref[...], lam_ref[...])

    row = lax.broadcasted_iota(jnp.int32, (tc, db), 0) & (SUBLANES - 1)
    shift = 1
    while shift < SUBLANES:
        a_prev = pltpu.roll(a, shift, 0)
        b_prev = pltpu.roll(b, shift, 0)
        keep = row >= shift
        b = jnp.where(keep, a * b_prev + b, b)
        a = jnp.where(keep, a * a_prev, a)
        shift *= 2
    abuf[...] = a
    bbuf[...] = b

    def group(g, h):
        off = pl.multiple_of(g * SUBLANES, SUBLANES)
        hb = abuf[pl.ds(off, SUBLANES), :] * h + bbuf[pl.ds(off, SUBLANES), :]
        bbuf[pl.ds(off, SUBLANES), :] = hb
        return jnp.broadcast_to(hb[SUBLANES - 1:SUBLANES, :], hb.shape)

    h = lax.fori_loop(0, tc // SUBLANES, group, hcar[...])
    hcar[...] = h
    hg_ref[...] = (bbuf[...] * gate_ref[...]).astype(BF16)

    @pl.when(c == pl.num_programs(2) - 1)
    def _():
        hlast_ref[0] = h[0:1, :]


def _rglru_seq(gate, u, conv0, h0, rw, nb, s, tc, db):
    t, d = u.shape
    nc = s // tc
    kb = db // RNN_BLOCK
    row = lambda b, j, c: (b * nc + c, j)
    vec = pl.BlockSpec((1, db), lambda b, j, c: (0, j))
    blk = pl.BlockSpec((kb, RNN_BLOCK, RNN_BLOCK), lambda b, j, c: (j, 0, 0))
    return pl.pallas_call(
        _rglru_seq_kernel,
        grid=(nb, d // db, nc),
        in_specs=[
            pl.BlockSpec((tc, db), row), pl.BlockSpec((tc, db), row),
            pl.BlockSpec((1, SUBLANES, db), lambda b, j, c: (b, 0, j)),
            pl.BlockSpec((1, 1, db), lambda b, j, c: (b, 0, j)),
            pl.BlockSpec((CONV_WIDTH, db), lambda b, j, c: (0, j)), vec, blk, vec, blk, vec, vec,
        ],
        out_specs=[pl.BlockSpec((tc, db), row), pl.BlockSpec((1, 1, db), lambda b, j, c: (b, 0, j))],
        out_shape=[jax.ShapeDtypeStruct((t, d), BF16), jax.ShapeDtypeStruct((nb, 1, d), F32)],
        scratch_shapes=[pltpu.VMEM((tc + SUBLANES, db), F32), pltpu.VMEM((SUBLANES, db), F32),
                        pltpu.VMEM((tc, db), F32), pltpu.VMEM((tc, db), F32)],
        compiler_params=_params("parallel", "parallel", "arbitrary"),
        name="rglru_seq",
    )(gate, u, conv0, h0, rw["conv_w"], rw["conv_b"], rw["w_a"], rw["b_a"], rw["w_i"], rw["b_i"], rw["lam"])


def _rglru_step_kernel(gate_ref, u_ref, conv0_ref, h0_ref, cw_ref, cb_ref, wa_ref, ba_ref, wi_ref, bi_ref,
                       lam_ref, hg_ref, hlast_ref):
    s, nb, db = u_ref.shape
    ue = jnp.concatenate([conv0_ref[...], u_ref[...]], axis=0)
    cw = cw_ref[...]
    conv = cb_ref[...][None]
    for k in range(CONV_WIDTH):
        conv = conv + ue[k:k + s] * cw[k:k + 1, :][None]
    a, b = _rglru_gates(conv.reshape(s * nb, db), wa_ref, ba_ref[...], wi_ref, bi_ref[...], lam_ref[...])
    a = a.reshape(s, nb, db)
    b = b.reshape(s, nb, db)
    h = h0_ref[...]
    for t in range(s):
        h = a[t] * h + b[t]
        hg_ref[t] = (h * gate_ref[t]).astype(BF16)
    hlast_ref[...] = h


def _rglru_step(gate, u, conv0, h0, rw, db):
    s, nb, d = u.shape
    kb = db // RNN_BLOCK
    cube = pl.BlockSpec((s, nb, db), lambda j: (0, 0, j))
    vec = pl.BlockSpec((1, db), lambda j: (0, j))
    blk = pl.BlockSpec((kb, RNN_BLOCK, RNN_BLOCK), lambda j: (j, 0, 0))
    return pl.pallas_call(
        _rglru_step_kernel,
        grid=(d // db,),
        in_specs=[cube, cube, pl.BlockSpec((CONV_WIDTH - 1, nb, db), lambda j: (0, 0, j)),
                  pl.BlockSpec((nb, db), lambda j: (0, j)),
                  pl.BlockSpec((CONV_WIDTH, db), lambda j: (0, j)), vec, blk, vec, blk, vec, vec],
        out_specs=[cube, pl.BlockSpec((nb, db), lambda j: (0, j))],
        out_shape=[jax.ShapeDtypeStruct((s, nb, d), BF16), jax.ShapeDtypeStruct((nb, d), F32)],
        compiler_params=_params("parallel"),
        name="rglru_step",
    )(gate, u, conv0, h0, rw["conv_w"], rw["conv_b"], rw["w_a"], rw["b_a"], rw["w_i"], rw["b_i"], rw["lam"])


def _mm_ln_kernel(x_ref, w_ref, res_ref, g_ref, b_ref, o_ref):
    y = _dot(x_ref[...], w_ref[...])
    o_ref[...] = _layer_norm(DEEPNORM_ALPHA * res_ref[...] + y, g_ref[...], b_ref[...])


def _mm_ln(x, w, res, g, b, tm):
    t, k = x.shape
    n = w.shape[1]
    return pl.pallas_call(
        _mm_ln_kernel,
        grid=(t // tm,),
        in_specs=[pl.BlockSpec((tm, k), lambda i: (i, 0)), _full((k, n)),
                  pl.BlockSpec((tm, n), lambda i: (i, 0)), _full((1, n)), _full((1, n))],
        out_specs=pl.BlockSpec((tm, n), lambda i: (i, 0)),
        out_shape=jax.ShapeDtypeStruct((t, n), F32),
        compiler_params=_params("parallel"),
        name="mm_ln",
    )(x, w, res, g, b)


def _ffn_kernel(te_ref, x_ref, wg_ref, wu_ref, wd_ref, *rest, post_norm):
    del te_ref
    if post_norm:
        g_ref, b_ref, o_ref, xb_ref, acc_ref = rest
    else:
        o_ref, xb_ref, acc_ref = rest
    c = pl.program_id(1)

    @pl.when(c == 0)
    def _():
        xb_ref[...] = x_ref[...].astype(BF16)
        acc_ref[...] = jnp.zeros_like(acc_ref)

    xb = xb_ref[...]
    h = (jax.nn.silu(_dot(xb, wg_ref[0])) * _dot(xb, wu_ref[0])).astype(BF16)
    acc_ref[...] += _dot(h, wd_ref[0])

    @pl.when(c == pl.num_programs(1) - 1)
    def _():
        if post_norm:
            o_ref[...] = _layer_norm(DEEPNORM_ALPHA * x_ref[...] + acc_ref[...], g_ref[...], b_ref[...])
        else:
            o_ref[...] = acc_ref[...]


def _ffn(tile_expert, x, wg, wu, wd, norm, tm, tf):
    r, d = x.shape
    f = wg.shape[2]
    in_specs = [
        pl.BlockSpec((tm, d), lambda i, c, te: (i, 0)),
        pl.BlockSpec((1, d, tf), lambda i, c, te: (te[i], 0, c)),
        pl.BlockSpec((1, d, tf), lambda i, c, te: (te[i], 0, c)),
        pl.BlockSpec((1, tf, d), lambda i, c, te: (te[i], c, 0)),
    ]
    args = [x, wg, wu, wd]
    if norm is not None:
        in_specs += [pl.BlockSpec((1, d), lambda i, c, te: (0, 0))] * 2
        args += list(norm)
    return pl.pallas_call(
        functools.partial(_ffn_kernel, post_norm=norm is not None),
        grid_spec=pltpu.PrefetchScalarGridSpec(
            num_scalar_prefetch=1,
            grid=(r // tm, f // tf),
            in_specs=in_specs,
            out_specs=pl.BlockSpec((tm, d), lambda i, c, te: (i, 0)),
            scratch_shapes=[pltpu.VMEM((tm, d), BF16), pltpu.VMEM((tm, d), F32)],
        ),
        out_shape=jax.ShapeDtypeStruct((r, d), F32),
        compiler_params=_params("parallel", "arbitrary"),
        name="ffn",
    )(tile_expert, *args)


def _ple_kernel(x_ref, p_ref, wg_ref, wp_ref, o_ref):
    x = x_ref[...]
    a = _dot(x.astype(BF16), wg_ref[...])
    c = _dot(p_ref[...].astype(BF16), wp_ref[...])
    o_ref[...] = x + jax.nn.sigmoid(a) * c


def _ple(x, p, wg, wp, tm):
    t, d = x.shape
    dp = p.shape[1]
    return pl.pallas_call(
        _ple_kernel,
        grid=(t // tm,),
        in_specs=[pl.BlockSpec((tm, d), lambda i: (i, 0)), pl.BlockSpec((tm, dp), lambda i: (i, 0)),
                  _full((d, d)), _full((dp, d))],
        out_specs=pl.BlockSpec((tm, d), lambda i: (i, 0)),
        out_shape=jax.ShapeDtypeStruct((t, d), F32),
        compiler_params=_params("parallel"),
        name="ple",
    )(x, p, wg, wp)


def _kva_kernel(x_ref, wc_ref, wr_ref, wrs_ref, g_ref, cos_ref, sin_ref, ckv_ref, kpe_ref, kv_ref):
    xb = x_ref[...].astype(BF16)
    ckv = _rms_norm(_dot(xb, wc_ref[...]), g_ref[...])
    kpe = _dot(xb, wr_ref[...]) * cos_ref[...] + _dot(xb, wrs_ref[...]) * sin_ref[...]
    ckv_ref[...] = ckv
    kpe_ref[...] = kpe
    kv_ref[:, :KV_LORA_RANK] = ckv.astype(BF16)
    kv_ref[:, KV_LORA_RANK:] = kpe.astype(BF16)


def _kva(x, wc, wr, wrs, g, cos, sin, tm):
    t, d = x.shape
    nper = cos.shape[0] // tm
    tab = pl.BlockSpec((tm, QK_ROPE_DIM), lambda i: (i % nper, 0))
    out = lambda n: pl.BlockSpec((tm, n), lambda i: (i, 0))
    return pl.pallas_call(
        _kva_kernel,
        grid=(t // tm,),
        in_specs=[pl.BlockSpec((tm, d), lambda i: (i, 0)), _full(wc.shape), _full(wr.shape), _full(wrs.shape),
                  _full(g.shape), tab, tab],
        out_specs=[out(KV_LORA_RANK), out(QK_ROPE_DIM), out(QK_WIDTH)],
        out_shape=[jax.ShapeDtypeStruct((t, KV_LORA_RANK), F32), jax.ShapeDtypeStruct((t, QK_ROPE_DIM), F32),
                   jax.ShapeDtypeStruct((t, QK_WIDTH), BF16)],
        compiler_params=_params("parallel"),
        name="kv_latent",
    )(x, wc, wr, wrs, g, cos, sin)


def _q_kernel(x_ref, wqa_ref, qg_ref, wqb_ref, wuk_ref, cos_ref, sin_ref, q_ref):
    tm = x_ref.shape[0]
    nope = N_HEADS * QK_NOPE_DIM
    pe = N_HEADS * QK_ROPE_DIM
    c = KV_LORA_RANK
    r = QK_ROPE_DIM
    cq = _rms_norm(_dot(x_ref[...].astype(BF16), wqa_ref[...]), qg_ref[...])
    q = _dot(cq.astype(BF16), wqb_ref[...])
    q_pe = (q[:, nope:nope + pe] * cos_ref[...] + q[:, nope + pe:nope + 2 * pe] * sin_ref[...]) * QK_SCALE
    qn = q[:, :nope].astype(BF16)
    pair = 2 * QK_NOPE_DIM
    for j in range(N_HEADS // 2):
        ql = _dot(qn[:, j * pair:(j + 1) * pair], wuk_ref[j]) * QK_SCALE
        for k in range(2):
            h = 2 * j + k
            q_ref[h * tm:(h + 1) * tm, :c] = ql[:, k * c:(k + 1) * c].astype(q_ref.dtype)
            q_ref[h * tm:(h + 1) * tm, c:] = q_pe[:, h * r:(h + 1) * r].astype(q_ref.dtype)


def _q_proj(x, wqa, qg, wqb, wuk, cos, sin, tm, dtype):
    t, d = x.shape
    nper = cos.shape[0] // tm
    pe = N_HEADS * QK_ROPE_DIM
    tab = pl.BlockSpec((tm, pe), lambda i: (i % nper, 0))
    return pl.pallas_call(
        _q_kernel,
        grid=(t // tm,),
        in_specs=[pl.BlockSpec((tm, d), lambda i: (i, 0)), _full(wqa.shape), _full(qg.shape), _full(wqb.shape),
                  _full(wuk.shape), tab, tab],
        out_specs=pl.BlockSpec((N_HEADS * tm, QK_WIDTH), lambda i: (i, 0)),
        out_shape=jax.ShapeDtypeStruct((N_HEADS * t, QK_WIDTH), dtype),
        compiler_params=_params("parallel"),
        name="q_proj",
    )(x, wqa, qg, wqb, wuk, cos, sin)


def _widen(stat, width):
    return jnp.concatenate([stat] * (width // LANES), axis=1)


def _softmax_step(s, kv, m_s, l_s, acc):
    m_prev = m_s[...]
    m_new = jnp.maximum(m_prev, jnp.max(s, -1, keepdims=True))
    corr = jnp.exp2(m_prev - m_new)
    p = jnp.exp2(s - _widen(m_new, s.shape[1]))
    l_s[...] = l_s[...] * corr + jnp.sum(p, -1, keepdims=True)
    acc[...] = acc[...] * _widen(corr, KV_LORA_RANK) + _dot(p.astype(BF16), kv)
    m_s[...] = m_new


def _attn_kernel(q_ref, kv_ref, o_ref, m_s, l_s, acc, *, tq, tk):
    i = pl.program_id(1)
    j = pl.program_id(2)

    @pl.when(j == 0)
    def _():
        m_s[...] = jnp.full_like(m_s, -jnp.inf)
        l_s[...] = jnp.zeros_like(l_s)
        acc[...] = jnp.zeros_like(acc)

    def step(masked):
        kv = kv_ref[...]
        s = _dot_nt(q_ref[...], kv)
        if masked:
            q_pos = i * tq + (lax.broadcasted_iota(jnp.int32, s.shape, 0) & (tq - 1))
            k_pos = j * tk + lax.broadcasted_iota(jnp.int32, s.shape, 1)
            s = jnp.where(k_pos <= q_pos, s, -jnp.inf)
        _softmax_step(s, kv[:, :KV_LORA_RANK], m_s, l_s, acc)

    first_q = i * tq
    last_k = j * tk + tk - 1

    @pl.when(last_k <= first_q)
    def _():
        step(False)

    @pl.when(jnp.logical_and(last_k > first_q, j * tk <= first_q + tq - 1))
    def _():
        step(True)

    @pl.when(j == pl.num_programs(2) - 1)
    def _():
        inv = 1.0 / l_s[...]
        o_ref[...] = (acc[...] * _widen(inv, KV_LORA_RANK)).astype(o_ref.dtype)


def _attn_prompt(q, kv, nb, s, tq, tk):
    nq, nk = s // tq, s // tk
    assert tq & (tq - 1) == 0
    rows = N_HEADS * tq
    kv_row = lambda b, i, j: (b * nk + jnp.minimum(j, (i * tq + tq - 1) // tk), 0)
    q_row = lambda b, i, j: (b * nq + i, 0)
    return pl.pallas_call(
        functools.partial(_attn_kernel, tq=tq, tk=tk),
        grid=(nb, nq, nk),
        in_specs=[pl.BlockSpec((rows, QK_WIDTH), q_row), pl.BlockSpec((tk, QK_WIDTH), kv_row)],
        out_specs=pl.BlockSpec((rows, KV_LORA_RANK), q_row),
        out_shape=jax.ShapeDtypeStruct((q.shape[0], KV_LORA_RANK), BF16),
        scratch_shapes=[pltpu.VMEM((rows, LANES), F32), pltpu.VMEM((rows, LANES), F32),
                        pltpu.VMEM((rows, KV_LORA_RANK), F32)],
        compiler_params=_params("parallel", "parallel", "arbitrary"),
        name="attn_prompt",
    )(q, kv)


def _attn_paged_kernel(pt_ref, q_ref, cnew_ref, knew_ref, *rest, pages, s_new):
    del pt_ref
    ck_refs = rest[:pages]
    kp_refs = rest[pages:2 * pages]
    o_ref, m_s, l_s, acc, kbuf, kpbuf = rest[2 * pages:]
    c = pl.program_id(1)
    rows = N_HEADS * s_new
    q = q_ref[...].reshape(rows, QK_WIDTH).astype(BF16)
    ql = q[:, :KV_LORA_RANK]
    qp = q[:, KV_LORA_RANK:]

    @pl.when(c == 0)
    def _():
        pad = 2 * SUBLANES - s_new
        cn = jnp.concatenate([cnew_ref[0], jnp.zeros((pad, KV_LORA_RANK), F32)], axis=0).astype(BF16)
        kn = jnp.concatenate([knew_ref[0], jnp.zeros((pad, QK_ROPE_DIM), F32)], axis=0).astype(BF16)
        s = _dot_nt(ql, cn) + _dot_nt(qp, kn)
        tok = lax.broadcasted_iota(jnp.int32, s.shape, 0) & (s_new - 1)
        key = lax.broadcasted_iota(jnp.int32, s.shape, 1)
        s = jnp.where(key <= tok, s, -jnp.inf)
        m = jnp.max(s, -1, keepdims=True)
        p = jnp.exp2(s - m)
        m_s[...] = jnp.broadcast_to(m, m_s.shape)
        l_s[...] = jnp.broadcast_to(jnp.sum(p, -1, keepdims=True), l_s.shape)
        acc[...] = _dot(p.astype(BF16), cn)

    page = ck_refs[0].shape[1]
    for k in range(pages):
        kbuf[k * page:(k + 1) * page, :] = ck_refs[k][0].astype(BF16)
        kpbuf[:, k * page:(k + 1) * page] = kp_refs[k][0].astype(BF16)
    kv = kbuf[...]
    s = _dot_nt(ql, kv) + _dot(qp, kpbuf[...])
    _softmax_step(s, kv, m_s, l_s, acc)

    @pl.when(c == pl.num_programs(1) - 1)
    def _():
        inv = 1.0 / l_s[...]
        o = acc[...] * _widen(inv, KV_LORA_RANK)
        o_ref[...] = o.reshape(o_ref.shape)


def _attn_paged(q, cnew, knew, cache_ckv, cache_kpe_t, page_table, pages):
    n_tiles, _, per_tile, s_new, _ = q.shape
    nb = n_tiles * per_tile
    assert s_new == SUBLANES
    n_pages = page_table.shape[1]
    page = cache_ckv.shape[1]
    c = KV_LORA_RANK
    r = QK_ROPE_DIM
    per_b = lambda b, j, pt: (b, 0, 0)
    q_blk = lambda w: pl.BlockSpec((None, N_HEADS, None, s_new, w),
                                   lambda b, j, pt: (b // per_tile, 0, b % per_tile, 0, 0))

    def page_spec(shape, k):
        return pl.BlockSpec((1,) + shape, lambda b, j, pt: (pt[b * n_pages + j * pages + k], 0, 0))

    in_specs = [q_blk(QK_WIDTH), pl.BlockSpec((1, s_new, c), per_b), pl.BlockSpec((1, s_new, r), per_b)]
    in_specs += [page_spec((page, c), k) for k in range(pages)]
    in_specs += [page_spec((r, page), k) for k in range(pages)]
    rows = N_HEADS * s_new
    return pl.pallas_call(
        functools.partial(_attn_paged_kernel, pages=pages, s_new=s_new),
        grid_spec=pltpu.PrefetchScalarGridSpec(
            num_scalar_prefetch=1,
            grid=(nb, n_pages // pages),
            in_specs=in_specs,
            out_specs=q_blk(c),
            scratch_shapes=[pltpu.VMEM((rows, LANES), F32), pltpu.VMEM((rows, LANES), F32),
                            pltpu.VMEM((rows, c), F32), pltpu.VMEM((pages * page, c), BF16),
                            pltpu.VMEM((r, pages * page), BF16)],
        ),
        out_shape=jax.ShapeDtypeStruct(q.shape[:-1] + (c,), F32),
        compiler_params=_params("parallel", "arbitrary"),
        name="attn_paged",
    )(page_table.reshape(-1), q, cnew, knew, *([cache_ckv] * pages), *([cache_kpe_t] * pages))


def _mla_out_kernel(o_ref, wuv_ref, wo_ref, res_ref, g_ref, b_ref, out_ref):
    tm = res_ref.shape[0]
    head = lambda h: o_ref[h * tm:(h + 1) * tm, :].astype(BF16)
    v = jnp.concatenate(
        [_dot(jnp.concatenate([head(2 * j), head(2 * j + 1)], axis=1), wuv_ref[j]) for j in range(N_HEADS // 2)],
        axis=1).astype(BF16)
    y = _dot(v, wo_ref[...])
    out_ref[...] = _layer_norm(DEEPNORM_ALPHA * res_ref[...] + y, g_ref[...], b_ref[...])


def _mla_out(o, wuv, wo, res, g, b, tm):
    t, n = res.shape
    return pl.pallas_call(
        _mla_out_kernel,
        grid=(t // tm,),
        in_specs=[pl.BlockSpec((N_HEADS * tm, o.shape[1]), lambda i: (i, 0)), _full(wuv.shape), _full(wo.shape),
                  pl.BlockSpec((tm, n), lambda i: (i, 0)), _full((1, n)), _full((1, n))],
        out_specs=pl.BlockSpec((tm, n), lambda i: (i, 0)),
        out_shape=jax.ShapeDtypeStruct((t, n), F32),
        compiler_params=_params("parallel"),
        name="mla_out",
    )(o, wuv, wo, res, g, b)


def _router_kernel(x_ref, w_ref, idx_ref, wt_ref):
    x = x_ref[...]
    w = w_ref[...]
    xh = x.astype(BF16)
    xl = (x - xh.astype(F32)).astype(BF16)
    wh = w.astype(BF16)
    wl = (w - wh.astype(F32)).astype(BF16)
    lg = (_dot_nt(wh, xh) + _dot_nt(wh, xl) + _dot_nt(wl, xh))[:N_EXPERTS]
    e = lax.broadcasted_iota(jnp.int32, lg.shape, 0).astype(F32)
    none = float(N_EXPERTS)
    v1 = jnp.max(lg, 0, keepdims=True)
    i1 = jnp.min(jnp.where(lg == v1, e, none), 0, keepdims=True)
    lg2 = jnp.where(e == i1, -jnp.inf, lg)
    v2 = jnp.max(lg2, 0, keepdims=True)
    i2 = jnp.min(jnp.where(lg2 == v2, e, none), 0, keepdims=True)
    ex = jnp.exp(v2 - v1)
    den = 1.0 + ex
    idx_ref[...] = jnp.concatenate([i1, i2], axis=0).astype(jnp.int32)
    wt_ref[...] = jnp.concatenate([1.0 / den, ex / den], axis=0)


def _router(x, w_t, tm):
    t, d = x.shape
    return pl.pallas_call(
        _router_kernel,
        grid=(t // tm,),
        in_specs=[pl.BlockSpec((tm, d), lambda i: (i, 0)), _full(w_t.shape)],
        out_specs=[pl.BlockSpec((2, tm), lambda i: (0, i))] * 2,
        out_shape=[jax.ShapeDtypeStruct((2, t), jnp.int32), jax.ShapeDtypeStruct((2, t), F32)],
        compiler_params=_params("parallel"),
        name="router",
    )(x, w_t)


def _dispatch_kernel(tok_ref, x_hbm, o_ref, sem):
    rows = o_ref.shape[0]
    base = pl.program_id(0) * rows

    def issue(r, carry):
        t = tok_ref[base + r]
        pltpu.make_async_copy(x_hbm.at[pl.ds(t, 1), :], o_ref.at[pl.ds(r, 1), :], sem).start()
        return carry

    lax.fori_loop(0, rows, issue, 0, unroll=8)
    pltpu.make_async_copy(x_hbm.at[pl.ds(0, rows), :], o_ref, sem).wait()


def _dispatch(slot_token, x, rows):
    ns = slot_token.shape[0]
    d = x.shape[1]
    return pl.pallas_call(
        _dispatch_kernel,
        grid_spec=pltpu.PrefetchScalarGridSpec(
            num_scalar_prefetch=1,
            grid=(ns // rows,),
            in_specs=[pl.BlockSpec(memory_space=pl.ANY)],
            out_specs=pl.BlockSpec((rows, d), lambda i, tok: (i, 0)),
            scratch_shapes=[pltpu.SemaphoreType.DMA],
        ),
        out_shape=jax.ShapeDtypeStruct((ns, d), x.dtype),
        compiler_params=_params("arbitrary"),
        name="moe_dispatch",
    )(slot_token, x)


def _combine_kernel(pos_ref, y_hbm, x_ref, w1_ref, w2_ref, g_ref, b_ref, o_ref, y1, y2, sem):
    rows = o_ref.shape[0]
    t = pl.num_programs(0) * rows
    base = pl.program_id(0) * rows

    def issue(r, carry):
        p1 = pos_ref[base + r]
        p2 = pos_ref[t + base + r]
        pltpu.make_async_copy(y_hbm.at[pl.ds(p1, 1), :], y1.at[pl.ds(r, 1), :], sem.at[0]).start()
        pltpu.make_async_copy(y_hbm.at[pl.ds(p2, 1), :], y2.at[pl.ds(r, 1), :], sem.at[1]).start()
        return carry

    lax.fori_loop(0, rows, issue, 0, unroll=8)
    pltpu.make_async_copy(y_hbm.at[pl.ds(0, rows), :], y1, sem.at[0]).wait()
    pltpu.make_async_copy(y_hbm.at[pl.ds(0, rows), :], y2, sem.at[1]).wait()
    ff = w1_ref[...] * y1[...] + w2_ref[...] * y2[...]
    o_ref[...] = _layer_norm(DEEPNORM_ALPHA * x_ref[...] + ff, g_ref[...], b_ref[...])


def _combine(pos, y, x, w1, w2, g, b, rows):
    t, d = x.shape
    row = pl.BlockSpec((rows, d), lambda i, p: (i, 0))
    col = pl.BlockSpec((rows, 1), lambda i, p: (i, 0))
    vec = pl.BlockSpec((1, d), lambda i, p: (0, 0))
    return pl.pallas_call(
        _combine_kernel,
        grid_spec=pltpu.PrefetchScalarGridSpec(
            num_scalar_prefetch=1,
            grid=(t // rows,),
            in_specs=[pl.BlockSpec(memory_space=pl.ANY), row, col, col, vec, vec],
            out_specs=row,
            scratch_shapes=[pltpu.VMEM((rows, d), F32), pltpu.VMEM((rows, d), F32),
                            pltpu.SemaphoreType.DMA((2,))],
        ),
        out_shape=jax.ShapeDtypeStruct((t, d), F32),
        compiler_params=_params("arbitrary"),
        name="moe_combine",
    )(pos, y, x, w1, w2, g, b)


def _moe(x, w_router_t, wg, wu, wd, g, b, tm, tile, tf):
    t, d = x.shape
    idx, wts = _router(x, w_router_t, tm)
    e_flat = idx.reshape(-1)
    onehot = (e_flat[:, None] == jnp.arange(N_EXPERTS, dtype=jnp.int32)[None, :]).astype(jnp.int32)
    csum = jnp.cumsum(onehot, axis=0)
    rank = jnp.sum(csum * onehot, axis=1) - 1
    counts = csum[-1]
    padded = ((counts + tile - 1) // tile) * tile
    ends = jnp.cumsum(padded)
    starts = ends - padded
    pos = (starts[e_flat] + rank).astype(jnp.int32)
    n_slots = 2 * t + N_EXPERTS * tile
    token = jnp.arange(2 * t, dtype=jnp.int32) % t
    slot_token = jnp.zeros((n_slots,), jnp.int32).at[pos].set(token)
    tile_start = jnp.arange(n_slots // tile, dtype=jnp.int32) * tile
    tile_expert = jnp.minimum(
        jnp.sum((tile_start[:, None] >= ends[None, :]).astype(jnp.int32), axis=1), N_EXPERTS - 1)

    xs = _dispatch(slot_token, x, tile)
    ys = _ffn(tile_expert, xs, wg, wu, wd, None, tile, tf)
    return _combine(pos, ys, x, wts[0][:, None], wts[1][:, None], g, b, tm)


def _swap_halves(w):
    half = w.shape[-1] // 2
    return jnp.concatenate([w[..., half:], w[..., :half]], axis=-1)


def _prep_weights(w):
    out = {}
    row = lambda v: v.reshape(1, -1).astype(F32)
    out["ln_mix"] = [(row(w["ln_mix_g"][i]), row(w["ln_mix_b"][i])) for i in range(DEPTH)]
    out["ln_ffn"] = [(row(w["ln_ffn_g"][i]), row(w["ln_ffn_b"][i])) for i in range(DEPTH)]
    out["rg_wg"] = w["rg_w_gate"][0].astype(BF16)
    out["rg_wx"] = w["rg_w_x"][0].astype(BF16)
    out["rg"] = {
        "conv_w": w["rg_conv_w"][0], "conv_b": row(w["rg_conv_b"][0]),
        "w_a": w["rg_w_a"][0].astype(BF16), "b_a": row(w["rg_b_a"][0]),
        "w_i": w["rg_w_i"][0].astype(BF16), "b_i": row(w["rg_b_i"][0]),
        "lam": row(w["rg_lambda"][0]),
    }
    out["rg_wout"] = w["rg_w_out"][0].astype(BF16)
    out["ffn"] = (w["ffn_w_gate"].astype(BF16), w["ffn_w_up"].astype(BF16), w["ffn_w_down"].astype(BF16))
    out["moe"] = (w["moe_w_gate"][0].astype(BF16), w["moe_w_up"][0].astype(BF16), w["moe_w_down"][0].astype(BF16))
    e = w["moe_w_router"].shape[-1]
    out["router_t"] = jnp.concatenate(
        [w["moe_w_router"][0].T, jnp.zeros((2 * SUBLANES - e, w["moe_w_router"].shape[1]), F32)], axis=0)
    out["ple_wg"] = w["ple_w_gate"].astype(BF16)
    out["ple_wp"] = w["ple_w_proj"].astype(BF16)
    kv = w["kv_w_a"]
    out["kv_wc"] = kv[:, :KV_LORA_RANK].astype(BF16)
    out["kv_wr"] = kv[:, KV_LORA_RANK:].astype(BF16)
    out["kv_wrs"] = _swap_halves(kv[:, KV_LORA_RANK:]).astype(BF16)
    out["kv_g"] = row(w["kv_norm_g"])
    out["q_wa"] = w["mla_w_q_a"][0].astype(BF16)
    out["q_g"] = row(w["mla_q_norm_g"][0])
    qb = w["mla_w_q_b"][0].reshape(-1, N_HEADS, QK_NOPE_DIM + QK_ROPE_DIM)
    lora = qb.shape[0]
    q_nope = qb[:, :, :QK_NOPE_DIM].reshape(lora, -1)
    q_rope = qb[:, :, QK_NOPE_DIM:]
    out["q_wb"] = jnp.concatenate(
        [q_nope, q_rope.reshape(lora, -1), _swap_halves(q_rope).reshape(lora, -1)], axis=1).astype(BF16)
    uk = jnp.transpose(w["kv_w_uk"], (1, 2, 0))
    zk = jnp.zeros_like(uk[0])
    out["wuk"] = jnp.stack([
        jnp.concatenate([jnp.concatenate([uk[2 * j], zk], axis=1), jnp.concatenate([zk, uk[2 * j + 1]], axis=1)],
                        axis=0) for j in range(N_HEADS // 2)]).astype(BF16)
    uv = jnp.transpose(w["kv_w_uv"], (1, 0, 2))
    zv = jnp.zeros_like(uv[0])
    out["wuv"] = jnp.stack([
        jnp.concatenate([jnp.concatenate([uv[2 * j], zv], axis=1), jnp.concatenate([zv, uv[2 * j + 1]], axis=1)],
                        axis=0) for j in range(N_HEADS // 2)]).astype(BF16)
    out["wo"] = w["mla_w_o"][0].astype(BF16)
    return out


def _rope_tables(pos, repeat):
    inv = ROPE_THETA ** (-jnp.arange(0, QK_ROPE_DIM, 2, dtype=F32) / QK_ROPE_DIM)
    ang = pos[:, None] * inv[None, :]
    cos, sin = jnp.cos(ang), jnp.sin(ang)
    cos_k = jnp.tile(jnp.concatenate([cos, cos], axis=-1), (repeat, 1))
    sin_k = jnp.tile(jnp.concatenate([-sin, sin], axis=-1), (repeat, 1))
    return cos_k, sin_k, jnp.tile(cos_k, (1, N_HEADS)), jnp.tile(sin_k, (1, N_HEADS))


def _tile_rows(t, want):
    tm = min(want, t)
    assert t % tm == 0
    return tm


def _trunk(x3, p4, conv0, rnn0, pos, pw, paged):
    nb, s, d = x3.shape
    t = nb * s
    x = x3.reshape(t, d)
    p = p4.reshape(DEPTH, t, -1)
    tm = _tile_rows(t, 512)

    gate, u = _rg_in(x, pw["rg_wg"], pw["rg_wx"], tm)
    if paged is None:
        conv_pad = jnp.concatenate(
            [jnp.zeros((nb, SUBLANES - (CONV_WIDTH - 1), D_RNN), F32), conv0], axis=1)
        hg, h_last = _rglru_seq(gate, u, conv_pad, rnn0[:, None, :], pw["rg"], nb, s, min(s, 256), 512)
        h_last = h_last[:, 0, :]
        tabs = _rope_tables(pos, 1)
    else:
        to_tm = lambda a: jnp.transpose(a.reshape(nb, s, -1), (1, 0, 2))
        hg, h_last = _rglru_step(to_tm(gate), to_tm(u), jnp.transpose(conv0, (1, 0, 2)), rnn0, pw["rg"], 512)
        hg = jnp.transpose(hg, (1, 0, 2)).reshape(t, -1)
        tabs = _rope_tables(pos, nb)
    conv_state = u.reshape(nb, s, -1)[:, s - (CONV_WIDTH - 1):, :]
    x = _mm_ln(hg, pw["rg_wout"], x, *pw["ln_mix"][0], tm)
    x = _ffn(jnp.zeros((t // tm,), jnp.int32), x, *pw["ffn"], pw["ln_ffn"][0], tm, pw["ffn"][0].shape[2] // 2)
    x = _ple(x, p[0], pw["ple_wg"][0], pw["ple_wp"][0], tm)
    cos_k, sin_k, cos_q, sin_q = tabs
    ckv, kpe, kv = _kva(x, pw["kv_wc"], pw["kv_wr"], pw["kv_wrs"], pw["kv_g"], cos_k, sin_k, tm)

    tq = _tile_rows(t, 256)
    if paged is None:
        q = _q_proj(x, pw["q_wa"], pw["q_g"], pw["q_wb"], pw["wuk"], cos_q, sin_q, tq, BF16)
        o = _attn_prompt(q, kv, nb, s, tq, 256)
    else:
        cache_ckv, cache_kpe, page_table = paged
        q = _q_proj(x, pw["q_wa"], pw["q_g"], pw["q_wb"], pw["wuk"], cos_q, sin_q, tq, F32)
        o = _attn_paged(q.reshape(t // tq, N_HEADS, tq // s, s, QK_WIDTH), ckv.reshape(nb, s, -1),
                        kpe.reshape(nb, s, -1), cache_ckv, jnp.swapaxes(cache_kpe, 1, 2), page_table, 16)
        o = o.reshape(t * N_HEADS, KV_LORA_RANK)
    x = _mla_out(o, pw["wuv"], pw["wo"], x, *pw["ln_mix"][1], tq)
    x = _moe(x, pw["router_t"], *pw["moe"], *pw["ln_ffn"][1], tq, 512 if paged is None else 256,
             pw["moe"][0].shape[2] // 2)
    x = _ple(x, p[1], pw["ple_wg"][1], pw["ple_wp"][1], tm)
    return (x.reshape(nb, s, d), conv_state[None], h_last[None], ckv.reshape(nb, s, -1), kpe.reshape(nb, s, -1))


def kernel(x_prompt, x_sample, p_prompt, p_sample, state_conv, state_rnn, cache_ckv, cache_kpe, page_table, ln_mix_g, ln_mix_b, ln_ffn_g, ln_ffn_b, rg_w_gate, rg_w_x, rg_conv_w, rg_conv_b, rg_w_a, rg_b_a, rg_w_i, rg_b_i, rg_lambda, rg_w_out, mla_w_q_a, mla_q_norm_g, mla_w_q_b, mla_w_o, kv_w_a, kv_norm_g, kv_w_uk, kv_w_uv, ffn_w_gate, ffn_w_up, ffn_w_down, moe_w_router, moe_w_gate, moe_w_up, moe_w_down, ple_w_gate, ple_w_proj):
    w = dict(
        ln_mix_g=ln_mix_g, ln_mix_b=ln_mix_b, ln_ffn_g=ln_ffn_g, ln_ffn_b=ln_ffn_b,
        rg_w_gate=rg_w_gate, rg_w_x=rg_w_x, rg_conv_w=rg_conv_w, rg_conv_b=rg_conv_b, rg_w_a=rg_w_a,
        rg_b_a=rg_b_a, rg_w_i=rg_w_i, rg_b_i=rg_b_i, rg_lambda=rg_lambda, rg_w_out=rg_w_out,
        mla_w_q_a=mla_w_q_a, mla_q_norm_g=mla_q_norm_g, mla_w_q_b=mla_w_q_b, mla_w_o=mla_w_o,
        kv_w_a=kv_w_a, kv_norm_g=kv_norm_g, kv_w_uk=kv_w_uk, kv_w_uv=kv_w_uv,
        ffn_w_gate=ffn_w_gate, ffn_w_up=ffn_w_up, ffn_w_down=ffn_w_down,
        moe_w_router=moe_w_router, moe_w_gate=moe_w_gate, moe_w_up=moe_w_up, moe_w_down=moe_w_down,
        ple_w_gate=ple_w_gate, ple_w_proj=ple_w_proj)
    pw = _prep_weights(w)
    nb, s = x_prompt.shape[:2]
    dec_s = x_sample.shape[1]
    past_len = page_table.shape[1] * cache_ckv.shape[1]
    n_a = state_conv.shape[0]
    assert n_a == 1
    conv0_p = jnp.zeros((nb, CONV_WIDTH - 1, D_RNN), state_conv.dtype)
    rnn0_p = jnp.zeros((nb, D_RNN), state_rnn.dtype)
    y_p, conv_p, rnn_p, ckv_p, kpe_p = _trunk(
        x_prompt, p_prompt, conv0_p, rnn0_p, jnp.arange(s, dtype=F32), pw, None)
    y_s, conv_s, rnn_s, ckv_s, kpe_s = _trunk(
        x_sample, p_sample, state_conv[0], state_rnn[0], past_len + jnp.arange(dec_s, dtype=F32), pw,
        (cache_ckv, cache_kpe, page_table))
    return (y_p, y_s, conv_p, rnn_p, ckv_p, kpe_p, conv_s, rnn_s, ckv_s, kpe_s)
```

```python
import functools

import jax
import jax.numpy as jnp
from jax import lax
from jax.experimental import pallas as pl
from jax.experimental.pallas import tpu as pltpu

F32 = jnp.float32
BF16 = jnp.bfloat16

D_RNN = 1536
RNN_BLOCK = 128
CONV_WIDTH = 4
LRU_C = 8.0
N_HEADS = 16
KV_LORA_RANK = 256
QK_NOPE_DIM = 64
QK_ROPE_DIM = 32
V_HEAD_DIM = 64
ROPE_THETA = 10000.0
SOFTMAX_SCALE = (QK_NOPE_DIM + QK_ROPE_DIM) ** -0.5
N_EXPERTS = 8
LN_EPS = 1e-5
RMS_EPS = 1e-6
DEPTH = 2
DEEPNORM_ALPHA = (2.0 * DEPTH) ** 0.25

QK_WIDTH = KV_LORA_RANK + QK_ROPE_DIM
QK_SCALE = SOFTMAX_SCALE * 1.4426950408889634

SUBLANES = 8
LANES = 128
VMEM_LIMIT = 48 * 1024 * 1024


def _params(*sem):
    return pltpu.CompilerParams(dimension_semantics=sem, vmem_limit_bytes=VMEM_LIMIT)


def _dot(a, b):
    return jnp.dot(a, b, preferred_element_type=F32)


def _dot_nt(a, b):
    return lax.dot_general(a, b, (((1,), (1,)), ((), ())), preferred_element_type=F32)


def _layer_norm(z, g, b):
    mu = jnp.mean(z, -1, keepdims=True)
    d = z - mu
    var = jnp.mean(d * d, -1, keepdims=True)
    return d * lax.rsqrt(var + LN_EPS) * g + b


def _rms_norm(z, g):
    return z * lax.rsqrt(jnp.mean(z * z, -1, keepdims=True) + RMS_EPS) * g


def _full(shape):
    n = len(shape)
    return pl.BlockSpec(shape, lambda *_: (0,) * n)


def _rg_in_kernel(x_ref, wg_ref, wx_ref, gate_ref, u_ref):
    xb = x_ref[...].astype(BF16)
    gate_ref[...] = jax.nn.gelu(_dot(xb, wg_ref[...]))
    u_ref[...] = _dot(xb, wx_ref[...])


def _rg_in(x, wg, wx, tm):
    t, d = x.shape
    n = wg.shape[1]
    return pl.pallas_call(
        _rg_in_kernel,
        grid=(t // tm,),
        in_specs=[pl.BlockSpec((tm, d), lambda i: (i, 0)), _full((d, n)), _full((d, n))],
        out_specs=[pl.BlockSpec((tm, n), lambda i: (i, 0))] * 2,
        out_shape=[jax.ShapeDtypeStruct((t, n), F32)] * 2,
        compiler_params=_params("parallel"),
        name="rg_in",
    )(x, wg, wx)


def _rglru_gates(conv, wa_ref, ba, wi_ref, bi, lam):
    nblk = conv.shape[1] // RNN_BLOCK
    cb = conv.astype(BF16)
    ra = jnp.concatenate(
        [_dot(cb[:, n * RNN_BLOCK:(n + 1) * RNN_BLOCK], wa_ref[n]) for n in range(nblk)], axis=1)
    ia = jnp.concatenate(
        [_dot(cb[:, n * RNN_BLOCK:(n + 1) * RNN_BLOCK], wi_ref[n]) for n in range(nblk)], axis=1)
    r = jax.nn.sigmoid(ra + ba)
    i = jax.nn.sigmoid(ia + bi)
    z = -lam
    softplus = jnp.maximum(z, 0.0) + jnp.log1p(jnp.exp(-jnp.abs(z)))
    log_a = -LRU_C * r * softplus
    a = jnp.exp(log_a)
    v = 1.0 - a * a
    b = jnp.where(v > 0.0, v * lax.rsqrt(v), 0.0) * (i * conv)
    return a, b


def _rglru_seq_kernel(gate_ref, u_ref, conv0_ref, h0_ref, cw_ref, cb_ref, wa_ref, ba_ref, wi_ref, bi_ref,
                      lam_ref, hg_ref, hlast_ref, ubuf, hcar, abuf, bbuf):
    c = pl.program_id(2)
    tc, db = u_ref.shape
    halo = SUBLANES

    @pl.when(c == 0)
    def _():
        ubuf[0:halo, :] = conv0_ref[0]
        hcar[...] = jnp.broadcast_to(h0_ref[0], hcar.shape)

    ubuf[halo:halo + tc, :] = u_ref[...]
    cw = cw_ref[...]
    conv = cb_ref[...]
    for k in range(CONV_WIDTH):
        off = halo - (CONV_WIDTH - 1) + k
        conv = conv + ubuf[off:off + tc, :] * cw[k:k + 1, :]
    ubuf[0:halo, :] = ubuf[tc:tc + halo, :]

    a, b = _rglru_gates(conv, wa_ref, ba_ref[...], wi_ref, bi_ref[...], lam_ref[...])

    row = lax.broadcasted_iota(jnp.int32, (tc, db), 0) & (SUBLANES - 1)
    shift = 1
    while shift < SUBLANES:
        a_prev = pltpu.roll(a, shift, 0)
        b_prev = pltpu.roll(b, shift, 0)
        keep = row >= shift
        b = jnp.where(keep, a * b_prev + b, b)
        a = jnp.where(keep, a * a_prev, a)
        shift *= 2
    abuf[...] = a
    bbuf[...] = b

    def group(g, h):
        off = pl.multiple_of(g * SUBLANES, SUBLANES)
        hb = abuf[pl.ds(off, SUBLANES), :] * h + bbuf[pl.ds(off, SUBLANES), :]
        bbuf[pl.ds(off, SUBLANES), :] = hb
        return jnp.broadcast_to(hb[SUBLANES - 1:SUBLANES, :], hb.shape)

    h = lax.fori_loop(0, tc // SUBLANES, group, hcar[...])
    hcar[...] = h
    hg_ref[...] = (bbuf[...] * gate_ref[...]).astype(BF16)

    @pl.when(c == pl.num_programs(2) - 1)
    def _():
        hlast_ref[0] = h[0:1, :]


def _rglru_seq(gate, u, conv0, h0, rw, nb, s, tc, db):
    t, d = u.shape
    nc = s // tc
    kb = db // RNN_BLOCK
    row = lambda b, j, c: (b * nc + c, j)
    vec = pl.BlockSpec((1, db), lambda b, j, c: (0, j))
    blk = pl.BlockSpec((kb, RNN_BLOCK, RNN_BLOCK), lambda b, j, c: (j, 0, 0))
    return pl.pallas_call(
        _rglru_seq_kernel,
        grid=(nb, d // db, nc),
        in_specs=[
            pl.BlockSpec((tc, db), row), pl.BlockSpec((tc, db), row),
            pl.BlockSpec((1, SUBLANES, db), lambda b, j, c: (b, 0, j)),
            pl.BlockSpec((1, 1, db), lambda b, j, c: (b, 0, j)),
            pl.BlockSpec((CONV_WIDTH, db), lambda b, j, c: (0, j)), vec, blk, vec, blk, vec, vec,
        ],
        out_specs=[pl.BlockSpec((tc, db), row), pl.BlockSpec((1, 1, db), lambda b, j, c: (b, 0, j))],
        out_shape=[jax.ShapeDtypeStruct((t, d), BF16), jax.ShapeDtypeStruct((nb, 1, d), F32)],
        scratch_shapes=[pltpu.VMEM((tc + SUBLANES, db), F32), pltpu.VMEM((SUBLANES, db), F32),
                        pltpu.VMEM((tc, db), F32), pltpu.VMEM((tc, db), F32)],
        compiler_params=_params("parallel", "parallel", "arbitrary"),
        name="rglru_seq",
    )(gate, u, conv0, h0, rw["conv_w"], rw["conv_b"], rw["w_a"], rw["b_a"], rw["w_i"], rw["b_i"], rw["lam"])


def _rglru_step_kernel(gate_ref, u_ref, conv0_ref, h0_ref, cw_ref, cb_ref, wa_ref, ba_ref, wi_ref, bi_ref,
                       lam_ref, hg_ref, hlast_ref):
    s, nb, db = u_ref.shape
    ue = jnp.concatenate([conv0_ref[...], u_ref[...]], axis=0)
    cw = cw_ref[...]
    conv = cb_ref[...][None]
    for k in range(CONV_WIDTH):
        conv = conv + ue[k:k + s] * cw[k:k + 1, :][None]
    a, b = _rglru_gates(conv.reshape(s * nb, db), wa_ref, ba_ref[...], wi_ref, bi_ref[...], lam_ref[...])
    a = a.reshape(s, nb, db)
    b = b.reshape(s, nb, db)
    h = h0_ref[...]
    for t in range(s):
        h = a[t] * h + b[t]
        hg_ref[t] = (h * gate_ref[t]).astype(BF16)
    hlast_ref[...] = h


def _rglru_step(gate, u, conv0, h0, rw, db):
    s, nb, d = u.shape
    kb = db // RNN_BLOCK
    cube = pl.BlockSpec((s, nb, db), lambda j: (0, 0, j))
    vec = pl.BlockSpec((1, db), lambda j: (0, j))
    blk = pl.BlockSpec((kb, RNN_BLOCK, RNN_BLOCK), lambda j: (j, 0, 0))
    return pl.pallas_call(
        _rglru_step_kernel,
        grid=(d // db,),
        in_specs=[cube, cube, pl.BlockSpec((CONV_WIDTH - 1, nb, db), lambda j: (0, 0, j)),
                  pl.BlockSpec((nb, db), lambda j: (0, j)),
                  pl.BlockSpec((CONV_WIDTH, db), lambda j: (0, j)), vec, blk, vec, blk, vec, vec],
        out_specs=[cube, pl.BlockSpec((nb, db), lambda j: (0, j))],
        out_shape=[jax.ShapeDtypeStruct((s, nb, d), BF16), jax.ShapeDtypeStruct((nb, d), F32)],
        compiler_params=_params("parallel"),
        name="rglru_step",
    )(gate, u, conv0, h0, rw["conv_w"], rw["conv_b"], rw["w_a"], rw["b_a"], rw["w_i"], rw["b_i"], rw["lam"])


def _mm_ln_kernel(x_ref, w_ref, res_ref, g_ref, b_ref, o_ref):
    y = _dot(x_ref[...], w_ref[...])
    o_ref[...] = _layer_norm(DEEPNORM_ALPHA * res_ref[...] + y, g_ref[...], b_ref[...])


def _mm_ln(x, w, res, g, b, tm):
    t, k = x.shape
    n = w.shape[1]
    return pl.pallas_call(
        _mm_ln_kernel,
        grid=(t // tm,),
        in_specs=[pl.BlockSpec((tm, k), lambda i: (i, 0)), _full((k, n)),
                  pl.BlockSpec((tm, n), lambda i: (i, 0)), _full((1, n)), _full((1, n))],
        out_specs=pl.BlockSpec((tm, n), lambda i: (i, 0)),
        out_shape=jax.ShapeDtypeStruct((t, n), F32),
        compiler_params=_params("parallel"),
        name="mm_ln",
    )(x, w, res, g, b)


def _ffn_kernel(te_ref, x_ref, wg_ref, wu_ref, wd_ref, *rest, post_norm):
    del te_ref
    if post_norm:
        g_ref, b_ref, o_ref, xb_ref, acc_ref = rest
    else:
        o_ref, xb_ref, acc_ref = rest
    c = pl.program_id(1)

    @pl.when(c == 0)
    def _():
        xb_ref[...] = x_ref[...].astype(BF16)
        acc_ref[...] = jnp.zeros_like(acc_ref)

    xb = xb_ref[...]
    h = (jax.nn.silu(_dot(xb, wg_ref[0])) * _dot(xb, wu_ref[0])).astype(BF16)
    acc_ref[...] += _dot(h, wd_ref[0])

    @pl.when(c == pl.num_programs(1) - 1)
    def _():
        if post_norm:
            o_ref[...] = _layer_norm(DEEPNORM_ALPHA * x_ref[...] + acc_ref[...], g_ref[...], b_ref[...])
        else:
            o_ref[...] = acc_ref[...]


def _ffn(tile_expert, x, wg, wu, wd, norm, tm, tf):
    r, d = x.shape
    f = wg.shape[2]
    in_specs = [
        pl.BlockSpec((tm, d), lambda i, c, te: (i, 0)),
        pl.BlockSpec((1, d, tf), lambda i, c, te: (te[i], 0, c)),
        pl.BlockSpec((1, d, tf), lambda i, c, te: (te[i], 0, c)),
        pl.BlockSpec((1, tf, d), lambda i, c, te: (te[i], c, 0)),
    ]
    args = [x, wg, wu, wd]
    if norm is not None:
        in_specs += [pl.BlockSpec((1, d), lambda i, c, te: (0, 0))] * 2
        args += list(norm)
    return pl.pallas_call(
        functools.partial(_ffn_kernel, post_norm=norm is not None),
        grid_spec=pltpu.PrefetchScalarGridSpec(
            num_scalar_prefetch=1,
            grid=(r // tm, f // tf),
            in_specs=in_specs,
            out_specs=pl.BlockSpec((tm, d), lambda i, c, te: (i, 0)),
            scratch_shapes=[pltpu.VMEM((tm, d), BF16), pltpu.VMEM((tm, d), F32)],
        ),
        out_shape=jax.ShapeDtypeStruct((r, d), F32),
        compiler_params=_params("parallel", "arbitrary"),
        name="ffn",
    )(tile_expert, *args)


def _ple_kernel(x_ref, p_ref, wg_ref, wp_ref, o_ref):
    x = x_ref[...]
    a = _dot(x.astype(BF16), wg_ref[...])
    c = _dot(p_ref[...].astype(BF16), wp_ref[...])
    o_ref[...] = x + jax.nn.sigmoid(a) * c


def _ple(x, p, wg, wp, tm):
    t, d = x.shape
    dp = p.shape[1]
    return pl.pallas_call(
        _ple_kernel,
        grid=(t // tm,),
        in_specs=[pl.BlockSpec((tm, d), lambda i: (i, 0)), pl.BlockSpec((tm, dp), lambda i: (i, 0)),
                  _full((d, d)), _full((dp, d))],
        out_specs=pl.BlockSpec((tm, d), lambda i: (i, 0)),
        out_shape=jax.ShapeDtypeStruct((t, d), F32),
        compiler_params=_params("parallel"),
        name="ple",
    )(x, p, wg, wp)


def _kva_kernel(x_ref, wc_ref, wr_ref, wrs_ref, g_ref, cos_ref, sin_ref, ckv_ref, kpe_ref, kv_ref, vt_ref):
    xb = x_ref[...].astype(BF16)
    ckv = _rms_norm(_dot(xb, wc_ref[...]), g_ref[...])
    kpe = _dot(xb, wr_ref[...]) * cos_ref[...] + _dot(xb, wrs_ref[...]) * sin_ref[...]
    ckv_ref[...] = ckv
    kpe_ref[...] = kpe
    kv_ref[:, :KV_LORA_RANK] = ckv.astype(BF16)
    kv_ref[:, KV_LORA_RANK:] = kpe.astype(BF16)
    vt_ref[...] = ckv.T.astype(BF16)


def _kva(x, wc, wr, wrs, g, cos, sin, tm):
    t, d = x.shape
    nper = cos.shape[0] // tm
    tab = pl.BlockSpec((tm, QK_ROPE_DIM), lambda i: (i % nper, 0))
    out = lambda n: pl.BlockSpec((tm, n), lambda i: (i, 0))
    return pl.pallas_call(
        _kva_kernel,
        grid=(t // tm,),
        in_specs=[pl.BlockSpec((tm, d), lambda i: (i, 0)), _full(wc.shape), _full(wr.shape), _full(wrs.shape),
                  _full(g.shape), tab, tab],
        out_specs=[out(KV_LORA_RANK), out(QK_ROPE_DIM), out(QK_WIDTH),
                   pl.BlockSpec((KV_LORA_RANK, tm), lambda i: (0, i))],
        out_shape=[jax.ShapeDtypeStruct((t, KV_LORA_RANK), F32), jax.ShapeDtypeStruct((t, QK_ROPE_DIM), F32),
                   jax.ShapeDtypeStruct((t, QK_WIDTH), BF16), jax.ShapeDtypeStruct((KV_LORA_RANK, t), BF16)],
        compiler_params=_params("parallel"),
        name="kv_latent",
    )(x, wc, wr, wrs, g, cos, sin)


def _q_kernel(x_ref, wqa_ref, qg_ref, wqb_ref, wuk_ref, cos_ref, sin_ref, q_ref):
    tm = x_ref.shape[0]
    nope = N_HEADS * QK_NOPE_DIM
    pe = N_HEADS * QK_ROPE_DIM
    c = KV_LORA_RANK
    r = QK_ROPE_DIM
    cq = _rms_norm(_dot(x_ref[...].astype(BF16), wqa_ref[...]), qg_ref[...])
    q = _dot(cq.astype(BF16), wqb_ref[...])
    q_pe = (q[:, nope:nope + pe] * cos_ref[...] + q[:, nope + pe:nope + 2 * pe] * sin_ref[...]) * QK_SCALE
    qn = q[:, :nope].astype(BF16)
    pair = 2 * QK_NOPE_DIM
    for j in range(N_HEADS // 2):
        ql = _dot(qn[:, j * pair:(j + 1) * pair], wuk_ref[j]) * QK_SCALE
        for k in range(2):
            h = 2 * j + k
            q_ref[h * tm:(h + 1) * tm, :c] = ql[:, k * c:(k + 1) * c].astype(q_ref.dtype)
            q_ref[h * tm:(h + 1) * tm, c:] = q_pe[:, h * r:(h + 1) * r].astype(q_ref.dtype)


def _q_proj(x, wqa, qg, wqb, wuk, cos, sin, tm, dtype):
    t, d = x.shape
    nper = cos.shape[0] // tm
    pe = N_HEADS * QK_ROPE_DIM
    tab = pl.BlockSpec((tm, pe), lambda i: (i % nper, 0))
    return pl.pallas_call(
        _q_kernel,
        grid=(t // tm,),
        in_specs=[pl.BlockSpec((tm, d), lambda i: (i, 0)), _full(wqa.shape), _full(qg.shape), _full(wqb.shape),
                  _full(wuk.shape), tab, tab],
        out_specs=pl.BlockSpec((N_HEADS * tm, QK_WIDTH), lambda i: (i, 0)),
        out_shape=jax.ShapeDtypeStruct((N_HEADS * t, QK_WIDTH), dtype),
        compiler_params=_params("parallel"),
        name="q_proj",
    )(x, wqa, qg, wqb, wuk, cos, sin)


def _widen(stat, width):
    return jnp.concatenate([stat] * (width // LANES), axis=1)


def _softmax_step(s, kv, m_s, l_s, acc):
    m_prev = m_s[...]
    m_new = jnp.maximum(m_prev, jnp.max(s, -1, keepdims=True))
    corr = jnp.exp2(m_prev - m_new)
    p = jnp.exp2(s - _widen(m_new, s.shape[1]))
    l_s[...] = l_s[...] * corr + jnp.sum(p, -1, keepdims=True)
    acc[...] = acc[...] * _widen(corr, KV_LORA_RANK) + _dot(p.astype(BF16), kv)
    m_s[...] = m_new


def _attn_kernel(q_ref, kv_ref, vt_ref, o_ref, m_s, l_s, acc, *, tq, tk):
    i = pl.program_id(1)
    j = pl.program_id(2)

    @pl.when(j == 0)
    def _():
        m_s[...] = jnp.full_like(m_s, -jnp.inf)
        l_s[...] = jnp.zeros_like(l_s)
        acc[...] = jnp.zeros_like(acc)

    def step(masked):
        st = _dot_nt(kv_ref[...], q_ref[...])
        if masked:
            k_pos = j * tk + lax.broadcasted_iota(jnp.int32, st.shape, 0)
            q_pos = i * tq + (lax.broadcasted_iota(jnp.int32, st.shape, 1) & (tq - 1))
            st = jnp.where(k_pos <= q_pos, st, -jnp.inf)
        m_prev = m_s[...]
        m_new = jnp.maximum(m_prev, jnp.max(st, 0, keepdims=True))
        corr = jnp.exp2(m_prev - m_new)
        pt = jnp.exp2(st - m_new)
        l_s[...] = l_s[...] * corr + jnp.sum(pt, 0, keepdims=True)
        acc[...] = acc[...] * corr + _dot(vt_ref[...], pt.astype(BF16))
        m_s[...] = m_new

    first_q = i * tq
    last_k = j * tk + tk - 1

    @pl.when(last_k <= first_q)
    def _():
        step(False)

    @pl.when(jnp.logical_and(last_k > first_q, j * tk <= first_q + tq - 1))
    def _():
        step(True)

    @pl.when(j == pl.num_programs(2) - 1)
    def _():
        o_ref[...] = (acc[...] * (1.0 / l_s[...])).T.astype(o_ref.dtype)


def _attn_prompt(q, kv, vt, nb, s, tq, tk):
    nq, nk = s // tq, s // tk
    assert tq & (tq - 1) == 0
    rows = N_HEADS * tq
    kv_blk = lambda b, i, j: b * nk + jnp.minimum(j, (i * tq + tq - 1) // tk)
    q_row = lambda b, i, j: (b * nq + i, 0)
    return pl.pallas_call(
        functools.partial(_attn_kernel, tq=tq, tk=tk),
        grid=(nb, nq, nk),
        in_specs=[pl.BlockSpec((rows, QK_WIDTH), q_row),
                  pl.BlockSpec((tk, QK_WIDTH), lambda b, i, j: (kv_blk(b, i, j), 0)),
                  pl.BlockSpec((KV_LORA_RANK, tk), lambda b, i, j: (0, kv_blk(b, i, j)))],
        out_specs=pl.BlockSpec((rows, KV_LORA_RANK), q_row),
        out_shape=jax.ShapeDtypeStruct((q.shape[0], KV_LORA_RANK), BF16),
        scratch_shapes=[pltpu.VMEM((1, rows), F32), pltpu.VMEM((1, rows), F32),
                        pltpu.VMEM((KV_LORA_RANK, rows), F32)],
        compiler_params=_params("parallel", "parallel", "arbitrary"),
        name="attn_prompt",
    )(q, kv, vt)


def _attn_paged_kernel(pt_ref, q_ref, cnew_ref, knew_ref, ckv_hbm, kpe_hbm, o_ref, m_s, l_s, acc, kbuf, kpbuf,
                       ck_in, kp_in, sem, *, seqs, pages, s_new):
    g = pl.program_id(0)
    c = pl.program_id(1)
    nc = pl.num_programs(1)
    n_pages = nc * pages
    step = g * nc + c
    rows = N_HEADS * s_new
    page = ck_in.shape[2]

    def page_copies(group, chunk, slot):
        copies = []
        for a in range(seqs):
            for k in range(pages):
                pid = pt_ref[(group * seqs + a) * n_pages + chunk * pages + k]
                copies.append(pltpu.make_async_copy(ckv_hbm.at[pid], ck_in.at[slot, a * pages + k], sem.at[slot, 0]))
                copies.append(pltpu.make_async_copy(kpe_hbm.at[pid], kp_in.at[slot, a * pages + k], sem.at[slot, 1]))
        return copies

    @pl.when(step == 0)
    def _():
        for cp in page_copies(0, 0, 0):
            cp.start()

    @pl.when(step + 1 < pl.num_programs(0) * nc)
    def _():
        nxt = step + 1
        for cp in page_copies(nxt // nc, nxt % nc, nxt % 2):
            cp.start()

    slot = step % 2
    pltpu.make_async_copy(ckv_hbm.at[pl.ds(0, seqs * pages)], ck_in.at[slot], sem.at[slot, 0]).wait()
    pltpu.make_async_copy(kpe_hbm.at[pl.ds(0, seqs * pages)], kp_in.at[slot], sem.at[slot, 1]).wait()
    ck_refs = [ck_in.at[slot, i] for i in range(seqs * pages)]
    kp_refs = [kp_in.at[slot, i] for i in range(seqs * pages)]

    qs = [q_ref[:, a].reshape(rows, QK_WIDTH).astype(BF16) for a in range(seqs)]
    qls = [q[:, :KV_LORA_RANK] for q in qs]
    qps = [q[:, KV_LORA_RANK:] for q in qs]

    @pl.when(c == 0)
    def _():
        for a in range(seqs):
            pad = 2 * SUBLANES - s_new
            cn = jnp.concatenate([cnew_ref[a], jnp.zeros((pad, KV_LORA_RANK), F32)], axis=0).astype(BF16)
            kn = jnp.concatenate([knew_ref[a], jnp.zeros((pad, QK_ROPE_DIM), F32)], axis=0).astype(BF16)
            s = _dot_nt(qls[a], cn) + _dot_nt(qps[a], kn)
            tok = lax.broadcasted_iota(jnp.int32, s.shape, 0) & (s_new - 1)
            key = lax.broadcasted_iota(jnp.int32, s.shape, 1)
            s = jnp.where(key <= tok, s, -jnp.inf)
            m = jnp.max(s, -1, keepdims=True)
            p = jnp.exp2(s - m)
            m_s[a] = jnp.broadcast_to(m, m_s.shape[1:])
            l_s[a] = jnp.broadcast_to(jnp.sum(p, -1, keepdims=True), l_s.shape[1:])
            acc[a] = _dot(p.astype(BF16), cn)

    for a in range(seqs):
        for k in range(pages):
            kbuf[a, k * page:(k + 1) * page, :] = ck_refs[a * pages + k][...].astype(BF16)
            kpbuf[a, :, k * page:(k + 1) * page] = kp_refs[a * pages + k][...].astype(BF16)
        kv = kbuf[a]
        s = _dot_nt(qls[a], kv) + _dot(qps[a], kpbuf[a])
        _softmax_step(s, kv, m_s.at[a], l_s.at[a], acc.at[a])

    @pl.when(c == pl.num_programs(1) - 1)
    def _():
        for a in range(seqs):
            o = acc[a] * _widen(1.0 / l_s[a], KV_LORA_RANK)
            o_ref[:, a] = o.reshape(N_HEADS, s_new, KV_LORA_RANK)


def _attn_paged(q, cnew, knew, cache_ckv, cache_kpe_t, page_table, pages, seqs):
    n_tiles, _, per_tile, s_new, _ = q.shape
    nb = n_tiles * per_tile
    assert s_new == SUBLANES and per_tile % seqs == 0
    groups = per_tile // seqs
    n_pages = page_table.shape[1]
    page = cache_ckv.shape[1]
    c = KV_LORA_RANK
    r = QK_ROPE_DIM
    per_g = lambda g, j, pt: (g, 0, 0)
    q_blk = lambda w: pl.BlockSpec((None, N_HEADS, seqs, s_new, w),
                                   lambda g, j, pt: (g // groups, 0, g % groups, 0, 0))

    in_specs = [q_blk(QK_WIDTH), pl.BlockSpec((seqs, s_new, c), per_g), pl.BlockSpec((seqs, s_new, r), per_g),
                pl.BlockSpec(memory_space=pl.ANY), pl.BlockSpec(memory_space=pl.ANY)]
    rows = N_HEADS * s_new
    assert n_pages % pages == 0
    return pl.pallas_call(
        functools.partial(_attn_paged_kernel, seqs=seqs, pages=pages, s_new=s_new),
        grid_spec=pltpu.PrefetchScalarGridSpec(
            num_scalar_prefetch=1,
            grid=(nb // seqs, n_pages // pages),
            in_specs=in_specs,
            out_specs=q_blk(c),
            scratch_shapes=[pltpu.VMEM((seqs, rows, LANES), F32), pltpu.VMEM((seqs, rows, LANES), F32),
                            pltpu.VMEM((seqs, rows, c), F32), pltpu.VMEM((seqs, pages * page, c), BF16),
                            pltpu.VMEM((seqs, r, pages * page), BF16),
                            pltpu.VMEM((2, seqs * pages, page, c), F32), pltpu.VMEM((2, seqs * pages, r, page), F32),
                            pltpu.SemaphoreType.DMA((2, 2))],
        ),
        out_shape=jax.ShapeDtypeStruct(q.shape[:-1] + (c,), F32),
        compiler_params=_params("arbitrary", "arbitrary"),
        name="attn_paged",
    )(page_table.reshape(-1), q, cnew, knew, cache_ckv, cache_kpe_t)


def _mla_out_kernel(o_ref, wuv_ref, wo_ref, res_ref, g_ref, b_ref, out_ref):
    tm = res_ref.shape[0]
    head = lambda h: o_ref[h * tm:(h + 1) * tm, :].astype(BF16)
    v = jnp.concatenate(
        [_dot(jnp.concatenate([head(2 * j), head(2 * j + 1)], axis=1), wuv_ref[j]) for j in range(N_HEADS // 2)],
        axis=1).astype(BF16)
    y = _dot(v, wo_ref[...])
    out_ref[...] = _layer_norm(DEEPNORM_ALPHA * res_ref[...] + y, g_ref[...], b_ref[...])


def _mla_out(o, wuv, wo, res, g, b, tm):
    t, n = res.shape
    return pl.pallas_call(
        _mla_out_kernel,
        grid=(t // tm,),
        in_specs=[pl.BlockSpec((N_HEADS * tm, o.shape[1]), lambda i: (i, 0)), _full(wuv.shape), _full(wo.shape),
                  pl.BlockSpec((tm, n), lambda i: (i, 0)), _full((1, n)), _full((1, n))],
        out_specs=pl.BlockSpec((tm, n), lambda i: (i, 0)),
        out_shape=jax.ShapeDtypeStruct((t, n), F32),
        compiler_params=_params("parallel"),
        name="mla_out",
    )(o, wuv, wo, res, g, b)


def _router_kernel(x_ref, w_ref, idx_ref, wt_ref):
    x = x_ref[...]
    w = w_ref[...]
    xh = x.astype(BF16)
    xl = (x - xh.astype(F32)).astype(BF16)
    wh = w.astype(BF16)
    wl = (w - wh.astype(F32)).astype(BF16)
    lg = (_dot_nt(wh, xh) + _dot_nt(wh, xl) + _dot_nt(wl, xh))[:N_EXPERTS]
    e = lax.broadcasted_iota(jnp.int32, lg.shape, 0).astype(F32)
    none = float(N_EXPERTS)
    v1 = jnp.max(lg, 0, keepdims=True)
    i1 = jnp.min(jnp.where(lg == v1, e, none), 0, keepdims=True)
    lg2 = jnp.where(e == i1, -jnp.inf, lg)
    v2 = jnp.max(lg2, 0, keepdims=True)
    i2 = jnp.min(jnp.where(lg2 == v2, e, none), 0, keepdims=True)
    ex = jnp.exp(v2 - v1)
    den = 1.0 + ex
    idx_ref[...] = jnp.concatenate([i1, i2], axis=0).astype(jnp.int32)
    wt_ref[...] = jnp.concatenate([1.0 / den, ex / den], axis=0)


def _router(x, w_t, tm):
    t, d = x.shape
    return pl.pallas_call(
        _router_kernel,
        grid=(t // tm,),
        in_specs=[pl.BlockSpec((tm, d), lambda i: (i, 0)), _full(w_t.shape)],
        out_specs=[pl.BlockSpec((2, tm), lambda i: (0, i))] * 2,
        out_shape=[jax.ShapeDtypeStruct((2, t), jnp.int32), jax.ShapeDtypeStruct((2, t), F32)],
        compiler_params=_params("parallel"),
        name="router",
    )(x, w_t)


def _dispatch_kernel(tok_ref, x_hbm, o_ref, sem):
    rows = o_ref.shape[0]
    base = pl.program_id(0) * rows

    def issue(pair, carry):
        for prio in range(2):
            r = 2 * pair + prio
            t = tok_ref[base + r]
            pltpu.make_async_copy(x_hbm.at[pl.ds(t, 1), :], o_ref.at[pl.ds(r, 1), :], sem).start(priority=prio)
        return carry

    lax.fori_loop(0, rows // 2, issue, 0, unroll=4)
    pltpu.make_async_copy(x_hbm.at[pl.ds(0, rows), :], o_ref, sem).wait()


def _dispatch(slot_token, x, rows):
    ns = slot_token.shape[0]
    d = x.shape[1]
    return pl.pallas_call(
        _dispatch_kernel,
        grid_spec=pltpu.PrefetchScalarGridSpec(
            num_scalar_prefetch=1,
            grid=(ns // rows,),
            in_specs=[pl.BlockSpec(memory_space=pl.ANY)],
            out_specs=pl.BlockSpec((rows, d), lambda i, tok: (i, 0)),
            scratch_shapes=[pltpu.SemaphoreType.DMA],
        ),
        out_shape=jax.ShapeDtypeStruct((ns, d), x.dtype),
        compiler_params=_params("arbitrary"),
        name="moe_dispatch",
    )(slot_token, x)


def _combine_kernel(pos_ref, y_hbm, x_ref, w1_ref, w2_ref, g_ref, b_ref, o_ref, y1, y2, sem):
    rows = o_ref.shape[0]
    t = pl.num_programs(0) * rows
    base = pl.program_id(0) * rows

    def issue(r, carry):
        p1 = pos_ref[base + r]
        p2 = pos_ref[t + base + r]
        pltpu.make_async_copy(y_hbm.at[pl.ds(p1, 1), :], y1.at[pl.ds(r, 1), :], sem.at[0]).start(priority=0)
        pltpu.make_async_copy(y_hbm.at[pl.ds(p2, 1), :], y2.at[pl.ds(r, 1), :], sem.at[1]).start(priority=1)
        return carry

    lax.fori_loop(0, rows, issue, 0, unroll=8)
    pltpu.make_async_copy(y_hbm.at[pl.ds(0, rows), :], y1, sem.at[0]).wait()
    pltpu.make_async_copy(y_hbm.at[pl.ds(0, rows), :], y2, sem.at[1]).wait()
    ff = w1_ref[...] * y1[...] + w2_ref[...] * y2[...]
    o_ref[...] = _layer_norm(DEEPNORM_ALPHA * x_ref[...] + ff, g_ref[...], b_ref[...])


def _combine(pos, y, x, w1, w2, g, b, rows):
    t, d = x.shape
    row = pl.BlockSpec((rows, d), lambda i, p: (i, 0))
    col = pl.BlockSpec((rows, 1), lambda i, p: (i, 0))
    vec = pl.BlockSpec((1, d), lambda i, p: (0, 0))
    return pl.pallas_call(
        _combine_kernel,
        grid_spec=pltpu.PrefetchScalarGridSpec(
            num_scalar_prefetch=1,
            grid=(t // rows,),
            in_specs=[pl.BlockSpec(memory_space=pl.ANY), row, col, col, vec, vec],
            out_specs=row,
            scratch_shapes=[pltpu.VMEM((rows, d), F32), pltpu.VMEM((rows, d), F32),
                            pltpu.SemaphoreType.DMA((2,))],
        ),
        out_shape=jax.ShapeDtypeStruct((t, d), F32),
        compiler_params=_params("arbitrary"),
        name="moe_combine",
    )(pos, y, x, w1, w2, g, b)


def _moe(x, w_router_t, wg, wu, wd, g, b, tm, tile, tf):
    t, d = x.shape
    idx, wts = _router(x, w_router_t, tm)
    e_flat = idx.reshape(-1)
    onehot = (e_flat[:, None] == jnp.arange(N_EXPERTS, dtype=jnp.int32)[None, :]).astype(jnp.int32)
    csum = jnp.cumsum(onehot, axis=0)
    rank = jnp.sum(csum * onehot, axis=1) - 1
    counts = csum[-1]
    padded = ((counts + tile - 1) // tile) * tile
    ends = jnp.cumsum(padded)
    starts = ends - padded
    pos = (starts[e_flat] + rank).astype(jnp.int32)
    n_slots = 2 * t + N_EXPERTS * tile
    token = jnp.arange(2 * t, dtype=jnp.int32) % t
    slot_token = jnp.zeros((n_slots,), jnp.int32).at[pos].set(token)
    tile_start = jnp.arange(n_slots // tile, dtype=jnp.int32) * tile
    tile_expert = jnp.minimum(
        jnp.sum((tile_start[:, None] >= ends[None, :]).astype(jnp.int32), axis=1), N_EXPERTS - 1)

    xs = _dispatch(slot_token, x, tile)
    ys = _ffn(tile_expert, xs, wg, wu, wd, None, tile, tf)
    return _combine(pos, ys, x, wts[0][:, None], wts[1][:, None], g, b, tm)


def _swap_halves(w):
    half = w.shape[-1] // 2
    return jnp.concatenate([w[..., half:], w[..., :half]], axis=-1)


def _prep_weights(w):
    out = {}
    row = lambda v: v.reshape(1, -1).astype(F32)
    out["ln_mix"] = [(row(w["ln_mix_g"][i]), row(w["ln_mix_b"][i])) for i in range(DEPTH)]
    out["ln_ffn"] = [(row(w["ln_ffn_g"][i]), row(w["ln_ffn_b"][i])) for i in range(DEPTH)]
    out["rg_wg"] = w["rg_w_gate"][0].astype(BF16)
    out["rg_wx"] = w["rg_w_x"][0].astype(BF16)
    out["rg"] = {
        "conv_w": w["rg_conv_w"][0], "conv_b": row(w["rg_conv_b"][0]),
        "w_a": w["rg_w_a"][0].astype(BF16), "b_a": row(w["rg_b_a"][0]),
        "w_i": w["rg_w_i"][0].astype(BF16), "b_i": row(w["rg_b_i"][0]),
        "lam": row(w["rg_lambda"][0]),
    }
    out["rg_wout"] = w["rg_w_out"][0].astype(BF16)
    out["ffn"] = (w["ffn_w_gate"].astype(BF16), w["ffn_w_up"].astype(BF16), w["ffn_w_down"].astype(BF16))
    out["moe"] = (w["moe_w_gate"][0].astype(BF16), w["moe_w_up"][0].astype(BF16), w["moe_w_down"][0].astype(BF16))
    e = w["moe_w_router"].shape[-1]
    out["router_t"] = jnp.concatenate(
        [w["moe_w_router"][0].T, jnp.zeros((2 * SUBLANES - e, w["moe_w_router"].shape[1]), F32)], axis=0)
    out["ple_wg"] = w["ple_w_gate"].astype(BF16)
    out["ple_wp"] = w["ple_w_proj"].astype(BF16)
    kv = w["kv_w_a"]
    out["kv_wc"] = kv[:, :KV_LORA_RANK].astype(BF16)
    out["kv_wr"] = kv[:, KV_LORA_RANK:].astype(BF16)
    out["kv_wrs"] = _swap_halves(kv[:, KV_LORA_RANK:]).astype(BF16)
    out["kv_g"] = row(w["kv_norm_g"])
    out["q_wa"] = w["mla_w_q_a"][0].astype(BF16)
    out["q_g"] = row(w["mla_q_norm_g"][0])
    qb = w["mla_w_q_b"][0].reshape(-1, N_HEADS, QK_NOPE_DIM + QK_ROPE_DIM)
    lora = qb.shape[0]
    q_nope = qb[:, :, :QK_NOPE_DIM].reshape(lora, -1)
    q_rope = qb[:, :, QK_NOPE_DIM:]
    out["q_wb"] = jnp.concatenate(
        [q_nope, q_rope.reshape(lora, -1), _swap_halves(q_rope).reshape(lora, -1)], axis=1).astype(BF16)
    uk = jnp.transpose(w["kv_w_uk"], (1, 2, 0))
    zk = jnp.zeros_like(uk[0])
    out["wuk"] = jnp.stack([
        jnp.concatenate([jnp.concatenate([uk[2 * j], zk], axis=1), jnp.concatenate([zk, uk[2 * j + 1]], axis=1)],
                        axis=0) for j in range(N_HEADS // 2)]).astype(BF16)
    uv = jnp.transpose(w["kv_w_uv"], (1, 0, 2))
    zv = jnp.zeros_like(uv[0])
    out["wuv"] = jnp.stack([
        jnp.concatenate([jnp.concatenate([uv[2 * j], zv], axis=1), jnp.concatenate([zv, uv[2 * j + 1]], axis=1)],
                        axis=0) for j in range(N_HEADS // 2)]).astype(BF16)
    out["wo"] = w["mla_w_o"][0].astype(BF16)
    return out


def _rope_tables(pos, repeat):
    inv = ROPE_THETA ** (-jnp.arange(0, QK_ROPE_DIM, 2, dtype=F32) / QK_ROPE_DIM)
    ang = pos[:, None] * inv[None, :]
    cos, sin = jnp.cos(ang), jnp.sin(ang)
    cos_k = jnp.tile(jnp.concatenate([cos, cos], axis=-1), (repeat, 1))
    sin_k = jnp.tile(jnp.concatenate([-sin, sin], axis=-1), (repeat, 1))
    return cos_k, sin_k, jnp.tile(cos_k, (1, N_HEADS)), jnp.tile(sin_k, (1, N_HEADS))


def _tile_rows(t, want):
    tm = min(want, t)
    assert t % tm == 0
    return tm


def _trunk(x3, p4, conv0, rnn0, pos, pw, paged):
    nb, s, d = x3.shape
    t = nb * s
    x = x3.reshape(t, d)
    p = p4.reshape(DEPTH, t, -1)
    tm = _tile_rows(t, 512)

    gate, u = _rg_in(x, pw["rg_wg"], pw["rg_wx"], tm)
    if paged is None:
        conv_pad = jnp.concatenate(
            [jnp.zeros((nb, SUBLANES - (CONV_WIDTH - 1), D_RNN), F32), conv0], axis=1)
        hg, h_last = _rglru_seq(gate, u, conv_pad, rnn0[:, None, :], pw["rg"], nb, s, min(s, 256), 512)
        h_last = h_last[:, 0, :]
        tabs = _rope_tables(pos, 1)
    else:
        to_tm = lambda a: jnp.transpose(a.reshape(nb, s, -1), (1, 0, 2))
        hg, h_last = _rglru_step(to_tm(gate), to_tm(u), jnp.transpose(conv0, (1, 0, 2)), rnn0, pw["rg"], 512)
        hg = jnp.transpose(hg, (1, 0, 2)).reshape(t, -1)
        tabs = _rope_tables(pos, nb)
    conv_state = u.reshape(nb, s, -1)[:, s - (CONV_WIDTH - 1):, :]
    x = _mm_ln(hg, pw["rg_wout"], x, *pw["ln_mix"][0], tm)
    x = _ffn(jnp.zeros((t // tm,), jnp.int32), x, *pw["ffn"], pw["ln_ffn"][0], tm, pw["ffn"][0].shape[2] // 2)
    x = _ple(x, p[0], pw["ple_wg"][0], pw["ple_wp"][0], tm)
    cos_k, sin_k, cos_q, sin_q = tabs
    ckv, kpe, kv, vt = _kva(x, pw["kv_wc"], pw["kv_wr"], pw["kv_wrs"], pw["kv_g"], cos_k, sin_k, tm)

    tq = _tile_rows(t, 256)
    if paged is None:
        q = _q_proj(x, pw["q_wa"], pw["q_g"], pw["q_wb"], pw["wuk"], cos_q, sin_q, tq, BF16)
        o = _attn_prompt(q, kv, vt, nb, s, tq, 256)
    else:
        cache_ckv, cache_kpe, page_table = paged
        q = _q_proj(x, pw["q_wa"], pw["q_g"], pw["q_wb"], pw["wuk"], cos_q, sin_q, tq, F32)
        o = _attn_paged(q.reshape(t // tq, N_HEADS, tq // s, s, QK_WIDTH), ckv.reshape(nb, s, -1),
                        kpe.reshape(nb, s, -1), cache_ckv, jnp.swapaxes(cache_kpe, 1, 2), page_table, 16, 2)
        o = o.reshape(t * N_HEADS, KV_LORA_RANK)
    x = _mla_out(o, pw["wuv"], pw["wo"], x, *pw["ln_mix"][1], tq)
    x = _moe(x, pw["router_t"], *pw["moe"], *pw["ln_ffn"][1], tq, 512 if paged is None else 256,
             pw["moe"][0].shape[2] // 2)
    x = _ple(x, p[1], pw["ple_wg"][1], pw["ple_wp"][1], tm)
    return (x.reshape(nb, s, d), conv_state[None], h_last[None], ckv.reshape(nb, s, -1), kpe.reshape(nb, s, -1))


def kernel(x_prompt, x_sample, p_prompt, p_sample, state_conv, state_rnn, cache_ckv, cache_kpe, page_table, ln_mix_g, ln_mix_b, ln_ffn_g, ln_ffn_b, rg_w_gate, rg_w_x, rg_conv_w, rg_conv_b, rg_w_a, rg_b_a, rg_w_i, rg_b_i, rg_lambda, rg_w_out, mla_w_q_a, mla_q_norm_g, mla_w_q_b, mla_w_o, kv_w_a, kv_norm_g, kv_w_uk, kv_w_uv, ffn_w_gate, ffn_w_up, ffn_w_down, moe_w_router, moe_w_gate, moe_w_up, moe_w_down, ple_w_gate, ple_w_proj):
    w = dict(
        ln_mix_g=ln_mix_g, ln_mix_b=ln_mix_b, ln_ffn_g=ln_ffn_g, ln_ffn_b=ln_ffn_b,
        rg_w_gate=rg_w_gate, rg_w_x=rg_w_x, rg_conv_w=rg_conv_w, rg_conv_b=rg_conv_b, rg_w_a=rg_w_a,
        rg_b_a=rg_b_a, rg_w_i=rg_w_i, rg_b_i=rg_b_i, rg_lambda=rg_lambda, rg_w_out=rg_w_out,
        mla_w_q_a=mla_w_q_a, mla_q_norm_g=mla_q_norm_g, mla_w_q_b=mla_w_q_b, mla_w_o=mla_w_o,
        kv_w_a=kv_w_a, kv_norm_g=kv_norm_g, kv_w_uk=kv_w_uk, kv_w_uv=kv_w_uv,
        ffn_w_gate=ffn_w_gate, ffn_w_up=ffn_w_up, ffn_w_down=ffn_w_down,
        moe_w_router=moe_w_router, moe_w_gate=moe_w_gate, moe_w_up=moe_w_up, moe_w_down=moe_w_down,
        ple_w_gate=ple_w_gate, ple_w_proj=ple_w_proj)
    pw = _prep_weights(w)
    nb, s = x_prompt.shape[:2]
    dec_s = x_sample.shape[1]
    past_len = page_table.shape[1] * cache_ckv.shape[1]
    n_a = state_conv.shape[0]
    assert n_a == 1
    conv0_p = jnp.zeros((nb, CONV_WIDTH - 1, D_RNN), state_conv.dtype)
    rnn0_p = jnp.zeros((nb, D_RNN), state_rnn.dtype)
    y_p, conv_p, rnn_p, ckv_p, kpe_p = _trunk(
        x_prompt, p_prompt, conv0_p, rnn0_p, jnp.arange(s, dtype=F32), pw, None)
    y_s, conv_s, rnn_s, ckv_s, kpe_s = _trunk(
        x_sample, p_sample, state_conv[0], state_rnn[0], past_len + jnp.arange(dec_s, dtype=F32), pw,
        (cache_ckv, cache_kpe, page_table))
    return (y_p, y_s, conv_p, rnn_p, ckv_p, kpe_p, conv_s, rnn_s, ckv_s, kpe_s)
```

```python
import functools

import jax
import jax.numpy as jnp
from jax import lax
from jax.experimental import pallas as pl
from jax.experimental.pallas import tpu as pltpu

F32 = jnp.float32
BF16 = jnp.bfloat16

D_RNN = 1536
RNN_BLOCK = 128
CONV_WIDTH = 4
LRU_C = 8.0
N_HEADS = 16
KV_LORA_RANK = 256
QK_NOPE_DIM = 64
QK_ROPE_DIM = 32
V_HEAD_DIM = 64
ROPE_THETA = 10000.0
SOFTMAX_SCALE = (QK_NOPE_DIM + QK_ROPE_DIM) ** -0.5
N_EXPERTS = 8
LN_EPS = 1e-5
RMS_EPS = 1e-6
DEPTH = 2
DEEPNORM_ALPHA = (2.0 * DEPTH) ** 0.25

QK_WIDTH = KV_LORA_RANK + QK_ROPE_DIM
QK_SCALE = SOFTMAX_SCALE * 1.4426950408889634

SUBLANES = 8
LANES = 128
VMEM_LIMIT = 48 * 1024 * 1024


def _params(*sem):
    return pltpu.CompilerParams(dimension_semantics=sem, vmem_limit_bytes=VMEM_LIMIT)


def _dot(a, b):
    return jnp.dot(a, b, preferred_element_type=F32)


def _dot_nt(a, b):
    return lax.dot_general(a, b, (((1,), (1,)), ((), ())), preferred_element_type=F32)


def _layer_norm(z, g, b):
    mu = jnp.mean(z, -1, keepdims=True)
    d = z - mu
    var = jnp.mean(d * d, -1, keepdims=True)
    return d * lax.rsqrt(var + LN_EPS) * g + b


def _rms_norm(z, g):
    return z * lax.rsqrt(jnp.mean(z * z, -1, keepdims=True) + RMS_EPS) * g


def _full(shape):
    n = len(shape)
    return pl.BlockSpec(shape, lambda *_: (0,) * n)


def _rg_in_kernel(x_ref, wg_ref, wx_ref, gate_ref, u_ref):
    xb = x_ref[...].astype(BF16)
    gate_ref[...] = jax.nn.gelu(_dot(xb, wg_ref[...]))
    u_ref[...] = _dot(xb, wx_ref[...])


def _rg_in(x, wg, wx, tm):
    t, d = x.shape
    n = wg.shape[1]
    return pl.pallas_call(
        _rg_in_kernel,
        grid=(t // tm,),
        in_specs=[pl.BlockSpec((tm, d), lambda i: (i, 0)), _full((d, n)), _full((d, n))],
        out_specs=[pl.BlockSpec((tm, n), lambda i: (i, 0))] * 2,
        out_shape=[jax.ShapeDtypeStruct((t, n), F32)] * 2,
        compiler_params=_params("parallel"),
        name="rg_in",
    )(x, wg, wx)


def _rglru_gates(conv, wa_ref, ba, wi_ref, bi, lam):
    nblk = conv.shape[1] // RNN_BLOCK
    cb = conv.astype(BF16)
    ra = jnp.concatenate(
        [_dot(cb[:, n * RNN_BLOCK:(n + 1) * RNN_BLOCK], wa_ref[n]) for n in range(nblk)], axis=1)
    ia = jnp.concatenate(
        [_dot(cb[:, n * RNN_BLOCK:(n + 1) * RNN_BLOCK], wi_ref[n]) for n in range(nblk)], axis=1)
    r = jax.nn.sigmoid(ra + ba)
    i = jax.nn.sigmoid(ia + bi)
    z = -lam
    softplus = jnp.maximum(z, 0.0) + jnp.log1p(jnp.exp(-jnp.abs(z)))
    log_a = -LRU_C * r * softplus
    a = jnp.exp(log_a)
    v = 1.0 - a * a
    b = jnp.where(v > 0.0, v * lax.rsqrt(v), 0.0) * (i * conv)
    return a, b


def _rglru_seq_kernel(gate_ref, u_ref, conv0_ref, h0_ref, cw_ref, cb_ref, wa_ref, ba_ref, wi_ref, bi_ref,
                      lam_ref, hg_ref, hlast_ref, ubuf, hcar, abuf, bbuf):
    c = pl.program_id(2)
    tc, db = u_ref.shape
    halo = SUBLANES

    @pl.when(c == 0)
    def _():
        ubuf[0:halo, :] = conv0_ref[0]
        hcar[...] = jnp.broadcast_to(h0_ref[0], hcar.shape)

    ubuf[halo:halo + tc, :] = u_ref[...]
    cw = cw_ref[...]
    conv = cb_ref[...]
    for k in range(CONV_WIDTH):
        off = halo - (CONV_WIDTH - 1) + k
        conv = conv + ubuf[off:off + tc, :] * cw[k:k + 1, :]
    ubuf[0:halo, :] = ubuf[tc:tc + halo, :]

    a, b = _rglru_gates(conv, wa_ref, ba_ref[...], wi_ref, bi_ref[...], lam_ref[...])

    row = lax.broadcasted_iota(jnp.int32, (tc, db), 0) & (SUBLANES - 1)
    shift = 1
    while shift < SUBLANES:
        a_prev = pltpu.roll(a, shift, 0)
        b_prev = pltpu.roll(b, shift, 0)
        keep = row >= shift
        b = jnp.where(keep, a * b_prev + b, b)
        a = jnp.where(keep, a * a_prev, a)
        shift *= 2
    abuf[...] = a
    bbuf[...] = b

    def group(g, h):
        off = pl.multiple_of(g * SUBLANES, SUBLANES)
        hb = abuf[pl.ds(off, SUBLANES), :] * h + bbuf[pl.ds(off, SUBLANES), :]
        bbuf[pl.ds(off, SUBLANES), :] = hb
        return jnp.broadcast_to(hb[SUBLANES - 1:SUBLANES, :], hb.shape)

    h = lax.fori_loop(0, tc // SUBLANES, group, hcar[...])
    hcar[...] = h
    hg_ref[...] = (bbuf[...] * gate_ref[...]).astype(BF16)

    @pl.when(c == pl.num_programs(2) - 1)
    def _():
        hlast_ref[0] = h[0:1, :]


def _rglru_seq(gate, u, conv0, h0, rw, nb, s, tc, db):
    t, d = u.shape
    nc = s // tc
    kb = db // RNN_BLOCK
    row = lambda b, j, c: (b * nc + c, j)
    vec = pl.BlockSpec((1, db), lambda b, j, c: (0, j))
    blk = pl.BlockSpec((kb, RNN_BLOCK, RNN_BLOCK), lambda b, j, c: (j, 0, 0))
    return pl.pallas_call(
        _rglru_seq_kernel,
        grid=(nb, d // db, nc),
        in_specs=[
            pl.BlockSpec((tc, db), row), pl.BlockSpec((tc, db), row),
            pl.BlockSpec((1, SUBLANES, db), lambda b, j, c: (b, 0, j)),
            pl.BlockSpec((1, 1, db), lambda b, j, c: (b, 0, j)),
            pl.BlockSpec((CONV_WIDTH, db), lambda b, j, c: (0, j)), vec, blk, vec, blk, vec, vec,
        ],
        out_specs=[pl.BlockSpec((tc, db), row), pl.BlockSpec((1, 1, db), lambda b, j, c: (b, 0, j))],
        out_shape=[jax.ShapeDtypeStruct((t, d), BF16), jax.ShapeDtypeStruct((nb, 1, d), F32)],
        scratch_shapes=[pltpu.VMEM((tc + SUBLANES, db), F32), pltpu.VMEM((SUBLANES, db), F32),
                        pltpu.VMEM((tc, db), F32), pltpu.VMEM((tc, db), F32)],
        compiler_params=_params("parallel", "parallel", "arbitrary"),
        name="rglru_seq",
    )(gate, u, conv0, h0, rw["conv_w"], rw["conv_b"], rw["w_a"], rw["b_a"], rw["w_i"], rw["b_i"], rw["lam"])


def _rglru_step_kernel(gate_ref, u_ref, conv0_ref, h0_ref, cw_ref, cb_ref, wa_ref, ba_ref, wi_ref, bi_ref,
                       lam_ref, hg_ref, hlast_ref):
    s, nb, db = u_ref.shape
    ue = jnp.concatenate([conv0_ref[...], u_ref[...]], axis=0)
    cw = cw_ref[...]
    conv = cb_ref[...][None]
    for k in range(CONV_WIDTH):
        conv = conv + ue[k:k + s] * cw[k:k + 1, :][None]
    a, b = _rglru_gates(conv.reshape(s * nb, db), wa_ref, ba_ref[...], wi_ref, bi_ref[...], lam_ref[...])
    a = a.reshape(s, nb, db)
    b = b.reshape(s, nb, db)
    h = h0_ref[...]
    for t in range(s):
        h = a[t] * h + b[t]
        hg_ref[t] = (h * gate_ref[t]).astype(BF16)
    hlast_ref[...] = h


def _rglru_step(gate, u, conv0, h0, rw, db):
    s, nb, d = u.shape
    kb = db // RNN_BLOCK
    cube = pl.BlockSpec((s, nb, db), lambda j: (0, 0, j))
    vec = pl.BlockSpec((1, db), lambda j: (0, j))
    blk = pl.BlockSpec((kb, RNN_BLOCK, RNN_BLOCK), lambda j: (j, 0, 0))
    return pl.pallas_call(
        _rglru_step_kernel,
        grid=(d // db,),
        in_specs=[cube, cube, pl.BlockSpec((CONV_WIDTH - 1, nb, db), lambda j: (0, 0, j)),
                  pl.BlockSpec((nb, db), lambda j: (0, j)),
                  pl.BlockSpec((CONV_WIDTH, db), lambda j: (0, j)), vec, blk, vec, blk, vec, vec],
        out_specs=[cube, pl.BlockSpec((nb, db), lambda j: (0, j))],
        out_shape=[jax.ShapeDtypeStruct((s, nb, d), BF16), jax.ShapeDtypeStruct((nb, d), F32)],
        compiler_params=_params("parallel"),
        name="rglru_step",
    )(gate, u, conv0, h0, rw["conv_w"], rw["conv_b"], rw["w_a"], rw["b_a"], rw["w_i"], rw["b_i"], rw["lam"])


def _mm_ln_kernel(x_ref, w_ref, res_ref, g_ref, b_ref, o_ref):
    y = _dot(x_ref[...], w_ref[...])
    o_ref[...] = _layer_norm(DEEPNORM_ALPHA * res_ref[...] + y, g_ref[...], b_ref[...])


def _mm_ln(x, w, res, g, b, tm):
    t, k = x.shape
    n = w.shape[1]
    return pl.pallas_call(
        _mm_ln_kernel,
        grid=(t // tm,),
        in_specs=[pl.BlockSpec((tm, k), lambda i: (i, 0)), _full((k, n)),
                  pl.BlockSpec((tm, n), lambda i: (i, 0)), _full((1, n)), _full((1, n))],
        out_specs=pl.BlockSpec((tm, n), lambda i: (i, 0)),
        out_shape=jax.ShapeDtypeStruct((t, n), F32),
        compiler_params=_params("parallel"),
        name="mm_ln",
    )(x, w, res, g, b)


def _ffn_kernel(te_ref, x_ref, wg_ref, wu_ref, wd_ref, *rest, post_norm):
    del te_ref
    if post_norm:
        g_ref, b_ref, o_ref, xb_ref, acc_ref = rest
    else:
        o_ref, xb_ref, acc_ref = rest
    c = pl.program_id(1)

    @pl.when(c == 0)
    def _():
        xb_ref[...] = x_ref[...].astype(BF16)
        acc_ref[...] = jnp.zeros_like(acc_ref)

    xb = xb_ref[...]
    h = (jax.nn.silu(_dot(xb, wg_ref[0])) * _dot(xb, wu_ref[0])).astype(BF16)
    acc_ref[...] += _dot(h, wd_ref[0])

    @pl.when(c == pl.num_programs(1) - 1)
    def _():
        if post_norm:
            o_ref[...] = _layer_norm(DEEPNORM_ALPHA * x_ref[...] + acc_ref[...], g_ref[...], b_ref[...])
        else:
            o_ref[...] = acc_ref[...]


def _ffn(tile_expert, x, wg, wu, wd, norm, tm, tf):
    r, d = x.shape
    f = wg.shape[2]
    in_specs = [
        pl.BlockSpec((tm, d), lambda i, c, te: (i, 0)),
        pl.BlockSpec((1, d, tf), lambda i, c, te: (te[i], 0, c)),
        pl.BlockSpec((1, d, tf), lambda i, c, te: (te[i], 0, c)),
        pl.BlockSpec((1, tf, d), lambda i, c, te: (te[i], c, 0)),
    ]
    args = [x, wg, wu, wd]
    if norm is not None:
        in_specs += [pl.BlockSpec((1, d), lambda i, c, te: (0, 0))] * 2
        args += list(norm)
    return pl.pallas_call(
        functools.partial(_ffn_kernel, post_norm=norm is not None),
        grid_spec=pltpu.PrefetchScalarGridSpec(
            num_scalar_prefetch=1,
            grid=(r // tm, f // tf),
            in_specs=in_specs,
            out_specs=pl.BlockSpec((tm, d), lambda i, c, te: (i, 0)),
            scratch_shapes=[pltpu.VMEM((tm, d), BF16), pltpu.VMEM((tm, d), F32)],
        ),
        out_shape=jax.ShapeDtypeStruct((r, d), F32),
        compiler_params=_params("parallel", "arbitrary"),
        name="ffn",
    )(tile_expert, *args)


def _ple_kernel(x_ref, p_ref, wg_ref, wp_ref, o_ref):
    x = x_ref[...]
    a = _dot(x.astype(BF16), wg_ref[...])
    c = _dot(p_ref[...].astype(BF16), wp_ref[...])
    o_ref[...] = x + jax.nn.sigmoid(a) * c


def _ple(x, p, wg, wp, tm):
    t, d = x.shape
    dp = p.shape[1]
    return pl.pallas_call(
        _ple_kernel,
        grid=(t // tm,),
        in_specs=[pl.BlockSpec((tm, d), lambda i: (i, 0)), pl.BlockSpec((tm, dp), lambda i: (i, 0)),
                  _full((d, d)), _full((dp, d))],
        out_specs=pl.BlockSpec((tm, d), lambda i: (i, 0)),
        out_shape=jax.ShapeDtypeStruct((t, d), F32),
        compiler_params=_params("parallel"),
        name="ple",
    )(x, p, wg, wp)


def _kva_kernel(x_ref, wc_ref, wr_ref, wrs_ref, g_ref, cos_ref, sin_ref, ckv_ref, kpe_ref, kv_ref, vt_ref):
    xb = x_ref[...].astype(BF16)
    ckv = _rms_norm(_dot(xb, wc_ref[...]), g_ref[...])
    kpe = _dot(xb, wr_ref[...]) * cos_ref[...] + _dot(xb, wrs_ref[...]) * sin_ref[...]
    ckv_ref[...] = ckv
    kpe_ref[...] = kpe
    kv_ref[:, :KV_LORA_RANK] = ckv.astype(BF16)
    kv_ref[:, KV_LORA_RANK:] = kpe.astype(BF16)
    vt_ref[...] = ckv.T.astype(BF16)


def _kva(x, wc, wr, wrs, g, cos, sin, tm):
    t, d = x.shape
    nper = cos.shape[0] // tm
    tab = pl.BlockSpec((tm, QK_ROPE_DIM), lambda i: (i % nper, 0))
    out = lambda n: pl.BlockSpec((tm, n), lambda i: (i, 0))
    return pl.pallas_call(
        _kva_kernel,
        grid=(t // tm,),
        in_specs=[pl.BlockSpec((tm, d), lambda i: (i, 0)), _full(wc.shape), _full(wr.shape), _full(wrs.shape),
                  _full(g.shape), tab, tab],
        out_specs=[out(KV_LORA_RANK), out(QK_ROPE_DIM), out(QK_WIDTH),
                   pl.BlockSpec((KV_LORA_RANK, tm), lambda i: (0, i))],
        out_shape=[jax.ShapeDtypeStruct((t, KV_LORA_RANK), F32), jax.ShapeDtypeStruct((t, QK_ROPE_DIM), F32),
                   jax.ShapeDtypeStruct((t, QK_WIDTH), BF16), jax.ShapeDtypeStruct((KV_LORA_RANK, t), BF16)],
        compiler_params=_params("parallel"),
        name="kv_latent",
    )(x, wc, wr, wrs, g, cos, sin)


def _q_kernel(x_ref, wqa_ref, qg_ref, wqb_ref, wuk_ref, cos_ref, sin_ref, q_ref):
    tm = x_ref.shape[0]
    nope = N_HEADS * QK_NOPE_DIM
    pe = N_HEADS * QK_ROPE_DIM
    c = KV_LORA_RANK
    r = QK_ROPE_DIM
    cq = _rms_norm(_dot(x_ref[...].astype(BF16), wqa_ref[...]), qg_ref[...])
    q = _dot(cq.astype(BF16), wqb_ref[...])
    q_pe = (q[:, nope:nope + pe] * cos_ref[...] + q[:, nope + pe:nope + 2 * pe] * sin_ref[...]) * QK_SCALE
    qn = q[:, :nope].astype(BF16)
    pair = 2 * QK_NOPE_DIM
    for j in range(N_HEADS // 2):
        ql = _dot(qn[:, j * pair:(j + 1) * pair], wuk_ref[j]) * QK_SCALE
        for k in range(2):
            h = 2 * j + k
            q_ref[h * tm:(h + 1) * tm, :c] = ql[:, k * c:(k + 1) * c].astype(q_ref.dtype)
            q_ref[h * tm:(h + 1) * tm, c:] = q_pe[:, h * r:(h + 1) * r].astype(q_ref.dtype)


def _q_proj(x, wqa, qg, wqb, wuk, cos, sin, tm, dtype):
    t, d = x.shape
    nper = cos.shape[0] // tm
    pe = N_HEADS * QK_ROPE_DIM
    tab = pl.BlockSpec((tm, pe), lambda i: (i % nper, 0))
    return pl.pallas_call(
        _q_kernel,
        grid=(t // tm,),
        in_specs=[pl.BlockSpec((tm, d), lambda i: (i, 0)), _full(wqa.shape), _full(qg.shape), _full(wqb.shape),
                  _full(wuk.shape), tab, tab],
        out_specs=pl.BlockSpec((N_HEADS * tm, QK_WIDTH), lambda i: (i, 0)),
        out_shape=jax.ShapeDtypeStruct((N_HEADS * t, QK_WIDTH), dtype),
        compiler_params=_params("parallel"),
        name="q_proj",
    )(x, wqa, qg, wqb, wuk, cos, sin)


def _widen(stat, width):
    return jnp.concatenate([stat] * (width // LANES), axis=1)


def _softmax_step(s, kv, m_s, l_s, acc):
    m_prev = m_s[...]
    m_new = jnp.maximum(m_prev, jnp.max(s, -1, keepdims=True))
    corr = jnp.exp2(m_prev - m_new)
    p = jnp.exp2(s - _widen(m_new, s.shape[1]))
    l_s[...] = l_s[...] * corr + jnp.sum(p, -1, keepdims=True)
    acc[...] = acc[...] * _widen(corr, KV_LORA_RANK) + _dot(p.astype(BF16), kv)
    m_s[...] = m_new


def _attn_kernel(qi_ref, kj_ref, q_ref, kv_ref, vt_ref, o_ref, m_s, l_s, acc, *, tq, tk):
    p = pl.program_id(1)
    i = qi_ref[p]
    j = kj_ref[p]

    @pl.when(j == 0)
    def _():
        m_s[...] = jnp.full_like(m_s, -jnp.inf)
        l_s[...] = jnp.zeros_like(l_s)
        acc[...] = jnp.zeros_like(acc)

    def step(masked):
        st = _dot_nt(kv_ref[...], q_ref[...])
        if masked:
            k_pos = j * tk + lax.broadcasted_iota(jnp.int32, st.shape, 0)
            q_pos = i * tq + (lax.broadcasted_iota(jnp.int32, st.shape, 1) & (tq - 1))
            st = jnp.where(k_pos <= q_pos, st, -jnp.inf)
        m_prev = m_s[...]
        m_new = jnp.maximum(m_prev, jnp.max(st, 0, keepdims=True))
        corr = jnp.exp2(m_prev - m_new)
        pt = jnp.exp2(st - m_new)
        l_s[...] = l_s[...] * corr + jnp.sum(pt, 0, keepdims=True)
        acc[...] = acc[...] * corr + _dot(vt_ref[...], pt.astype(BF16))
        m_s[...] = m_new

    crosses_diagonal = j * tk + tk - 1 > i * tq

    @pl.when(jnp.logical_not(crosses_diagonal))
    def _():
        step(False)

    @pl.when(crosses_diagonal)
    def _():
        step(True)

    @pl.when(j == (i * tq + tq - 1) // tk)
    def _():
        o_ref[...] = (acc[...] * (1.0 / l_s[...])).T.astype(o_ref.dtype)


def _attn_prompt(q, kv, vt, nb, s, tq, tk):
    nq, nk = s // tq, s // tk
    assert tq & (tq - 1) == 0
    rows = N_HEADS * tq
    pairs = [(i, j) for i in range(nq) for j in range((i * tq + tq - 1) // tk + 1)]
    qi = jnp.asarray([i for i, _ in pairs], jnp.int32)
    kj = jnp.asarray([j for _, j in pairs], jnp.int32)
    q_row = lambda b, p, qi, kj: (b * nq + qi[p], 0)
    return pl.pallas_call(
        functools.partial(_attn_kernel, tq=tq, tk=tk),
        grid_spec=pltpu.PrefetchScalarGridSpec(
            num_scalar_prefetch=2,
            grid=(nb, len(pairs)),
            in_specs=[pl.BlockSpec((rows, QK_WIDTH), q_row),
                      pl.BlockSpec((tk, QK_WIDTH), lambda b, p, qi, kj: (b * nk + kj[p], 0)),
                      pl.BlockSpec((KV_LORA_RANK, tk), lambda b, p, qi, kj: (0, b * nk + kj[p]))],
            out_specs=pl.BlockSpec((rows, KV_LORA_RANK), q_row),
            scratch_shapes=[pltpu.VMEM((1, rows), F32), pltpu.VMEM((1, rows), F32),
                            pltpu.VMEM((KV_LORA_RANK, rows), F32)],
        ),
        out_shape=jax.ShapeDtypeStruct((q.shape[0], KV_LORA_RANK), BF16),
        compiler_params=_params("parallel", "arbitrary"),
        name="attn_prompt",
    )(qi, kj, q, kv, vt)


def _attn_paged_kernel(pt_ref, q_ref, cnew_ref, knew_ref, ckv_hbm, kpe_hbm, o_ref, m_s, l_s, acc, kbuf, kpbuf,
                       ck_in, kp_in, sem, *, seqs, pages, s_new):
    g = pl.program_id(0)
    c = pl.program_id(1)
    nc = pl.num_programs(1)
    n_pages = nc * pages
    step = g * nc + c
    rows = N_HEADS * s_new
    page = ck_in.shape[2]

    def page_copies(group, chunk, slot):
        copies = []
        for a in range(seqs):
            for k in range(pages):
                pid = pt_ref[(group * seqs + a) * n_pages + chunk * pages + k]
                copies.append(pltpu.make_async_copy(ckv_hbm.at[pid], ck_in.at[slot, a * pages + k], sem.at[slot, 0]))
                copies.append(pltpu.make_async_copy(kpe_hbm.at[pid], kp_in.at[slot, a * pages + k], sem.at[slot, 1]))
        return copies

    @pl.when(step == 0)
    def _():
        for cp in page_copies(0, 0, 0):
            cp.start()

    @pl.when(step + 1 < pl.num_programs(0) * nc)
    def _():
        nxt = step + 1
        for cp in page_copies(nxt // nc, nxt % nc, nxt % 2):
            cp.start()

    slot = step % 2
    pltpu.make_async_copy(ckv_hbm.at[pl.ds(0, seqs * pages)], ck_in.at[slot], sem.at[slot, 0]).wait()
    pltpu.make_async_copy(kpe_hbm.at[pl.ds(0, seqs * pages)], kp_in.at[slot], sem.at[slot, 1]).wait()
    ck_refs = [ck_in.at[slot, i] for i in range(seqs * pages)]
    kp_refs = [kp_in.at[slot, i] for i in range(seqs * pages)]

    qs = [q_ref[:, a].reshape(rows, QK_WIDTH).astype(BF16) for a in range(seqs)]
    qls = [q[:, :KV_LORA_RANK] for q in qs]
    qps = [q[:, KV_LORA_RANK:] for q in qs]

    @pl.when(c == 0)
    def _():
        for a in range(seqs):
            pad = 2 * SUBLANES - s_new
            cn = jnp.concatenate([cnew_ref[a], jnp.zeros((pad, KV_LORA_RANK), F32)], axis=0).astype(BF16)
            kn = jnp.concatenate([knew_ref[a], jnp.zeros((pad, QK_ROPE_DIM), F32)], axis=0).astype(BF16)
            s = _dot_nt(qls[a], cn) + _dot_nt(qps[a], kn)
            tok = lax.broadcasted_iota(jnp.int32, s.shape, 0) & (s_new - 1)
            key = lax.broadcasted_iota(jnp.int32, s.shape, 1)
            s = jnp.where(key <= tok, s, -jnp.inf)
            m = jnp.max(s, -1, keepdims=True)
            p = jnp.exp2(s - m)
            m_s[a] = jnp.broadcast_to(m, m_s.shape[1:])
            l_s[a] = jnp.broadcast_to(jnp.sum(p, -1, keepdims=True), l_s.shape[1:])
            acc[a] = _dot(p.astype(BF16), cn)

    for a in range(seqs):
        for k in range(pages):
            kbuf[a, k * page:(k + 1) * page, :] = ck_refs[a * pages + k][...].astype(BF16)
            kpbuf[a, :, k * page:(k + 1) * page] = kp_refs[a * pages + k][...].astype(BF16)
        kv = kbuf[a]
        s = _dot_nt(qls[a], kv) + _dot(qps[a], kpbuf[a])
        _softmax_step(s, kv, m_s.at[a], l_s.at[a], acc.at[a])

    @pl.when(c == pl.num_programs(1) - 1)
    def _():
        for a in range(seqs):
            o = acc[a] * _widen(1.0 / l_s[a], KV_LORA_RANK)
            o_ref[:, a] = o.reshape(N_HEADS, s_new, KV_LORA_RANK)


def _attn_paged(q, cnew, knew, cache_ckv, cache_kpe_t, page_table, pages, seqs):
    n_tiles, _, per_tile, s_new, _ = q.shape
    nb = n_tiles * per_tile
    assert s_new == SUBLANES and per_tile % seqs == 0
    groups = per_tile // seqs
    n_pages = page_table.shape[1]
    page = cache_ckv.shape[1]
    c = KV_LORA_RANK
    r = QK_ROPE_DIM
    per_g = lambda g, j, pt: (g, 0, 0)
    q_blk = lambda w: pl.BlockSpec((None, N_HEADS, seqs, s_new, w),
                                   lambda g, j, pt: (g // groups, 0, g % groups, 0, 0))

    in_specs = [q_blk(QK_WIDTH), pl.BlockSpec((seqs, s_new, c), per_g), pl.BlockSpec((seqs, s_new, r), per_g),
                pl.BlockSpec(memory_space=pl.ANY), pl.BlockSpec(memory_space=pl.ANY)]
    rows = N_HEADS * s_new
    assert n_pages % pages == 0
    return pl.pallas_call(
        functools.partial(_attn_paged_kernel, seqs=seqs, pages=pages, s_new=s_new),
        grid_spec=pltpu.PrefetchScalarGridSpec(
            num_scalar_prefetch=1,
            grid=(nb // seqs, n_pages // pages),
            in_specs=in_specs,
            out_specs=q_blk(c),
            scratch_shapes=[pltpu.VMEM((seqs, rows, LANES), F32), pltpu.VMEM((seqs, rows, LANES), F32),
                            pltpu.VMEM((seqs, rows, c), F32), pltpu.VMEM((seqs, pages * page, c), BF16),
                            pltpu.VMEM((seqs, r, pages * page), BF16),
                            pltpu.VMEM((2, seqs * pages, page, c), F32), pltpu.VMEM((2, seqs * pages, r, page), F32),
                            pltpu.SemaphoreType.DMA((2, 2))],
        ),
        out_shape=jax.ShapeDtypeStruct(q.shape[:-1] + (c,), F32),
        compiler_params=_params("arbitrary", "arbitrary"),
        name="attn_paged",
    )(page_table.reshape(-1), q, cnew, knew, cache_ckv, cache_kpe_t)


def _mla_out_kernel(o_ref, wuv_ref, wo_ref, res_ref, g_ref, b_ref, out_ref):
    tm = res_ref.shape[0]
    head = lambda h: o_ref[h * tm:(h + 1) * tm, :].astype(BF16)
    v = jnp.concatenate(
        [_dot(jnp.concatenate([head(2 * j), head(2 * j + 1)], axis=1), wuv_ref[j]) for j in range(N_HEADS // 2)],
        axis=1).astype(BF16)
    y = _dot(v, wo_ref[...])
    out_ref[...] = _layer_norm(DEEPNORM_ALPHA * res_ref[...] + y, g_ref[...], b_ref[...])


def _mla_out(o, wuv, wo, res, g, b, tm):
    t, n = res.shape
    return pl.pallas_call(
        _mla_out_kernel,
        grid=(t // tm,),
        in_specs=[pl.BlockSpec((N_HEADS * tm, o.shape[1]), lambda i: (i, 0)), _full(wuv.shape), _full(wo.shape),
                  pl.BlockSpec((tm, n), lambda i: (i, 0)), _full((1, n)), _full((1, n))],
        out_specs=pl.BlockSpec((tm, n), lambda i: (i, 0)),
        out_shape=jax.ShapeDtypeStruct((t, n), F32),
        compiler_params=_params("parallel"),
        name="mla_out",
    )(o, wuv, wo, res, g, b)


def _router_kernel(x_ref, w_ref, idx_ref, wt_ref):
    x = x_ref[...]
    w = w_ref[...]
    xh = x.astype(BF16)
    xl = (x - xh.astype(F32)).astype(BF16)
    wh = w.astype(BF16)
    wl = (w - wh.astype(F32)).astype(BF16)
    lg = (_dot_nt(wh, xh) + _dot_nt(wh, xl) + _dot_nt(wl, xh))[:N_EXPERTS]
    e = lax.broadcasted_iota(jnp.int32, lg.shape, 0).astype(F32)
    none = float(N_EXPERTS)
    v1 = jnp.max(lg, 0, keepdims=True)
    i1 = jnp.min(jnp.where(lg == v1, e, none), 0, keepdims=True)
    lg2 = jnp.where(e == i1, -jnp.inf, lg)
    v2 = jnp.max(lg2, 0, keepdims=True)
    i2 = jnp.min(jnp.where(lg2 == v2, e, none), 0, keepdims=True)
    ex = jnp.exp(v2 - v1)
    den = 1.0 + ex
    idx_ref[...] = jnp.concatenate([i1, i2], axis=0).astype(jnp.int32)
    wt_ref[...] = jnp.concatenate([1.0 / den, ex / den], axis=0)


def _router(x, w_t, tm):
    t, d = x.shape
    return pl.pallas_call(
        _router_kernel,
        grid=(t // tm,),
        in_specs=[pl.BlockSpec((tm, d), lambda i: (i, 0)), _full(w_t.shape)],
        out_specs=[pl.BlockSpec((2, tm), lambda i: (0, i))] * 2,
        out_shape=[jax.ShapeDtypeStruct((2, t), jnp.int32), jax.ShapeDtypeStruct((2, t), F32)],
        compiler_params=_params("parallel"),
        name="router",
    )(x, w_t)


def _dispatch_kernel(pos_ref, fill_ref, x_hbm, xs_hbm, zeros, sem, fill_sem, *, rows, tile):
    i = pl.program_id(0)
    t = pl.num_programs(0) * rows
    n_fill = fill_ref.shape[0] // 2
    fill = lambda k: pltpu.make_async_copy(
        zeros, xs_hbm.at[pl.ds(pl.multiple_of(fill_ref[2 * k + 1], tile), tile), :], fill_sem)

    @pl.when(i == 0)
    def _():
        zeros[...] = jnp.zeros_like(zeros)
        for k in range(n_fill):
            @pl.when(fill_ref[2 * k] == 1)
            def _(k=k):
                fill(k).start()
        for k in range(n_fill):
            @pl.when(fill_ref[2 * k] == 1)
            def _(k=k):
                fill(k).wait()

    base = i * rows

    def issue(r, carry):
        tok = base + r
        for k in range(2):
            slot = pos_ref[k * t + tok]
            pltpu.make_async_copy(x_hbm.at[pl.ds(tok, 1), :], xs_hbm.at[pl.ds(slot, 1), :], sem).start()
        return carry

    lax.fori_loop(0, rows, issue, 0, unroll=4)
    step_wait = pltpu.make_async_copy(x_hbm.at[pl.ds(0, 2 * rows), :], xs_hbm.at[pl.ds(0, 2 * rows), :], sem)

    @pl.when(i > 0)
    def _():
        step_wait.wait()

    @pl.when(i == pl.num_programs(0) - 1)
    def _():
        step_wait.wait()


def _dispatch(pos, fill, x, n_slots, rows, tile):
    t, d = x.shape
    return pl.pallas_call(
        functools.partial(_dispatch_kernel, rows=rows, tile=tile),
        grid_spec=pltpu.PrefetchScalarGridSpec(
            num_scalar_prefetch=2,
            grid=(t // rows,),
            in_specs=[pl.BlockSpec(memory_space=pl.ANY)],
            out_specs=pl.BlockSpec(memory_space=pl.ANY),
            scratch_shapes=[pltpu.VMEM((tile, d), x.dtype), pltpu.SemaphoreType.DMA, pltpu.SemaphoreType.DMA],
        ),
        out_shape=jax.ShapeDtypeStruct((n_slots, d), x.dtype),
        compiler_params=_params("arbitrary"),
        name="moe_dispatch",
    )(pos, fill, x)


def _combine_kernel(pos_ref, y_hbm, x_ref, w1_ref, w2_ref, g_ref, b_ref, o_ref, y1, y2, sem):
    rows = o_ref.shape[0]
    i = pl.program_id(0)
    n = pl.num_programs(0)
    t = n * rows

    def gather(step, slot):
        base = step * rows

        def issue(r, carry):
            p1 = pos_ref[base + r]
            p2 = pos_ref[t + base + r]
            pltpu.make_async_copy(y_hbm.at[pl.ds(p1, 1), :], y1.at[slot, pl.ds(r, 1), :], sem.at[slot, 0]).start()
            pltpu.make_async_copy(y_hbm.at[pl.ds(p2, 1), :], y2.at[slot, pl.ds(r, 1), :], sem.at[slot, 1]).start()
            return carry

        lax.fori_loop(0, rows, issue, 0, unroll=8)

    @pl.when(i == 0)
    def _():
        gather(0, 0)

    @pl.when(i + 1 < n)
    def _():
        gather(i + 1, (i + 1) % 2)

    slot = i % 2
    pltpu.make_async_copy(y_hbm.at[pl.ds(0, rows), :], y1.at[slot], sem.at[slot, 0]).wait()
    pltpu.make_async_copy(y_hbm.at[pl.ds(0, rows), :], y2.at[slot], sem.at[slot, 1]).wait()
    ff = w1_ref[...] * y1[slot] + w2_ref[...] * y2[slot]
    o_ref[...] = _layer_norm(DEEPNORM_ALPHA * x_ref[...] + ff, g_ref[...], b_ref[...])


def _combine(pos, y, x, w1, w2, g, b, rows):
    t, d = x.shape
    row = pl.BlockSpec((rows, d), lambda i, p: (i, 0))
    col = pl.BlockSpec((rows, 1), lambda i, p: (i, 0))
    vec = pl.BlockSpec((1, d), lambda i, p: (0, 0))
    return pl.pallas_call(
        _combine_kernel,
        grid_spec=pltpu.PrefetchScalarGridSpec(
            num_scalar_prefetch=1,
            grid=(t // rows,),
            in_specs=[pl.BlockSpec(memory_space=pl.ANY), row, col, col, vec, vec],
            out_specs=row,
            scratch_shapes=[pltpu.VMEM((2, rows, d), F32), pltpu.VMEM((2, rows, d), F32),
                            pltpu.SemaphoreType.DMA((2, 2))],
        ),
        out_shape=jax.ShapeDtypeStruct((t, d), F32),
        compiler_params=_params("arbitrary"),
        name="moe_combine",
    )(pos, y, x, w1, w2, g, b)


def _moe(x, w_router_t, wg, wu, wd, g, b, tm, tile, tf):
    t, d = x.shape
    idx, wts = _router(x, w_router_t, tm)
    e_flat = idx.reshape(-1)
    onehot = (e_flat[:, None] == jnp.arange(N_EXPERTS, dtype=jnp.int32)[None, :]).astype(jnp.int32)
    csum = jnp.cumsum(onehot, axis=0)
    rank = jnp.sum(csum * onehot, axis=1) - 1
    counts = csum[-1]
    padded = ((counts + tile - 1) // tile) * tile
    ends = jnp.cumsum(padded)
    starts = ends - padded
    pos = (starts[e_flat] + rank).astype(jnp.int32)
    n_slots = 2 * t + N_EXPERTS * tile
    tile_start = jnp.arange(n_slots // tile, dtype=jnp.int32) * tile
    tile_expert = jnp.minimum(
        jnp.sum((tile_start[:, None] >= ends[None, :]).astype(jnp.int32), axis=1), N_EXPERTS - 1)
    tail = ends[-1] + jnp.arange(N_EXPERTS, dtype=jnp.int32) * tile
    fill_start = jnp.concatenate([ends - tile, tail])
    fill_valid = jnp.concatenate([padded > counts, tail < n_slots])
    fill = jnp.stack([fill_valid.astype(jnp.int32), jnp.where(fill_valid, fill_start, 0).astype(jnp.int32)],
                     axis=1).reshape(-1)

    xs = _dispatch(pos, fill, x, n_slots, tm, tile)
    ys = _ffn(tile_expert, xs, wg, wu, wd, None, tile, tf)
    return _combine(pos, ys, x, wts[0][:, None], wts[1][:, None], g, b, tm)


def _swap_halves(w):
    half = w.shape[-1] // 2
    return jnp.concatenate([w[..., half:], w[..., :half]], axis=-1)


def _prep_weights(w):
    out = {}
    row = lambda v: v.reshape(1, -1).astype(F32)
    out["ln_mix"] = [(row(w["ln_mix_g"][i]), row(w["ln_mix_b"][i])) for i in range(DEPTH)]
    out["ln_ffn"] = [(row(w["ln_ffn_g"][i]), row(w["ln_ffn_b"][i])) for i in range(DEPTH)]
    out["rg_wg"] = w["rg_w_gate"][0].astype(BF16)
    out["rg_wx"] = w["rg_w_x"][0].astype(BF16)
    out["rg"] = {
        "conv_w": w["rg_conv_w"][0], "conv_b": row(w["rg_conv_b"][0]),
        "w_a": w["rg_w_a"][0].astype(BF16), "b_a": row(w["rg_b_a"][0]),
        "w_i": w["rg_w_i"][0].astype(BF16), "b_i": row(w["rg_b_i"][0]),
        "lam": row(w["rg_lambda"][0]),
    }
    out["rg_wout"] = w["rg_w_out"][0].astype(BF16)
    out["ffn"] = (w["ffn_w_gate"].astype(BF16), w["ffn_w_up"].astype(BF16), w["ffn_w_down"].astype(BF16))
    out["moe"] = (w["moe_w_gate"][0].astype(BF16), w["moe_w_up"][0].astype(BF16), w["moe_w_down"][0].astype(BF16))
    e = w["moe_w_router"].shape[-1]
    out["router_t"] = jnp.concatenate(
        [w["moe_w_router"][0].T, jnp.zeros((2 * SUBLANES - e, w["moe_w_router"].shape[1]), F32)], axis=0)
    out["ple_wg"] = w["ple_w_gate"].astype(BF16)
    out["ple_wp"] = w["ple_w_proj"].astype(BF16)
    kv = w["kv_w_a"]
    out["kv_wc"] = kv[:, :KV_LORA_RANK].astype(BF16)
    out["kv_wr"] = kv[:, KV_LORA_RANK:].astype(BF16)
    out["kv_wrs"] = _swap_halves(kv[:, KV_LORA_RANK:]).astype(BF16)
    out["kv_g"] = row(w["kv_norm_g"])
    out["q_wa"] = w["mla_w_q_a"][0].astype(BF16)
    out["q_g"] = row(w["mla_q_norm_g"][0])
    qb = w["mla_w_q_b"][0].reshape(-1, N_HEADS, QK_NOPE_DIM + QK_ROPE_DIM)
    lora = qb.shape[0]
    q_nope = qb[:, :, :QK_NOPE_DIM].reshape(lora, -1)
    q_rope = qb[:, :, QK_NOPE_DIM:]
    out["q_wb"] = jnp.concatenate(
        [q_nope, q_rope.reshape(lora, -1), _swap_halves(q_rope).reshape(lora, -1)], axis=1).astype(BF16)
    uk = jnp.transpose(w["kv_w_uk"], (1, 2, 0))
    zk = jnp.zeros_like(uk[0])
    out["wuk"] = jnp.stack([
        jnp.concatenate([jnp.concatenate([uk[2 * j], zk], axis=1), jnp.concatenate([zk, uk[2 * j + 1]], axis=1)],
                        axis=0) for j in range(N_HEADS // 2)]).astype(BF16)
    uv = jnp.transpose(w["kv_w_uv"], (1, 0, 2))
    zv = jnp.zeros_like(uv[0])
    out["wuv"] = jnp.stack([
        jnp.concatenate([jnp.concatenate([uv[2 * j], zv], axis=1), jnp.concatenate([zv, uv[2 * j + 1]], axis=1)],
                        axis=0) for j in range(N_HEADS // 2)]).astype(BF16)
    out["wo"] = w["mla_w_o"][0].astype(BF16)
    return out


def _rope_tables(pos, repeat):
    inv = ROPE_THETA ** (-jnp.arange(0, QK_ROPE_DIM, 2, dtype=F32) / QK_ROPE_DIM)
    ang = pos[:, None] * inv[None, :]
    cos, sin = jnp.cos(ang), jnp.sin(ang)
    cos_k = jnp.tile(jnp.concatenate([cos, cos], axis=-1), (repeat, 1))
    sin_k = jnp.tile(jnp.concatenate([-sin, sin], axis=-1), (repeat, 1))
    return cos_k, sin_k, jnp.tile(cos_k, (1, N_HEADS)), jnp.tile(sin_k, (1, N_HEADS))


def _tile_rows(t, want):
    tm = min(want, t)
    assert t % tm == 0
    return tm


def _trunk(x3, p4, conv0, rnn0, pos, pw, paged):
    nb, s, d = x3.shape
    t = nb * s
    x = x3.reshape(t, d)
    p = p4.reshape(DEPTH, t, -1)
    tm = _tile_rows(t, 512)

    gate, u = _rg_in(x, pw["rg_wg"], pw["rg_wx"], tm)
    if paged is None:
        conv_pad = jnp.concatenate(
            [jnp.zeros((nb, SUBLANES - (CONV_WIDTH - 1), D_RNN), F32), conv0], axis=1)
        hg, h_last = _rglru_seq(gate, u, conv_pad, rnn0[:, None, :], pw["rg"], nb, s, min(s, 256), 512)
        h_last = h_last[:, 0, :]
        tabs = _rope_tables(pos, 1)
    else:
        to_tm = lambda a: jnp.transpose(a.reshape(nb, s, -1), (1, 0, 2))
        hg, h_last = _rglru_step(to_tm(gate), to_tm(u), jnp.transpose(conv0, (1, 0, 2)), rnn0, pw["rg"], 512)
        hg = jnp.transpose(hg, (1, 0, 2)).reshape(t, -1)
        tabs = _rope_tables(pos, nb)
    conv_state = u.reshape(nb, s, -1)[:, s - (CONV_WIDTH - 1):, :]
    x = _mm_ln(hg, pw["rg_wout"], x, *pw["ln_mix"][0], tm)
    x = _ffn(jnp.zeros((t // tm,), jnp.int32), x, *pw["ffn"], pw["ln_ffn"][0], tm, pw["ffn"][0].shape[2] // 2)
    x = _ple(x, p[0], pw["ple_wg"][0], pw["ple_wp"][0], tm)
    cos_k, sin_k, cos_q, sin_q = tabs
    ckv, kpe, kv, vt = _kva(x, pw["kv_wc"], pw["kv_wr"], pw["kv_wrs"], pw["kv_g"], cos_k, sin_k, tm)

    tq = _tile_rows(t, 256)
    if paged is None:
        q = _q_proj(x, pw["q_wa"], pw["q_g"], pw["q_wb"], pw["wuk"], cos_q, sin_q, tq, BF16)
        o = _attn_prompt(q, kv, vt, nb, s, tq, 256)
    else:
        cache_ckv, cache_kpe, page_table = paged
        q = _q_proj(x, pw["q_wa"], pw["q_g"], pw["q_wb"], pw["wuk"], cos_q, sin_q, tq, F32)
        o = _attn_paged(q.reshape(t // tq, N_HEADS, tq // s, s, QK_WIDTH), ckv.reshape(nb, s, -1),
                        kpe.reshape(nb, s, -1), cache_ckv, jnp.swapaxes(cache_kpe, 1, 2), page_table, 16, 2)
        o = o.reshape(t * N_HEADS, KV_LORA_RANK)
    x = _mla_out(o, pw["wuv"], pw["wo"], x, *pw["ln_mix"][1], tq)
    x = _moe(x, pw["router_t"], *pw["moe"], *pw["ln_ffn"][1], tq, 512 if paged is None else 256,
             pw["moe"][0].shape[2] // 2)
    x = _ple(x, p[1], pw["ple_wg"][1], pw["ple_wp"][1], tm)
    return (x.reshape(nb, s, d), conv_state[None], h_last[None], ckv.reshape(nb, s, -1), kpe.reshape(nb, s, -1))


def kernel(x_prompt, x_sample, p_prompt, p_sample, state_conv, state_rnn, cache_ckv, cache_kpe, page_table, ln_mix_g, ln_mix_b, ln_ffn_g, ln_ffn_b, rg_w_gate, rg_w_x, rg_conv_w, rg_conv_b, rg_w_a, rg_b_a, rg_w_i, rg_b_i, rg_lambda, rg_w_out, mla_w_q_a, mla_q_norm_g, mla_w_q_b, mla_w_o, kv_w_a, kv_norm_g, kv_w_uk, kv_w_uv, ffn_w_gate, ffn_w_up, ffn_w_down, moe_w_router, moe_w_gate, moe_w_up, moe_w_down, ple_w_gate, ple_w_proj):
    w = dict(
        ln_mix_g=ln_mix_g, ln_mix_b=ln_mix_b, ln_ffn_g=ln_ffn_g, ln_ffn_b=ln_ffn_b,
        rg_w_gate=rg_w_gate, rg_w_x=rg_w_x, rg_conv_w=rg_conv_w, rg_conv_b=rg_conv_b, rg_w_a=rg_w_a,
        rg_b_a=rg_b_a, rg_w_i=rg_w_i, rg_b_i=rg_b_i, rg_lambda=rg_lambda, rg_w_out=rg_w_out,
        mla_w_q_a=mla_w_q_a, mla_q_norm_g=mla_q_norm_g, mla_w_q_b=mla_w_q_b, mla_w_o=mla_w_o,
        kv_w_a=kv_w_a, kv_norm_g=kv_norm_g, kv_w_uk=kv_w_uk, kv_w_uv=kv_w_uv,
        ffn_w_gate=ffn_w_gate, ffn_w_up=ffn_w_up, ffn_w_down=ffn_w_down,
        moe_w_router=moe_w_router, moe_w_gate=moe_w_gate, moe_w_up=moe_w_up, moe_w_down=moe_w_down,
        ple_w_gate=ple_w_gate, ple_w_proj=ple_w_proj)
    pw = _prep_weights(w)
    nb, s = x_prompt.shape[:2]
    dec_s = x_sample.shape[1]
    past_len = page_table.shape[1] * cache_ckv.shape[1]
    n_a = state_conv.shape[0]
    assert n_a == 1
    conv0_p = jnp.zeros((nb, CONV_WIDTH - 1, D_RNN), state_conv.dtype)
    rnn0_p = jnp.zeros((nb, D_RNN), state_rnn.dtype)
    y_p, conv_p, rnn_p, ckv_p, kpe_p = _trunk(
        x_prompt, p_prompt, conv0_p, rnn0_p, jnp.arange(s, dtype=F32), pw, None)
    y_s, conv_s, rnn_s, ckv_s, kpe_s = _trunk(
        x_sample, p_sample, state_conv[0], state_rnn[0], past_len + jnp.arange(dec_s, dtype=F32), pw,
        (cache_ckv, cache_kpe, page_table))
    return (y_p, y_s, conv_p, rnn_p, ckv_p, kpe_p, conv_s, rnn_s, ckv_s, kpe_s)
```

```python
import functools

import jax
import jax.numpy as jnp
from jax import lax
from jax.experimental import pallas as pl
from jax.experimental.pallas import tpu as pltpu

F32 = jnp.float32
BF16 = jnp.bfloat16

D_RNN = 1536
RNN_BLOCK = 128
CONV_WIDTH = 4
LRU_C = 8.0
N_HEADS = 16
KV_LORA_RANK = 256
QK_NOPE_DIM = 64
QK_ROPE_DIM = 32
V_HEAD_DIM = 64
ROPE_THETA = 10000.0
SOFTMAX_SCALE = (QK_NOPE_DIM + QK_ROPE_DIM) ** -0.5
N_EXPERTS = 8
LN_EPS = 1e-5
RMS_EPS = 1e-6
DEPTH = 2
DEEPNORM_ALPHA = (2.0 * DEPTH) ** 0.25

QK_WIDTH = KV_LORA_RANK + QK_ROPE_DIM
QK_SCALE = SOFTMAX_SCALE * 1.4426950408889634

SUBLANES = 8
LANES = 128
VMEM_LIMIT = 48 * 1024 * 1024


def _params(*sem):
    return pltpu.CompilerParams(dimension_semantics=sem, vmem_limit_bytes=VMEM_LIMIT)


def _dot(a, b):
    return jnp.dot(a, b, preferred_element_type=F32)


def _dot_nt(a, b):
    return lax.dot_general(a, b, (((1,), (1,)), ((), ())), preferred_element_type=F32)


def _layer_norm(z, g, b):
    mu = jnp.mean(z, -1, keepdims=True)
    d = z - mu
    var = jnp.mean(d * d, -1, keepdims=True)
    return d * lax.rsqrt(var + LN_EPS) * g + b


def _rms_norm(z, g):
    return z * lax.rsqrt(jnp.mean(z * z, -1, keepdims=True) + RMS_EPS) * g


def _full(shape):
    n = len(shape)
    return pl.BlockSpec(shape, lambda *_: (0,) * n)


def _rg_in_kernel(x_ref, wg_ref, wx_ref, gate_ref, u_ref):
    xb = x_ref[...].astype(BF16)
    gate_ref[...] = jax.nn.gelu(_dot(xb, wg_ref[...]))
    u_ref[...] = _dot(xb, wx_ref[...])


def _rg_in(x, wg, wx, tm):
    t, d = x.shape
    n = wg.shape[1]
    return pl.pallas_call(
        _rg_in_kernel,
        grid=(t // tm,),
        in_specs=[pl.BlockSpec((tm, d), lambda i: (i, 0)), _full((d, n)), _full((d, n))],
        out_specs=[pl.BlockSpec((tm, n), lambda i: (i, 0))] * 2,
        out_shape=[jax.ShapeDtypeStruct((t, n), F32)] * 2,
        compiler_params=_params("parallel"),
        name="rg_in",
    )(x, wg, wx)


def _rglru_gates(conv, wa_ref, ba, wi_ref, bi, lam):
    nblk = conv.shape[1] // RNN_BLOCK
    cb = conv.astype(BF16)
    ra = jnp.concatenate(
        [_dot(cb[:, n * RNN_BLOCK:(n + 1) * RNN_BLOCK], wa_ref[n]) for n in range(nblk)], axis=1)
    ia = jnp.concatenate(
        [_dot(cb[:, n * RNN_BLOCK:(n + 1) * RNN_BLOCK], wi_ref[n]) for n in range(nblk)], axis=1)
    r = jax.nn.sigmoid(ra + ba)
    i = jax.nn.sigmoid(ia + bi)
    z = -lam
    softplus = jnp.maximum(z, 0.0) + jnp.log1p(jnp.exp(-jnp.abs(z)))
    log_a = -LRU_C * r * softplus
    a = jnp.exp(log_a)
    v = 1.0 - a * a
    b = jnp.where(v > 0.0, v * lax.rsqrt(v), 0.0) * (i * conv)
    return a, b


def _rglru_seq_kernel(gate_ref, u_ref, conv0_ref, h0_ref, cw_ref, cb_ref, wa_ref, ba_ref, wi_ref, bi_ref,
                      lam_ref, hg_ref, hlast_ref, ubuf, hcar, abuf, bbuf):
    c = pl.program_id(2)
    tc, db = u_ref.shape
    halo = SUBLANES

    @pl.when(c == 0)
    def _():
        ubuf[0:halo, :] = conv0_ref[0]
        hcar[...] = jnp.broadcast_to(h0_ref[0], hcar.shape)

    ubuf[halo:halo + tc, :] = u_ref[...]
    cw = cw_ref[...]
    conv = cb_ref[...]
    for k in range(CONV_WIDTH):
        off = halo - (CONV_WIDTH - 1) + k
        conv = conv + ubuf[off:off + tc, :] * cw[k:k + 1, :]
    ubuf[0:halo, :] = ubuf[tc:tc + halo, :]

    a, b = _rglru_gates(conv, wa_ref, ba_ref[...], wi_ref, bi_ref[...], lam_ref[...])

    row = lax.broadcasted_iota(jnp.int32, (tc, db), 0) & (SUBLANES - 1)
    shift = 1
    while shift < SUBLANES:
        a_prev = pltpu.roll(a, shift, 0)
        b_prev = pltpu.roll(b, shift, 0)
        keep = row >= shift
        b = jnp.where(keep, a * b_prev + b, b)
        a = jnp.where(keep, a * a_prev, a)
        shift *= 2
    abuf[...] = a
    bbuf[...] = b

    def group(g, h):
        off = pl.multiple_of(g * SUBLANES, SUBLANES)
        hb = abuf[pl.ds(off, SUBLANES), :] * h + bbuf[pl.ds(off, SUBLANES), :]
        bbuf[pl.ds(off, SUBLANES), :] = hb
        return jnp.broadcast_to(hb[SUBLANES - 1:SUBLANES, :], hb.shape)

    h = lax.fori_loop(0, tc // SUBLANES, group, hcar[...])
    hcar[...] = h
    hg_ref[...] = (bbuf[...] * gate_ref[...]).astype(BF16)

    @pl.when(c == pl.num_programs(2) - 1)
    def _():
        hlast_ref[0] = h[0:1, :]


def _rglru_seq(gate, u, conv0, h0, rw, nb, s, tc, db):
    t, d = u.shape
    nc = s // tc
    kb = db // RNN_BLOCK
    row = lambda b, j, c: (b * nc + c, j)
    vec = pl.BlockSpec((1, db), lambda b, j, c: (0, j))
    blk = pl.BlockSpec((kb, RNN_BLOCK, RNN_BLOCK), lambda b, j, c: (j, 0, 0))
    return pl.pallas_call(
        _rglru_seq_kernel,
        grid=(nb, d // db, nc),
        in_specs=[
            pl.BlockSpec((tc, db), row), pl.BlockSpec((tc, db), row),
            pl.BlockSpec((1, SUBLANES, db), lambda b, j, c: (b, 0, j)),
            pl.BlockSpec((1, 1, db), lambda b, j, c: (b, 0, j)),
            pl.BlockSpec((CONV_WIDTH, db), lambda b, j, c: (0, j)), vec, blk, vec, blk, vec, vec,
        ],
        out_specs=[pl.BlockSpec((tc, db), row), pl.BlockSpec((1, 1, db), lambda b, j, c: (b, 0, j))],
        out_shape=[jax.ShapeDtypeStruct((t, d), BF16), jax.ShapeDtypeStruct((nb, 1, d), F32)],
        scratch_shapes=[pltpu.VMEM((tc + SUBLANES, db), F32), pltpu.VMEM((SUBLANES, db), F32),
                        pltpu.VMEM((tc, db), F32), pltpu.VMEM((tc, db), F32)],
        compiler_params=_params("parallel", "parallel", "arbitrary"),
        name="rglru_seq",
    )(gate, u, conv0, h0, rw["conv_w"], rw["conv_b"], rw["w_a"], rw["b_a"], rw["w_i"], rw["b_i"], rw["lam"])


def _rglru_step_kernel(gate_ref, u_ref, conv0_ref, h0_ref, cw_ref, cb_ref, wa_ref, ba_ref, wi_ref, bi_ref,
                       lam_ref, hg_ref, hlast_ref):
    s, nb, db = u_ref.shape
    ue = jnp.concatenate([conv0_ref[...], u_ref[...]], axis=0)
    cw = cw_ref[...]
    conv = cb_ref[...][None]
    for k in range(CONV_WIDTH):
        conv = conv + ue[k:k + s] * cw[k:k + 1, :][None]
    a, b = _rglru_gates(conv.reshape(s * nb, db), wa_ref, ba_ref[...], wi_ref, bi_ref[...], lam_ref[...])
    a = a.reshape(s, nb, db)
    b = b.reshape(s, nb, db)
    h = h0_ref[...]
    for t in range(s):
        h = a[t] * h + b[t]
        hg_ref[t] = (h * gate_ref[t]).astype(BF16)
    hlast_ref[...] = h


def _rglru_step(gate, u, conv0, h0, rw, db):
    s, nb, d = u.shape
    kb = db // RNN_BLOCK
    cube = pl.BlockSpec((s, nb, db), lambda j: (0, 0, j))
    vec = pl.BlockSpec((1, db), lambda j: (0, j))
    blk = pl.BlockSpec((kb, RNN_BLOCK, RNN_BLOCK), lambda j: (j, 0, 0))
    return pl.pallas_call(
        _rglru_step_kernel,
        grid=(d // db,),
        in_specs=[cube, cube, pl.BlockSpec((CONV_WIDTH - 1, nb, db), lambda j: (0, 0, j)),
                  pl.BlockSpec((nb, db), lambda j: (0, j)),
                  pl.BlockSpec((CONV_WIDTH, db), lambda j: (0, j)), vec, blk, vec, blk, vec, vec],
        out_specs=[cube, pl.BlockSpec((nb, db), lambda j: (0, j))],
        out_shape=[jax.ShapeDtypeStruct((s, nb, d), BF16), jax.ShapeDtypeStruct((nb, d), F32)],
        compiler_params=_params("parallel"),
        name="rglru_step",
    )(gate, u, conv0, h0, rw["conv_w"], rw["conv_b"], rw["w_a"], rw["b_a"], rw["w_i"], rw["b_i"], rw["lam"])


def _mm_ln_kernel(x_ref, w_ref, res_ref, g_ref, b_ref, o_ref):
    y = _dot(x_ref[...], w_ref[...])
    o_ref[...] = _layer_norm(DEEPNORM_ALPHA * res_ref[...] + y, g_ref[...], b_ref[...])


def _mm_ln(x, w, res, g, b, tm):
    t, k = x.shape
    n = w.shape[1]
    return pl.pallas_call(
        _mm_ln_kernel,
        grid=(t // tm,),
        in_specs=[pl.BlockSpec((tm, k), lambda i: (i, 0)), _full((k, n)),
                  pl.BlockSpec((tm, n), lambda i: (i, 0)), _full((1, n)), _full((1, n))],
        out_specs=pl.BlockSpec((tm, n), lambda i: (i, 0)),
        out_shape=jax.ShapeDtypeStruct((t, n), F32),
        compiler_params=_params("parallel"),
        name="mm_ln",
    )(x, w, res, g, b)


def _ffn_kernel(te_ref, x_ref, wg_ref, wu_ref, wd_ref, *rest, post_norm):
    del te_ref
    if post_norm:
        g_ref, b_ref, o_ref, xb_ref, acc_ref = rest
    else:
        o_ref, xb_ref, acc_ref = rest
    c = pl.program_id(1)

    @pl.when(c == 0)
    def _():
        xb_ref[...] = x_ref[...].astype(BF16)
        acc_ref[...] = jnp.zeros_like(acc_ref)

    xb = xb_ref[...]
    h = (jax.nn.silu(_dot(xb, wg_ref[0])) * _dot(xb, wu_ref[0])).astype(BF16)
    acc_ref[...] += _dot(h, wd_ref[0])

    @pl.when(c == pl.num_programs(1) - 1)
    def _():
        if post_norm:
            o_ref[...] = _layer_norm(DEEPNORM_ALPHA * x_ref[...] + acc_ref[...], g_ref[...], b_ref[...])
        else:
            o_ref[...] = acc_ref[...]


def _ffn(tile_expert, x, wg, wu, wd, norm, tm, tf):
    r, d = x.shape
    f = wg.shape[2]
    in_specs = [
        pl.BlockSpec((tm, d), lambda i, c, te: (i, 0)),
        pl.BlockSpec((1, d, tf), lambda i, c, te: (te[i], 0, c)),
        pl.BlockSpec((1, d, tf), lambda i, c, te: (te[i], 0, c)),
        pl.BlockSpec((1, tf, d), lambda i, c, te: (te[i], c, 0)),
    ]
    args = [x, wg, wu, wd]
    if norm is not None:
        in_specs += [pl.BlockSpec((1, d), lambda i, c, te: (0, 0))] * 2
        args += list(norm)
    return pl.pallas_call(
        functools.partial(_ffn_kernel, post_norm=norm is not None),
        grid_spec=pltpu.PrefetchScalarGridSpec(
            num_scalar_prefetch=1,
            grid=(r // tm, f // tf),
            in_specs=in_specs,
            out_specs=pl.BlockSpec((tm, d), lambda i, c, te: (i, 0)),
            scratch_shapes=[pltpu.VMEM((tm, d), BF16), pltpu.VMEM((tm, d), F32)],
        ),
        out_shape=jax.ShapeDtypeStruct((r, d), F32),
        compiler_params=_params("parallel", "arbitrary"),
        name="ffn",
    )(tile_expert, *args)


def _ple_kernel(x_ref, p_ref, wg_ref, wp_ref, o_ref):
    x = x_ref[...]
    a = _dot(x.astype(BF16), wg_ref[...])
    c = _dot(p_ref[...].astype(BF16), wp_ref[...])
    o_ref[...] = x + jax.nn.sigmoid(a) * c


def _ple(x, p, wg, wp, tm):
    t, d = x.shape
    dp = p.shape[1]
    return pl.pallas_call(
        _ple_kernel,
        grid=(t // tm,),
        in_specs=[pl.BlockSpec((tm, d), lambda i: (i, 0)), pl.BlockSpec((tm, dp), lambda i: (i, 0)),
                  _full((d, d)), _full((dp, d))],
        out_specs=pl.BlockSpec((tm, d), lambda i: (i, 0)),
        out_shape=jax.ShapeDtypeStruct((t, d), F32),
        compiler_params=_params("parallel"),
        name="ple",
    )(x, p, wg, wp)


def _kva_kernel(x_ref, wc_ref, wr_ref, wrs_ref, g_ref, cos_ref, sin_ref, *rest, attn_operands):
    xb = x_ref[...].astype(BF16)
    ckv = _rms_norm(_dot(xb, wc_ref[...]), g_ref[...])
    kpe = _dot(xb, wr_ref[...]) * cos_ref[...] + _dot(xb, wrs_ref[...]) * sin_ref[...]
    if not attn_operands:
        ckv_ref, kpe_ref = rest
        ckv_ref[...] = ckv
        kpe_ref[...] = kpe
        return
    ckv_ref, kpe_ref, kv_ref, vt_ref = rest
    ckv_ref[...] = ckv
    kpe_ref[...] = kpe
    kv_ref[:, :KV_LORA_RANK] = ckv.astype(BF16)
    kv_ref[:, KV_LORA_RANK:] = kpe.astype(BF16)
    vt_ref[...] = ckv.T.astype(BF16)


def _kva(x, wc, wr, wrs, g, cos, sin, tm, attn_operands=False):
    t, d = x.shape
    nper = cos.shape[0] // tm
    tab = pl.BlockSpec((tm, QK_ROPE_DIM), lambda i: (i % nper, 0))
    out = lambda n: pl.BlockSpec((tm, n), lambda i: (i, 0))
    out_specs = [out(KV_LORA_RANK), out(QK_ROPE_DIM)]
    out_shape = [jax.ShapeDtypeStruct((t, KV_LORA_RANK), F32), jax.ShapeDtypeStruct((t, QK_ROPE_DIM), F32)]
    if attn_operands:
        out_specs += [out(QK_WIDTH), pl.BlockSpec((KV_LORA_RANK, tm), lambda i: (0, i))]
        out_shape += [jax.ShapeDtypeStruct((t, QK_WIDTH), BF16), jax.ShapeDtypeStruct((KV_LORA_RANK, t), BF16)]
    return pl.pallas_call(
        functools.partial(_kva_kernel, attn_operands=attn_operands),
        grid=(t // tm,),
        in_specs=[pl.BlockSpec((tm, d), lambda i: (i, 0)), _full(wc.shape), _full(wr.shape), _full(wrs.shape),
                  _full(g.shape), tab, tab],
        out_specs=out_specs,
        out_shape=out_shape,
        compiler_params=_params("parallel"),
        name="kv_latent",
    )(x, wc, wr, wrs, g, cos, sin)


def _q_kernel(x_ref, wqa_ref, qg_ref, wqb_ref, wuk_ref, cos_ref, sin_ref, q_ref):
    tm = x_ref.shape[0]
    nope = N_HEADS * QK_NOPE_DIM
    pe = N_HEADS * QK_ROPE_DIM
    c = KV_LORA_RANK
    r = QK_ROPE_DIM
    cq = _rms_norm(_dot(x_ref[...].astype(BF16), wqa_ref[...]), qg_ref[...])
    q = _dot(cq.astype(BF16), wqb_ref[...])
    q_pe = (q[:, nope:nope + pe] * cos_ref[...] + q[:, nope + pe:nope + 2 * pe] * sin_ref[...]) * QK_SCALE
    qn = q[:, :nope].astype(BF16)
    pair = 2 * QK_NOPE_DIM
    for j in range(N_HEADS // 2):
        ql = _dot(qn[:, j * pair:(j + 1) * pair], wuk_ref[j]) * QK_SCALE
        for k in range(2):
            h = 2 * j + k
            q_ref[h * tm:(h + 1) * tm, :c] = ql[:, k * c:(k + 1) * c].astype(q_ref.dtype)
            q_ref[h * tm:(h + 1) * tm, c:] = q_pe[:, h * r:(h + 1) * r].astype(q_ref.dtype)


def _q_proj(x, wqa, qg, wqb, wuk, cos, sin, tm, dtype):
    t, d = x.shape
    nper = cos.shape[0] // tm
    pe = N_HEADS * QK_ROPE_DIM
    tab = pl.BlockSpec((tm, pe), lambda i: (i % nper, 0))
    return pl.pallas_call(
        _q_kernel,
        grid=(t // tm,),
        in_specs=[pl.BlockSpec((tm, d), lambda i: (i, 0)), _full(wqa.shape), _full(qg.shape), _full(wqb.shape),
                  _full(wuk.shape), tab, tab],
        out_specs=pl.BlockSpec((N_HEADS * tm, QK_WIDTH), lambda i: (i, 0)),
        out_shape=jax.ShapeDtypeStruct((N_HEADS * t, QK_WIDTH), dtype),
        compiler_params=_params("parallel"),
        name="q_proj",
    )(x, wqa, qg, wqb, wuk, cos, sin)


def _widen(stat, width):
    return jnp.concatenate([stat] * (width // LANES), axis=1)


def _softmax_step(s, kv, m_s, l_s, acc):
    m_prev = m_s[...]
    m_new = jnp.maximum(m_prev, jnp.max(s, -1, keepdims=True))
    corr = jnp.exp2(m_prev - m_new)
    p = jnp.exp2(s - _widen(m_new, s.shape[1]))
    l_s[...] = l_s[...] * corr + jnp.sum(p, -1, keepdims=True)
    acc[...] = acc[...] * _widen(corr, KV_LORA_RANK) + _dot(p.astype(BF16), kv)
    m_s[...] = m_new


def _attn_kernel(qi_ref, kj_ref, q_ref, kv_ref, vt_ref, o_ref, m_s, l_s, acc, *, tq, tk):
    p = pl.program_id(1)
    i = qi_ref[p]
    j = kj_ref[p]

    @pl.when(j == 0)
    def _():
        m_s[...] = jnp.full_like(m_s, -jnp.inf)
        l_s[...] = jnp.zeros_like(l_s)
        acc[...] = jnp.zeros_like(acc)

    def step(masked):
        st = _dot_nt(kv_ref[...], q_ref[...])
        if masked:
            k_pos = j * tk + lax.broadcasted_iota(jnp.int32, st.shape, 0)
            q_pos = i * tq + (lax.broadcasted_iota(jnp.int32, st.shape, 1) & (tq - 1))
            st = jnp.where(k_pos <= q_pos, st, -jnp.inf)
        m_prev = m_s[...]
        m_new = jnp.maximum(m_prev, jnp.max(st, 0, keepdims=True))
        corr = jnp.exp2(m_prev - m_new)
        pt = jnp.exp2(st - m_new)
        l_s[...] = l_s[...] * corr + jnp.sum(pt, 0, keepdims=True)
        acc[...] = acc[...] * corr + _dot(vt_ref[...], pt.astype(BF16))
        m_s[...] = m_new

    crosses_diagonal = j * tk + tk - 1 > i * tq

    @pl.when(jnp.logical_not(crosses_diagonal))
    def _():
        step(False)

    @pl.when(crosses_diagonal)
    def _():
        step(True)

    @pl.when(j == (i * tq + tq - 1) // tk)
    def _():
        o_ref[...] = (acc[...] * (1.0 / l_s[...])).T.astype(o_ref.dtype)


def _attn_prompt(q, kv, vt, nb, s, tq, tk):
    nq, nk = s // tq, s // tk
    assert tq & (tq - 1) == 0
    rows = N_HEADS * tq
    pairs = [(i, j) for i in range(nq) for j in range((i * tq + tq - 1) // tk + 1)]
    qi = jnp.asarray([i for i, _ in pairs], jnp.int32)
    kj = jnp.asarray([j for _, j in pairs], jnp.int32)
    q_row = lambda b, p, qi, kj: (b * nq + qi[p], 0)
    return pl.pallas_call(
        functools.partial(_attn_kernel, tq=tq, tk=tk),
        grid_spec=pltpu.PrefetchScalarGridSpec(
            num_scalar_prefetch=2,
            grid=(nb, len(pairs)),
            in_specs=[pl.BlockSpec((rows, QK_WIDTH), q_row),
                      pl.BlockSpec((tk, QK_WIDTH), lambda b, p, qi, kj: (b * nk + kj[p], 0)),
                      pl.BlockSpec((KV_LORA_RANK, tk), lambda b, p, qi, kj: (0, b * nk + kj[p]))],
            out_specs=pl.BlockSpec((rows, KV_LORA_RANK), q_row),
            scratch_shapes=[pltpu.VMEM((1, rows), F32), pltpu.VMEM((1, rows), F32),
                            pltpu.VMEM((KV_LORA_RANK, rows), F32)],
        ),
        out_shape=jax.ShapeDtypeStruct((q.shape[0], KV_LORA_RANK), BF16),
        compiler_params=_params("parallel", "arbitrary"),
        name="attn_prompt",
    )(qi, kj, q, kv, vt)


def _attn_paged_kernel(pt_ref, q_ref, cnew_ref, knew_ref, ckv_hbm, kpe_hbm, o_ref, m_s, l_s, acc, kbuf, kpbuf,
                       ck_in, kp_in, sem, *, seqs, pages, s_new):
    g = pl.program_id(0)
    c = pl.program_id(1)
    nc = pl.num_programs(1)
    n_pages = nc * pages
    step = g * nc + c
    rows = N_HEADS * s_new
    page = ck_in.shape[2]

    def page_copies(group, chunk, slot):
        copies = []
        for a in range(seqs):
            for k in range(pages):
                pid = pt_ref[(group * seqs + a) * n_pages + chunk * pages + k]
                copies.append(pltpu.make_async_copy(ckv_hbm.at[pid], ck_in.at[slot, a * pages + k], sem.at[slot, 0]))
                copies.append(pltpu.make_async_copy(kpe_hbm.at[pid], kp_in.at[slot, a * pages + k], sem.at[slot, 1]))
        return copies

    @pl.when(step == 0)
    def _():
        for cp in page_copies(0, 0, 0):
            cp.start()

    @pl.when(step + 1 < pl.num_programs(0) * nc)
    def _():
        nxt = step + 1
        for cp in page_copies(nxt // nc, nxt % nc, nxt % 2):
            cp.start()

    slot = step % 2
    pltpu.make_async_copy(ckv_hbm.at[pl.ds(0, seqs * pages)], ck_in.at[slot], sem.at[slot, 0]).wait()
    pltpu.make_async_copy(kpe_hbm.at[pl.ds(0, seqs * pages)], kp_in.at[slot], sem.at[slot, 1]).wait()
    ck_refs = [ck_in.at[slot, i] for i in range(seqs * pages)]
    kp_refs = [kp_in.at[slot, i] for i in range(seqs * pages)]

    qs = [q_ref[:, a].reshape(rows, QK_WIDTH).astype(BF16) for a in range(seqs)]
    qls = [q[:, :KV_LORA_RANK] for q in qs]
    qps = [q[:, KV_LORA_RANK:] for q in qs]

    @pl.when(c == 0)
    def _():
        for a in range(seqs):
            pad = 2 * SUBLANES - s_new
            cn = jnp.concatenate([cnew_ref[a], jnp.zeros((pad, KV_LORA_RANK), F32)], axis=0).astype(BF16)
            kn = jnp.concatenate([knew_ref[a], jnp.zeros((pad, QK_ROPE_DIM), F32)], axis=0).astype(BF16)
            s = _dot_nt(qls[a], cn) + _dot_nt(qps[a], kn)
            tok = lax.broadcasted_iota(jnp.int32, s.shape, 0) & (s_new - 1)
            key = lax.broadcasted_iota(jnp.int32, s.shape, 1)
            s = jnp.where(key <= tok, s, -jnp.inf)
            m = jnp.max(s, -1, keepdims=True)
            p = jnp.exp2(s - m)
            m_s[a] = jnp.broadcast_to(m, m_s.shape[1:])
            l_s[a] = jnp.broadcast_to(jnp.sum(p, -1, keepdims=True), l_s.shape[1:])
            acc[a] = _dot(p.astype(BF16), cn)

    for a in range(seqs):
        for k in range(pages):
            kbuf[a, k * page:(k + 1) * page, :] = ck_refs[a * pages + k][...].astype(BF16)
            kpbuf[a, :, k * page:(k + 1) * page] = kp_refs[a * pages + k][...].astype(BF16)
        kv = kbuf[a]
        s = _dot_nt(qls[a], kv) + _dot(qps[a], kpbuf[a])
        _softmax_step(s, kv, m_s.at[a], l_s.at[a], acc.at[a])

    @pl.when(c == pl.num_programs(1) - 1)
    def _():
        for a in range(seqs):
            o = acc[a] * _widen(1.0 / l_s[a], KV_LORA_RANK)
            o_ref[:, a] = o.reshape(N_HEADS, s_new, KV_LORA_RANK)


def _attn_paged(q, cnew, knew, cache_ckv, cache_kpe_t, page_table, pages, seqs):
    n_tiles, _, per_tile, s_new, _ = q.shape
    nb = n_tiles * per_tile
    assert s_new == SUBLANES and per_tile % seqs == 0
    groups = per_tile // seqs
    n_pages = page_table.shape[1]
    page = cache_ckv.shape[1]
    c = KV_LORA_RANK
    r = QK_ROPE_DIM
    per_g = lambda g, j, pt: (g, 0, 0)
    q_blk = lambda w: pl.BlockSpec((None, N_HEADS, seqs, s_new, w),
                                   lambda g, j, pt: (g // groups, 0, g % groups, 0, 0))

    in_specs = [q_blk(QK_WIDTH), pl.BlockSpec((seqs, s_new, c), per_g), pl.BlockSpec((seqs, s_new, r), per_g),
                pl.BlockSpec(memory_space=pl.ANY), pl.BlockSpec(memory_space=pl.ANY)]
    rows = N_HEADS * s_new
    assert n_pages % pages == 0
    return pl.pallas_call(
        functools.partial(_attn_paged_kernel, seqs=seqs, pages=pages, s_new=s_new),
        grid_spec=pltpu.PrefetchScalarGridSpec(
            num_scalar_prefetch=1,
            grid=(nb // seqs, n_pages // pages),
            in_specs=in_specs,
            out_specs=q_blk(c),
            scratch_shapes=[pltpu.VMEM((seqs, rows, LANES), F32), pltpu.VMEM((seqs, rows, LANES), F32),
                            pltpu.VMEM((seqs, rows, c), F32), pltpu.VMEM((seqs, pages * page, c), BF16),
                            pltpu.VMEM((seqs, r, pages * page), BF16),
                            pltpu.VMEM((2, seqs * pages, page, c), F32), pltpu.VMEM((2, seqs * pages, r, page), F32),
                            pltpu.SemaphoreType.DMA((2, 2))],
        ),
        out_shape=jax.ShapeDtypeStruct(q.shape[:-1] + (c,), F32),
        compiler_params=_params("arbitrary", "arbitrary"),
        name="attn_paged",
    )(page_table.reshape(-1), q, cnew, knew, cache_ckv, cache_kpe_t)


def _mla_out_kernel(o_ref, wuv_ref, wo_ref, res_ref, g_ref, b_ref, out_ref):
    tm = res_ref.shape[0]
    head = lambda h: o_ref[h * tm:(h + 1) * tm, :].astype(BF16)
    v = jnp.concatenate(
        [_dot(jnp.concatenate([head(2 * j), head(2 * j + 1)], axis=1), wuv_ref[j]) for j in range(N_HEADS // 2)],
        axis=1).astype(BF16)
    y = _dot(v, wo_ref[...])
    out_ref[...] = _layer_norm(DEEPNORM_ALPHA * res_ref[...] + y, g_ref[...], b_ref[...])


def _mla_out(o, wuv, wo, res, g, b, tm):
    t, n = res.shape
    return pl.pallas_call(
        _mla_out_kernel,
        grid=(t // tm,),
        in_specs=[pl.BlockSpec((N_HEADS * tm, o.shape[1]), lambda i: (i, 0)), _full(wuv.shape), _full(wo.shape),
                  pl.BlockSpec((tm, n), lambda i: (i, 0)), _full((1, n)), _full((1, n))],
        out_specs=pl.BlockSpec((tm, n), lambda i: (i, 0)),
        out_shape=jax.ShapeDtypeStruct((t, n), F32),
        compiler_params=_params("parallel"),
        name="mla_out",
    )(o, wuv, wo, res, g, b)


def _router_kernel(x_ref, w_ref, idx_ref, wt_ref):
    x = x_ref[...]
    w = w_ref[...]
    xh = x.astype(BF16)
    xl = (x - xh.astype(F32)).astype(BF16)
    wh = w.astype(BF16)
    wl = (w - wh.astype(F32)).astype(BF16)
    lg = (_dot_nt(wh, xh) + _dot_nt(wh, xl) + _dot_nt(wl, xh))[:N_EXPERTS]
    e = lax.broadcasted_iota(jnp.int32, lg.shape, 0).astype(F32)
    none = float(N_EXPERTS)
    v1 = jnp.max(lg, 0, keepdims=True)
    i1 = jnp.min(jnp.where(lg == v1, e, none), 0, keepdims=True)
    lg2 = jnp.where(e == i1, -jnp.inf, lg)
    v2 = jnp.max(lg2, 0, keepdims=True)
    i2 = jnp.min(jnp.where(lg2 == v2, e, none), 0, keepdims=True)
    ex = jnp.exp(v2 - v1)
    den = 1.0 + ex
    idx_ref[...] = jnp.concatenate([i1, i2], axis=0).astype(jnp.int32)
    wt_ref[...] = jnp.concatenate([1.0 / den, ex / den], axis=0)


def _router(x, w_t, tm):
    t, d = x.shape
    return pl.pallas_call(
        _router_kernel,
        grid=(t // tm,),
        in_specs=[pl.BlockSpec((tm, d), lambda i: (i, 0)), _full(w_t.shape)],
        out_specs=[pl.BlockSpec((2, tm), lambda i: (0, i))] * 2,
        out_shape=[jax.ShapeDtypeStruct((2, t), jnp.int32), jax.ShapeDtypeStruct((2, t), F32)],
        compiler_params=_params("parallel"),
        name="router",
    )(x, w_t)


def _dispatch_kernel(pos_ref, fill_ref, x_ref, xs_hbm, zeros, sem, fill_sem, *, rows, tile):
    i = pl.program_id(0)
    t = pl.num_programs(0) * rows
    n_fill = fill_ref.shape[0] // 2
    fill = lambda k: pltpu.make_async_copy(
        zeros, xs_hbm.at[pl.ds(pl.multiple_of(fill_ref[2 * k + 1], tile), tile), :], fill_sem)

    @pl.when(i == 0)
    def _():
        zeros[...] = jnp.zeros_like(zeros)
        for k in range(n_fill):
            @pl.when(fill_ref[2 * k] == 1)
            def _(k=k):
                fill(k).start()
        for k in range(n_fill):
            @pl.when(fill_ref[2 * k] == 1)
            def _(k=k):
                fill(k).wait()

    base = i * rows

    def issue(r, carry):
        for k in range(2):
            slot = pos_ref[k * t + base + r]
            pltpu.make_async_copy(x_ref.at[pl.ds(r, 1), :], xs_hbm.at[pl.ds(slot, 1), :], sem.at[k]).start()
        return carry

    lax.fori_loop(0, rows, issue, 0, unroll=4)
    for k in range(2):
        pltpu.make_async_copy(x_ref, xs_hbm.at[pl.ds(0, rows), :], sem.at[k]).wait()


def _dispatch(pos, fill, x, n_slots, rows, tile):
    t, d = x.shape
    return pl.pallas_call(
        functools.partial(_dispatch_kernel, rows=rows, tile=tile),
        grid_spec=pltpu.PrefetchScalarGridSpec(
            num_scalar_prefetch=2,
            grid=(t // rows,),
            in_specs=[pl.BlockSpec((rows, d), lambda i, pos, fill: (i, 0))],
            out_specs=pl.BlockSpec(memory_space=pl.ANY),
            scratch_shapes=[pltpu.VMEM((tile, d), x.dtype), pltpu.SemaphoreType.DMA((2,)),
                            pltpu.SemaphoreType.DMA],
        ),
        out_shape=jax.ShapeDtypeStruct((n_slots, d), x.dtype),
        compiler_params=_params("arbitrary"),
        name="moe_dispatch",
    )(pos, fill, x)


def _combine_kernel(pos_ref, y_hbm, x_ref, w1_ref, w2_ref, g_ref, b_ref, o_ref, y1, y2, sem):
    rows = o_ref.shape[0]
    i = pl.program_id(0)
    n = pl.num_programs(0)
    t = n * rows

    def gather(step, slot):
        base = step * rows

        def issue(r, carry):
            p1 = pos_ref[base + r]
            p2 = pos_ref[t + base + r]
            pltpu.make_async_copy(y_hbm.at[pl.ds(p1, 1), :], y1.at[slot, pl.ds(r, 1), :], sem.at[slot, 0]).start()
            pltpu.make_async_copy(y_hbm.at[pl.ds(p2, 1), :], y2.at[slot, pl.ds(r, 1), :], sem.at[slot, 1]).start()
            return carry

        lax.fori_loop(0, rows, issue, 0, unroll=8)

    @pl.when(i == 0)
    def _():
        gather(0, 0)

    @pl.when(i + 1 < n)
    def _():
        gather(i + 1, (i + 1) % 2)

    slot = i % 2
    pltpu.make_async_copy(y_hbm.at[pl.ds(0, rows), :], y1.at[slot], sem.at[slot, 0]).wait()
    pltpu.make_async_copy(y_hbm.at[pl.ds(0, rows), :], y2.at[slot], sem.at[slot, 1]).wait()
    ff = w1_ref[...] * y1[slot] + w2_ref[...] * y2[slot]
    o_ref[...] = _layer_norm(DEEPNORM_ALPHA * x_ref[...] + ff, g_ref[...], b_ref[...])


def _combine(pos, y, x, w1, w2, g, b, rows):
    t, d = x.shape
    row = pl.BlockSpec((rows, d), lambda i, p: (i, 0))
    col = pl.BlockSpec((rows, 1), lambda i, p: (i, 0))
    vec = pl.BlockSpec((1, d), lambda i, p: (0, 0))
    return pl.pallas_call(
        _combine_kernel,
        grid_spec=pltpu.PrefetchScalarGridSpec(
            num_scalar_prefetch=1,
            grid=(t // rows,),
            in_specs=[pl.BlockSpec(memory_space=pl.ANY), row, col, col, vec, vec],
            out_specs=row,
            scratch_shapes=[pltpu.VMEM((2, rows, d), F32), pltpu.VMEM((2, rows, d), F32),
                            pltpu.SemaphoreType.DMA((2, 2))],
        ),
        out_shape=jax.ShapeDtypeStruct((t, d), F32),
        compiler_params=_params("arbitrary"),
        name="moe_combine",
    )(pos, y, x, w1, w2, g, b)


def _moe(x, w_router_t, wg, wu, wd, g, b, tm, tile, tf):
    t, d = x.shape
    idx, wts = _router(x, w_router_t, tm)
    e_flat = idx.reshape(-1)
    onehot = (e_flat[:, None] == jnp.arange(N_EXPERTS, dtype=jnp.int32)[None, :]).astype(jnp.int32)
    csum = jnp.cumsum(onehot, axis=0)
    rank = jnp.sum(csum * onehot, axis=1) - 1
    counts = csum[-1]
    padded = ((counts + tile - 1) // tile) * tile
    ends = jnp.cumsum(padded)
    starts = ends - padded
    pos = (starts[e_flat] + rank).astype(jnp.int32)
    n_slots = 2 * t + N_EXPERTS * tile
    tile_start = jnp.arange(n_slots // tile, dtype=jnp.int32) * tile
    tile_expert = jnp.minimum(
        jnp.sum((tile_start[:, None] >= ends[None, :]).astype(jnp.int32), axis=1), N_EXPERTS - 1)
    tail = ends[-1] + jnp.arange(N_EXPERTS, dtype=jnp.int32) * tile
    fill_start = jnp.concatenate([ends - tile, tail])
    fill_valid = jnp.concatenate([padded > counts, tail < n_slots])
    fill = jnp.stack([fill_valid.astype(jnp.int32), jnp.where(fill_valid, fill_start, 0).astype(jnp.int32)],
                     axis=1).reshape(-1)

    xs = _dispatch(pos, fill, x, n_slots, tm, tile)
    ys = _ffn(tile_expert, xs, wg, wu, wd, None, tile, tf)
    return _combine(pos, ys, x, wts[0][:, None], wts[1][:, None], g, b, tm)


def _swap_halves(w):
    half = w.shape[-1] // 2
    return jnp.concatenate([w[..., half:], w[..., :half]], axis=-1)


def _prep_weights(w):
    out = {}
    row = lambda v: v.reshape(1, -1).astype(F32)
    out["ln_mix"] = [(row(w["ln_mix_g"][i]), row(w["ln_mix_b"][i])) for i in range(DEPTH)]
    out["ln_ffn"] = [(row(w["ln_ffn_g"][i]), row(w["ln_ffn_b"][i])) for i in range(DEPTH)]
    out["rg_wg"] = w["rg_w_gate"][0].astype(BF16)
    out["rg_wx"] = w["rg_w_x"][0].astype(BF16)
    out["rg"] = {
        "conv_w": w["rg_conv_w"][0], "conv_b": row(w["rg_conv_b"][0]),
        "w_a": w["rg_w_a"][0].astype(BF16), "b_a": row(w["rg_b_a"][0]),
        "w_i": w["rg_w_i"][0].astype(BF16), "b_i": row(w["rg_b_i"][0]),
        "lam": row(w["rg_lambda"][0]),
    }
    out["rg_wout"] = w["rg_w_out"][0].astype(BF16)
    out["ffn"] = (w["ffn_w_gate"].astype(BF16), w["ffn_w_up"].astype(BF16), w["ffn_w_down"].astype(BF16))
    out["moe"] = (w["moe_w_gate"][0].astype(BF16), w["moe_w_up"][0].astype(BF16), w["moe_w_down"][0].astype(BF16))
    e = w["moe_w_router"].shape[-1]
    out["router_t"] = jnp.concatenate(
        [w["moe_w_router"][0].T, jnp.zeros((2 * SUBLANES - e, w["moe_w_router"].shape[1]), F32)], axis=0)
    out["ple_wg"] = w["ple_w_gate"].astype(BF16)
    out["ple_wp"] = w["ple_w_proj"].astype(BF16)
    kv = w["kv_w_a"]
    out["kv_wc"] = kv[:, :KV_LORA_RANK].astype(BF16)
    out["kv_wr"] = kv[:, KV_LORA_RANK:].astype(BF16)
    out["kv_wrs"] = _swap_halves(kv[:, KV_LORA_RANK:]).astype(BF16)
    out["kv_g"] = row(w["kv_norm_g"])
    out["q_wa"] = w["mla_w_q_a"][0].astype(BF16)
    out["q_g"] = row(w["mla_q_norm_g"][0])
    qb = w["mla_w_q_b"][0].reshape(-1, N_HEADS, QK_NOPE_DIM + QK_ROPE_DIM)
    lora = qb.shape[0]
    q_nope = qb[:, :, :QK_NOPE_DIM].reshape(lora, -1)
    q_rope = qb[:, :, QK_NOPE_DIM:]
    out["q_wb"] = jnp.concatenate(
        [q_nope, q_rope.reshape(lora, -1), _swap_halves(q_rope).reshape(lora, -1)], axis=1).astype(BF16)
    uk = jnp.transpose(w["kv_w_uk"], (1, 2, 0))
    zk = jnp.zeros_like(uk[0])
    out["wuk"] = jnp.stack([
        jnp.concatenate([jnp.concatenate([uk[2 * j], zk], axis=1), jnp.concatenate([zk, uk[2 * j + 1]], axis=1)],
                        axis=0) for j in range(N_HEADS // 2)]).astype(BF16)
    uv = jnp.transpose(w["kv_w_uv"], (1, 0, 2))
    zv = jnp.zeros_like(uv[0])
    out["wuv"] = jnp.stack([
        jnp.concatenate([jnp.concatenate([uv[2 * j], zv], axis=1), jnp.concatenate([zv, uv[2 * j + 1]], axis=1)],
                        axis=0) for j in range(N_HEADS // 2)]).astype(BF16)
    out["wo"] = w["mla_w_o"][0].astype(BF16)
    return out


def _rope_tables(pos, repeat):
    inv = ROPE_THETA ** (-jnp.arange(0, QK_ROPE_DIM, 2, dtype=F32) / QK_ROPE_DIM)
    ang = pos[:, None] * inv[None, :]
    cos, sin = jnp.cos(ang), jnp.sin(ang)
    cos_k = jnp.tile(jnp.concatenate([cos, cos], axis=-1), (repeat, 1))
    sin_k = jnp.tile(jnp.concatenate([-sin, sin], axis=-1), (repeat, 1))
    return cos_k, sin_k, jnp.tile(cos_k, (1, N_HEADS)), jnp.tile(sin_k, (1, N_HEADS))


def _tile_rows(t, want):
    tm = min(want, t)
    assert t % tm == 0
    return tm


def _trunk(x3, p4, conv0, rnn0, pos, pw, paged):
    nb, s, d = x3.shape
    t = nb * s
    x = x3.reshape(t, d)
    p = p4.reshape(DEPTH, t, -1)
    tm = _tile_rows(t, 512)

    gate, u = _rg_in(x, pw["rg_wg"], pw["rg_wx"], tm)
    if paged is None:
        conv_pad = jnp.concatenate(
            [jnp.zeros((nb, SUBLANES - (CONV_WIDTH - 1), D_RNN), F32), conv0], axis=1)
        hg, h_last = _rglru_seq(gate, u, conv_pad, rnn0[:, None, :], pw["rg"], nb, s, min(s, 256), 512)
        h_last = h_last[:, 0, :]
        tabs = _rope_tables(pos, 1)
    else:
        to_tm = lambda a: jnp.transpose(a.reshape(nb, s, -1), (1, 0, 2))
        hg, h_last = _rglru_step(to_tm(gate), to_tm(u), jnp.transpose(conv0, (1, 0, 2)), rnn0, pw["rg"], 512)
        hg = jnp.transpose(hg, (1, 0, 2)).reshape(t, -1)
        tabs = _rope_tables(pos, nb)
    conv_state = u.reshape(nb, s, -1)[:, s - (CONV_WIDTH - 1):, :]
    x = _mm_ln(hg, pw["rg_wout"], x, *pw["ln_mix"][0], tm)
    x = _ffn(jnp.zeros((t // tm,), jnp.int32), x, *pw["ffn"], pw["ln_ffn"][0], tm, pw["ffn"][0].shape[2] // 2)
    x = _ple(x, p[0], pw["ple_wg"][0], pw["ple_wp"][0], tm)
    cos_k, sin_k, cos_q, sin_q = tabs
    kv_args = (x, pw["kv_wc"], pw["kv_wr"], pw["kv_wrs"], pw["kv_g"], cos_k, sin_k, tm)

    tq = _tile_rows(t, 256)
    if paged is None:
        ckv, kpe, kv, vt = _kva(*kv_args, attn_operands=True)
        q = _q_proj(x, pw["q_wa"], pw["q_g"], pw["q_wb"], pw["wuk"], cos_q, sin_q, tq, BF16)
        o = _attn_prompt(q, kv, vt, nb, s, tq, 256)
    else:
        cache_ckv, cache_kpe, page_table = paged
        ckv, kpe = _kva(*kv_args)
        q = _q_proj(x, pw["q_wa"], pw["q_g"], pw["q_wb"], pw["wuk"], cos_q, sin_q, tq, F32)
        o = _attn_paged(q.reshape(t // tq, N_HEADS, tq // s, s, QK_WIDTH), ckv.reshape(nb, s, -1),
                        kpe.reshape(nb, s, -1), cache_ckv, jnp.swapaxes(cache_kpe, 1, 2), page_table, 16, 2)
        o = o.reshape(t * N_HEADS, KV_LORA_RANK)
    x = _mla_out(o, pw["wuv"], pw["wo"], x, *pw["ln_mix"][1], tq)
    x = _moe(x, pw["router_t"], *pw["moe"], *pw["ln_ffn"][1], tq, 512 if paged is None else 256,
             pw["moe"][0].shape[2] // 2)
    x = _ple(x, p[1], pw["ple_wg"][1], pw["ple_wp"][1], tm)
    return (x.reshape(nb, s, d), conv_state[None], h_last[None], ckv.reshape(nb, s, -1), kpe.reshape(nb, s, -1))


def kernel(x_prompt, x_sample, p_prompt, p_sample, state_conv, state_rnn, cache_ckv, cache_kpe, page_table, ln_mix_g, ln_mix_b, ln_ffn_g, ln_ffn_b, rg_w_gate, rg_w_x, rg_conv_w, rg_conv_b, rg_w_a, rg_b_a, rg_w_i, rg_b_i, rg_lambda, rg_w_out, mla_w_q_a, mla_q_norm_g, mla_w_q_b, mla_w_o, kv_w_a, kv_norm_g, kv_w_uk, kv_w_uv, ffn_w_gate, ffn_w_up, ffn_w_down, moe_w_router, moe_w_gate, moe_w_up, moe_w_down, ple_w_gate, ple_w_proj):
    w = dict(
        ln_mix_g=ln_mix_g, ln_mix_b=ln_mix_b, ln_ffn_g=ln_ffn_g, ln_ffn_b=ln_ffn_b,
        rg_w_gate=rg_w_gate, rg_w_x=rg_w_x, rg_conv_w=rg_conv_w, rg_conv_b=rg_conv_b, rg_w_a=rg_w_a,
        rg_b_a=rg_b_a, rg_w_i=rg_w_i, rg_b_i=rg_b_i, rg_lambda=rg_lambda, rg_w_out=rg_w_out,
        mla_w_q_a=mla_w_q_a, mla_q_norm_g=mla_q_norm_g, mla_w_q_b=mla_w_q_b, mla_w_o=mla_w_o,
        kv_w_a=kv_w_a, kv_norm_g=kv_norm_g, kv_w_uk=kv_w_uk, kv_w_uv=kv_w_uv,
        ffn_w_gate=ffn_w_gate, ffn_w_up=ffn_w_up, ffn_w_down=ffn_w_down,
        moe_w_router=moe_w_router, moe_w_gate=moe_w_gate, moe_w_up=moe_w_up, moe_w_down=moe_w_down,
        ple_w_gate=ple_w_gate, ple_w_proj=ple_w_proj)
    pw = _prep_weights(w)
    nb, s = x_prompt.shape[:2]
    dec_s = x_sample.shape[1]
    past_len = page_table.shape[1] * cache_ckv.shape[1]
    n_a = state_conv.shape[0]
    assert n_a == 1
    conv0_p = jnp.zeros((nb, CONV_WIDTH - 1, D_RNN), state_conv.dtype)
    rnn0_p = jnp.zeros((nb, D_RNN), state_rnn.dtype)
    y_p, conv_p, rnn_p, ckv_p, kpe_p = _trunk(
        x_prompt, p_prompt, conv0_p, rnn0_p, jnp.arange(s, dtype=F32), pw, None)
    y_s, conv_s, rnn_s, ckv_s, kpe_s = _trunk(
        x_sample, p_sample, state_conv[0], state_rnn[0], past_len + jnp.arange(dec_s, dtype=F32), pw,
        (cache_ckv, cache_kpe, page_table))
    return (y_p, y_s, conv_p, rnn_p, ckv_p, kpe_p, conv_s, rnn_s, ckv_s, kpe_s)
```

```python
import functools

import jax
import jax.numpy as jnp
from jax import lax
from jax.experimental import pallas as pl
from jax.experimental.pallas import tpu as pltpu

F32 = jnp.float32
BF16 = jnp.bfloat16

D_RNN = 1536
RNN_BLOCK = 128
CONV_WIDTH = 4
LRU_C = 8.0
N_HEADS = 16
KV_LORA_RANK = 256
QK_NOPE_DIM = 64
QK_ROPE_DIM = 32
V_HEAD_DIM = 64
ROPE_THETA = 10000.0
SOFTMAX_SCALE = (QK_NOPE_DIM + QK_ROPE_DIM) ** -0.5
N_EXPERTS = 8
LN_EPS = 1e-5
RMS_EPS = 1e-6
DEPTH = 2
DEEPNORM_ALPHA = (2.0 * DEPTH) ** 0.25

QK_WIDTH = KV_LORA_RANK + QK_ROPE_DIM
QK_SCALE = SOFTMAX_SCALE * 1.4426950408889634

SUBLANES = 8
LANES = 128
VMEM_LIMIT = 48 * 1024 * 1024


def _params(*sem):
    return pltpu.CompilerParams(dimension_semantics=sem, vmem_limit_bytes=VMEM_LIMIT)


def _dot(a, b):
    return jnp.dot(a, b, preferred_element_type=F32)


def _dot_nt(a, b):
    return lax.dot_general(a, b, (((1,), (1,)), ((), ())), preferred_element_type=F32)


def _layer_norm(z, g, b):
    mu = jnp.mean(z, -1, keepdims=True)
    d = z - mu
    var = jnp.mean(d * d, -1, keepdims=True)
    return d * lax.rsqrt(var + LN_EPS) * g + b


def _rms_norm(z, g):
    return z * lax.rsqrt(jnp.mean(z * z, -1, keepdims=True) + RMS_EPS) * g


def _full(shape):
    n = len(shape)
    return pl.BlockSpec(shape, lambda *_: (0,) * n)


def _rg_in_kernel(x_ref, wg_ref, wx_ref, gate_ref, u_ref):
    xb = x_ref[...].astype(BF16)
    gate_ref[...] = jax.nn.gelu(_dot(xb, wg_ref[...]))
    u_ref[...] = _dot(xb, wx_ref[...])


def _rg_in(x, wg, wx, tm):
    t, d = x.shape
    n = wg.shape[1]
    return pl.pallas_call(
        _rg_in_kernel,
        grid=(t // tm,),
        in_specs=[pl.BlockSpec((tm, d), lambda i: (i, 0)), _full((d, n)), _full((d, n))],
        out_specs=[pl.BlockSpec((tm, n), lambda i: (i, 0))] * 2,
        out_shape=[jax.ShapeDtypeStruct((t, n), F32)] * 2,
        compiler_params=_params("parallel"),
        name="rg_in",
    )(x, wg, wx)


def _rglru_gates(conv, wa_ref, ba, wi_ref, bi, lam):
    nblk = conv.shape[1] // RNN_BLOCK
    cb = conv.astype(BF16)
    ra = jnp.concatenate(
        [_dot(cb[:, n * RNN_BLOCK:(n + 1) * RNN_BLOCK], wa_ref[n]) for n in range(nblk)], axis=1)
    ia = jnp.concatenate(
        [_dot(cb[:, n * RNN_BLOCK:(n + 1) * RNN_BLOCK], wi_ref[n]) for n in range(nblk)], axis=1)
    r = jax.nn.sigmoid(ra + ba)
    i = jax.nn.sigmoid(ia + bi)
    z = -lam
    softplus = jnp.maximum(z, 0.0) + jnp.log1p(jnp.exp(-jnp.abs(z)))
    log_a = -LRU_C * r * softplus
    a = jnp.exp(log_a)
    v = 1.0 - a * a
    b = jnp.where(v > 0.0, v * lax.rsqrt(v), 0.0) * (i * conv)
    return a, b


def _rglru_seq_kernel(gate_ref, u_ref, conv0_ref, h0_ref, cw_ref, cb_ref, wa_ref, ba_ref, wi_ref, bi_ref,
                      lam_ref, hg_ref, hlast_ref, ubuf, hcar, abuf, bbuf):
    c = pl.program_id(2)
    tc, db = u_ref.shape
    halo = SUBLANES

    @pl.when(c == 0)
    def _():
        ubuf[0:halo, :] = conv0_ref[0]
        hcar[...] = jnp.broadcast_to(h0_ref[0], hcar.shape)

    ubuf[halo:halo + tc, :] = u_ref[...]
    cw = cw_ref[...]
    conv = cb_ref[...]
    for k in range(CONV_WIDTH):
        off = halo - (CONV_WIDTH - 1) + k
        conv = conv + ubuf[off:off + tc, :] * cw[k:k + 1, :]
    ubuf[0:halo, :] = ubuf[tc:tc + halo, :]

    a, b = _rglru_gates(conv, wa_ref, ba_ref[...], wi_ref, bi_ref[...], lam_ref[...])

    groups = tc // SUBLANES
    a = a.reshape(groups, SUBLANES, db)
    b = b.reshape(groups, SUBLANES, db)
    row = lax.broadcasted_iota(jnp.int32, a.shape, 1)
    shift = 1
    while shift < SUBLANES:
        a_prev = pltpu.roll(a, shift, 1)
        b_prev = pltpu.roll(b, shift, 1)
        keep = row >= shift
        b = jnp.where(keep, a * b_prev + b, b)
        a = jnp.where(keep, a * a_prev, a)
        shift *= 2
    abuf[...] = a.reshape(tc, db)
    bbuf[...] = b.reshape(tc, db)

    def group(g, h):
        off = pl.multiple_of(g * SUBLANES, SUBLANES)
        hb = abuf[pl.ds(off, SUBLANES), :] * h + bbuf[pl.ds(off, SUBLANES), :]
        bbuf[pl.ds(off, SUBLANES), :] = hb
        return jnp.broadcast_to(hb[SUBLANES - 1:SUBLANES, :], hb.shape)

    h = lax.fori_loop(0, tc // SUBLANES, group, hcar[...])
    hcar[...] = h
    hg_ref[...] = (bbuf[...] * gate_ref[...]).astype(BF16)

    @pl.when(c == pl.num_programs(2) - 1)
    def _():
        hlast_ref[0] = h[0:1, :]


def _rglru_seq(gate, u, conv0, h0, rw, nb, s, tc, db):
    t, d = u.shape
    nc = s // tc
    kb = db // RNN_BLOCK
    row = lambda b, j, c: (b * nc + c, j)
    vec = pl.BlockSpec((1, db), lambda b, j, c: (0, j))
    blk = pl.BlockSpec((kb, RNN_BLOCK, RNN_BLOCK), lambda b, j, c: (j, 0, 0))
    return pl.pallas_call(
        _rglru_seq_kernel,
        grid=(nb, d // db, nc),
        in_specs=[
            pl.BlockSpec((tc, db), row), pl.BlockSpec((tc, db), row),
            pl.BlockSpec((1, SUBLANES, db), lambda b, j, c: (b, 0, j)),
            pl.BlockSpec((1, 1, db), lambda b, j, c: (b, 0, j)),
            pl.BlockSpec((CONV_WIDTH, db), lambda b, j, c: (0, j)), vec, blk, vec, blk, vec, vec,
        ],
        out_specs=[pl.BlockSpec((tc, db), row), pl.BlockSpec((1, 1, db), lambda b, j, c: (b, 0, j))],
        out_shape=[jax.ShapeDtypeStruct((t, d), BF16), jax.ShapeDtypeStruct((nb, 1, d), F32)],
        scratch_shapes=[pltpu.VMEM((tc + SUBLANES, db), F32), pltpu.VMEM((SUBLANES, db), F32),
                        pltpu.VMEM((tc, db), F32), pltpu.VMEM((tc, db), F32)],
        compiler_params=_params("parallel", "parallel", "arbitrary"),
        name="rglru_seq",
    )(gate, u, conv0, h0, rw["conv_w"], rw["conv_b"], rw["w_a"], rw["b_a"], rw["w_i"], rw["b_i"], rw["lam"])


def _rglru_step_kernel(gate_ref, u_ref, conv0_ref, h0_ref, cw_ref, cb_ref, wa_ref, ba_ref, wi_ref, bi_ref,
                       lam_ref, hg_ref, hlast_ref):
    s, nb, db = u_ref.shape
    ue = jnp.concatenate([conv0_ref[...], u_ref[...]], axis=0)
    cw = cw_ref[...]
    conv = cb_ref[...][None]
    for k in range(CONV_WIDTH):
        conv = conv + ue[k:k + s] * cw[k:k + 1, :][None]
    a, b = _rglru_gates(conv.reshape(s * nb, db), wa_ref, ba_ref[...], wi_ref, bi_ref[...], lam_ref[...])
    a = a.reshape(s, nb, db)
    b = b.reshape(s, nb, db)
    h = h0_ref[...]
    for t in range(s):
        h = a[t] * h + b[t]
        hg_ref[t] = (h * gate_ref[t]).astype(BF16)
    hlast_ref[...] = h


def _rglru_step(gate, u, conv0, h0, rw, db):
    s, nb, d = u.shape
    kb = db // RNN_BLOCK
    cube = pl.BlockSpec((s, nb, db), lambda j: (0, 0, j))
    vec = pl.BlockSpec((1, db), lambda j: (0, j))
    blk = pl.BlockSpec((kb, RNN_BLOCK, RNN_BLOCK), lambda j: (j, 0, 0))
    return pl.pallas_call(
        _rglru_step_kernel,
        grid=(d // db,),
        in_specs=[cube, cube, pl.BlockSpec((CONV_WIDTH - 1, nb, db), lambda j: (0, 0, j)),
                  pl.BlockSpec((nb, db), lambda j: (0, j)),
                  pl.BlockSpec((CONV_WIDTH, db), lambda j: (0, j)), vec, blk, vec, blk, vec, vec],
        out_specs=[cube, pl.BlockSpec((nb, db), lambda j: (0, j))],
        out_shape=[jax.ShapeDtypeStruct((s, nb, d), BF16), jax.ShapeDtypeStruct((nb, d), F32)],
        compiler_params=_params("parallel"),
        name="rglru_step",
    )(gate, u, conv0, h0, rw["conv_w"], rw["conv_b"], rw["w_a"], rw["b_a"], rw["w_i"], rw["b_i"], rw["lam"])


def _mm_ln_kernel(x_ref, w_ref, res_ref, g_ref, b_ref, o_ref):
    y = _dot(x_ref[...], w_ref[...])
    o_ref[...] = _layer_norm(DEEPNORM_ALPHA * res_ref[...] + y, g_ref[...], b_ref[...])


def _mm_ln(x, w, res, g, b, tm):
    t, k = x.shape
    n = w.shape[1]
    return pl.pallas_call(
        _mm_ln_kernel,
        grid=(t // tm,),
        in_specs=[pl.BlockSpec((tm, k), lambda i: (i, 0)), _full((k, n)),
                  pl.BlockSpec((tm, n), lambda i: (i, 0)), _full((1, n)), _full((1, n))],
        out_specs=pl.BlockSpec((tm, n), lambda i: (i, 0)),
        out_shape=jax.ShapeDtypeStruct((t, n), F32),
        compiler_params=_params("parallel"),
        name="mm_ln",
    )(x, w, res, g, b)


def _ffn_kernel(te_ref, x_ref, wg_ref, wu_ref, wd_ref, *rest, post_norm):
    del te_ref
    if post_norm:
        g_ref, b_ref, o_ref, xb_ref, acc_ref = rest
    else:
        o_ref, xb_ref, acc_ref = rest
    c = pl.program_id(1)

    @pl.when(c == 0)
    def _():
        xb_ref[...] = x_ref[...].astype(BF16)
        acc_ref[...] = jnp.zeros_like(acc_ref)

    xb = xb_ref[...]
    h = (jax.nn.silu(_dot(xb, wg_ref[0])) * _dot(xb, wu_ref[0])).astype(BF16)
    acc_ref[...] += _dot(h, wd_ref[0])

    @pl.when(c == pl.num_programs(1) - 1)
    def _():
        if post_norm:
            o_ref[...] = _layer_norm(DEEPNORM_ALPHA * x_ref[...] + acc_ref[...], g_ref[...], b_ref[...])
        else:
            o_ref[...] = acc_ref[...]


def _ffn(tile_expert, x, wg, wu, wd, norm, tm, tf):
    r, d = x.shape
    f = wg.shape[2]
    once = dict(pipeline_mode=pl.Buffered(1)) if (wg.shape[0] == 1 and tf == f) else {}
    in_specs = [
        pl.BlockSpec((tm, d), lambda i, c, te: (i, 0)),
        pl.BlockSpec((1, d, tf), lambda i, c, te: (te[i], 0, c), **once),
        pl.BlockSpec((1, d, tf), lambda i, c, te: (te[i], 0, c), **once),
        pl.BlockSpec((1, tf, d), lambda i, c, te: (te[i], c, 0), **once),
    ]
    args = [x, wg, wu, wd]
    if norm is not None:
        in_specs += [pl.BlockSpec((1, d), lambda i, c, te: (0, 0))] * 2
        args += list(norm)
    return pl.pallas_call(
        functools.partial(_ffn_kernel, post_norm=norm is not None),
        grid_spec=pltpu.PrefetchScalarGridSpec(
            num_scalar_prefetch=1,
            grid=(r // tm, f // tf),
            in_specs=in_specs,
            out_specs=pl.BlockSpec((tm, d), lambda i, c, te: (i, 0)),
            scratch_shapes=[pltpu.VMEM((tm, d), BF16), pltpu.VMEM((tm, d), F32)],
        ),
        out_shape=jax.ShapeDtypeStruct((r, d), F32),
        compiler_params=_params("parallel", "arbitrary"),
        name="ffn",
    )(tile_expert, *args)


def _ple_kernel(x_ref, p_ref, wg_ref, wp_ref, o_ref):
    x = x_ref[...]
    a = _dot(x.astype(BF16), wg_ref[...])
    c = _dot(p_ref[...].astype(BF16), wp_ref[...])
    o_ref[...] = x + jax.nn.sigmoid(a) * c


def _ple(x, p, wg, wp, tm):
    t, d = x.shape
    dp = p.shape[1]
    return pl.pallas_call(
        _ple_kernel,
        grid=(t // tm,),
        in_specs=[pl.BlockSpec((tm, d), lambda i: (i, 0)), pl.BlockSpec((tm, dp), lambda i: (i, 0)),
                  _full((d, d)), _full((dp, d))],
        out_specs=pl.BlockSpec((tm, d), lambda i: (i, 0)),
        out_shape=jax.ShapeDtypeStruct((t, d), F32),
        compiler_params=_params("parallel"),
        name="ple",
    )(x, p, wg, wp)


def _kva_kernel(x_ref, wc_ref, wr_ref, wrs_ref, g_ref, cos_ref, sin_ref, *rest, attn_operands):
    xb = x_ref[...].astype(BF16)
    ckv = _rms_norm(_dot(xb, wc_ref[...]), g_ref[...])
    kpe = _dot(xb, wr_ref[...]) * cos_ref[...] + _dot(xb, wrs_ref[...]) * sin_ref[...]
    if not attn_operands:
        ckv_ref, kpe_ref = rest
        ckv_ref[...] = ckv
        kpe_ref[...] = kpe
        return
    ckv_ref, kpe_ref, kv_ref, vt_ref = rest
    ckv_ref[...] = ckv
    kpe_ref[...] = kpe
    kv_ref[:, :KV_LORA_RANK] = ckv.astype(BF16)
    kv_ref[:, KV_LORA_RANK:] = kpe.astype(BF16)
    vt_ref[...] = ckv.T.astype(BF16)


def _kva(x, wc, wr, wrs, g, cos, sin, tm, attn_operands=False):
    t, d = x.shape
    nper = cos.shape[0] // tm
    tab = pl.BlockSpec((tm, QK_ROPE_DIM), lambda i: (i % nper, 0))
    out = lambda n: pl.BlockSpec((tm, n), lambda i: (i, 0))
    out_specs = [out(KV_LORA_RANK), out(QK_ROPE_DIM)]
    out_shape = [jax.ShapeDtypeStruct((t, KV_LORA_RANK), F32), jax.ShapeDtypeStruct((t, QK_ROPE_DIM), F32)]
    if attn_operands:
        out_specs += [out(QK_WIDTH), pl.BlockSpec((KV_LORA_RANK, tm), lambda i: (0, i))]
        out_shape += [jax.ShapeDtypeStruct((t, QK_WIDTH), BF16), jax.ShapeDtypeStruct((KV_LORA_RANK, t), BF16)]
    return pl.pallas_call(
        functools.partial(_kva_kernel, attn_operands=attn_operands),
        grid=(t // tm,),
        in_specs=[pl.BlockSpec((tm, d), lambda i: (i, 0)), _full(wc.shape), _full(wr.shape), _full(wrs.shape),
                  _full(g.shape), tab, tab],
        out_specs=out_specs,
        out_shape=out_shape,
        compiler_params=_params("parallel"),
        name="kv_latent",
    )(x, wc, wr, wrs, g, cos, sin)


def _q_kernel(x_ref, wqa_ref, qg_ref, wqb_ref, wuk_ref, cos_ref, sin_ref, q_ref):
    tm = x_ref.shape[0]
    nope = N_HEADS * QK_NOPE_DIM
    pe = N_HEADS * QK_ROPE_DIM
    c = KV_LORA_RANK
    r = QK_ROPE_DIM
    cq = _rms_norm(_dot(x_ref[...].astype(BF16), wqa_ref[...]), qg_ref[...])
    q = _dot(cq.astype(BF16), wqb_ref[...])
    q_pe = (q[:, nope:nope + pe] * cos_ref[...] + q[:, nope + pe:nope + 2 * pe] * sin_ref[...]) * QK_SCALE
    qn = q[:, :nope].astype(BF16)
    pair = 2 * QK_NOPE_DIM
    for j in range(N_HEADS // 2):
        ql = _dot(qn[:, j * pair:(j + 1) * pair], wuk_ref[j]) * QK_SCALE
        for k in range(2):
            h = 2 * j + k
            q_ref[h * tm:(h + 1) * tm, :c] = ql[:, k * c:(k + 1) * c].astype(q_ref.dtype)
            q_ref[h * tm:(h + 1) * tm, c:] = q_pe[:, h * r:(h + 1) * r].astype(q_ref.dtype)


def _q_proj(x, wqa, qg, wqb, wuk, cos, sin, tm, dtype):
    t, d = x.shape
    nper = cos.shape[0] // tm
    pe = N_HEADS * QK_ROPE_DIM
    tab = pl.BlockSpec((tm, pe), lambda i: (i % nper, 0))
    return pl.pallas_call(
        _q_kernel,
        grid=(t // tm,),
        in_specs=[pl.BlockSpec((tm, d), lambda i: (i, 0)), _full(wqa.shape), _full(qg.shape), _full(wqb.shape),
                  _full(wuk.shape), tab, tab],
        out_specs=pl.BlockSpec((N_HEADS * tm, QK_WIDTH), lambda i: (i, 0)),
        out_shape=jax.ShapeDtypeStruct((N_HEADS * t, QK_WIDTH), dtype),
        compiler_params=_params("parallel"),
        name="q_proj",
    )(x, wqa, qg, wqb, wuk, cos, sin)


def _widen(stat, width):
    return jnp.concatenate([stat] * (width // LANES), axis=1)


def _softmax_step(s, kv, m_s, l_s, acc):
    m_prev = m_s[...]
    m_new = jnp.maximum(m_prev, jnp.max(s, -1, keepdims=True))
    corr = jnp.exp2(m_prev - m_new)
    p = jnp.exp2(s - _widen(m_new, s.shape[1]))
    l_s[...] = l_s[...] * corr + jnp.sum(p, -1, keepdims=True)
    acc[...] = acc[...] * _widen(corr, KV_LORA_RANK) + _dot(p.astype(BF16), kv)
    m_s[...] = m_new


def _attn_kernel(qi_ref, kj_ref, q_ref, kv_ref, vt_ref, o_ref, m_s, l_s, acc, *, tq, tk):
    p = pl.program_id(1)
    i = qi_ref[p]
    j = kj_ref[p]

    @pl.when(j == 0)
    def _():
        m_s[...] = jnp.full_like(m_s, -jnp.inf)
        l_s[...] = jnp.zeros_like(l_s)
        acc[...] = jnp.zeros_like(acc)

    def step(masked):
        st = _dot_nt(kv_ref[...], q_ref[...])
        if masked:
            k_pos = j * tk + lax.broadcasted_iota(jnp.int32, st.shape, 0)
            q_pos = i * tq + (lax.broadcasted_iota(jnp.int32, st.shape, 1) & (tq - 1))
            st = jnp.where(k_pos <= q_pos, st, -jnp.inf)
        m_prev = m_s[...]
        m_new = jnp.maximum(m_prev, jnp.max(st, 0, keepdims=True))
        corr = jnp.exp2(m_prev - m_new)
        pt = jnp.exp2(st - m_new)
        l_s[...] = l_s[...] * corr + jnp.sum(pt, 0, keepdims=True)
        acc[...] = acc[...] * corr + _dot(vt_ref[...], pt.astype(BF16))
        m_s[...] = m_new

    crosses_diagonal = j * tk + tk - 1 > i * tq

    @pl.when(jnp.logical_not(crosses_diagonal))
    def _():
        step(False)

    @pl.when(crosses_diagonal)
    def _():
        step(True)

    @pl.when(j == (i * tq + tq - 1) // tk)
    def _():
        o_ref[...] = (acc[...] * (1.0 / l_s[...])).T.astype(o_ref.dtype)


def _attn_prompt(q, kv, vt, nb, s, tq, tk):
    nq, nk = s // tq, s // tk
    assert tq & (tq - 1) == 0
    rows = N_HEADS * tq
    pairs = [(i, j) for i in range(nq) for j in range((i * tq + tq - 1) // tk + 1)]
    qi = jnp.asarray([i for i, _ in pairs], jnp.int32)
    kj = jnp.asarray([j for _, j in pairs], jnp.int32)
    q_row = lambda b, p, qi, kj: (b * nq + qi[p], 0)
    return pl.pallas_call(
        functools.partial(_attn_kernel, tq=tq, tk=tk),
        grid_spec=pltpu.PrefetchScalarGridSpec(
            num_scalar_prefetch=2,
            grid=(nb, len(pairs)),
            in_specs=[pl.BlockSpec((rows, QK_WIDTH), q_row),
                      pl.BlockSpec((tk, QK_WIDTH), lambda b, p, qi, kj: (b * nk + kj[p], 0)),
                      pl.BlockSpec((KV_LORA_RANK, tk), lambda b, p, qi, kj: (0, b * nk + kj[p]))],
            out_specs=pl.BlockSpec((rows, KV_LORA_RANK), q_row),
            scratch_shapes=[pltpu.VMEM((1, rows), F32), pltpu.VMEM((1, rows), F32),
                            pltpu.VMEM((KV_LORA_RANK, rows), F32)],
        ),
        out_shape=jax.ShapeDtypeStruct((q.shape[0], KV_LORA_RANK), BF16),
        compiler_params=_params("parallel", "arbitrary"),
        name="attn_prompt",
    )(qi, kj, q, kv, vt)


def _attn_paged_kernel(pt_ref, q_ref, cnew_ref, knew_ref, ckv_hbm, kpe_hbm, o_ref, m_s, l_s, acc, kbuf, kpbuf,
                       ck_in, kp_in, sem, *, seqs, pages, s_new):
    g = pl.program_id(0)
    c = pl.program_id(1)
    nc = pl.num_programs(1)
    n_pages = nc * pages
    step = g * nc + c
    rows = N_HEADS * s_new
    page = ck_in.shape[2]

    def page_copies(group, chunk, slot):
        copies = []
        for a in range(seqs):
            for k in range(pages):
                pid = pt_ref[(group * seqs + a) * n_pages + chunk * pages + k]
                copies.append(pltpu.make_async_copy(ckv_hbm.at[pid], ck_in.at[slot, a * pages + k], sem.at[slot, 0]))
                copies.append(pltpu.make_async_copy(kpe_hbm.at[pid], kp_in.at[slot, a * pages + k], sem.at[slot, 1]))
        return copies

    @pl.when(step == 0)
    def _():
        for cp in page_copies(0, 0, 0):
            cp.start()

    @pl.when(step + 1 < pl.num_programs(0) * nc)
    def _():
        nxt = step + 1
        for cp in page_copies(nxt // nc, nxt % nc, nxt % 2):
            cp.start()

    slot = step % 2
    pltpu.make_async_copy(ckv_hbm.at[pl.ds(0, seqs * pages)], ck_in.at[slot], sem.at[slot, 0]).wait()
    pltpu.make_async_copy(kpe_hbm.at[pl.ds(0, seqs * pages)], kp_in.at[slot], sem.at[slot, 1]).wait()
    ck_refs = [ck_in.at[slot, i] for i in range(seqs * pages)]
    kp_refs = [kp_in.at[slot, i] for i in range(seqs * pages)]

    qs = [q_ref[:, a].reshape(rows, QK_WIDTH).astype(BF16) for a in range(seqs)]
    qls = [q[:, :KV_LORA_RANK] for q in qs]
    qps = [q[:, KV_LORA_RANK:] for q in qs]

    @pl.when(c == 0)
    def _():
        for a in range(seqs):
            pad = 2 * SUBLANES - s_new
            cn = jnp.concatenate([cnew_ref[a], jnp.zeros((pad, KV_LORA_RANK), F32)], axis=0).astype(BF16)
            kn = jnp.concatenate([knew_ref[a], jnp.zeros((pad, QK_ROPE_DIM), F32)], axis=0).astype(BF16)
            s = _dot_nt(qls[a], cn) + _dot_nt(qps[a], kn)
            tok = lax.broadcasted_iota(jnp.int32, s.shape, 0) & (s_new - 1)
            key = lax.broadcasted_iota(jnp.int32, s.shape, 1)
            s = jnp.where(key <= tok, s, -jnp.inf)
            m = jnp.max(s, -1, keepdims=True)
            p = jnp.exp2(s - m)
            m_s[a] = jnp.broadcast_to(m, m_s.shape[1:])
            l_s[a] = jnp.broadcast_to(jnp.sum(p, -1, keepdims=True), l_s.shape[1:])
            acc[a] = _dot(p.astype(BF16), cn)

    for a in range(seqs):
        for k in range(pages):
            kbuf[a, k * page:(k + 1) * page, :] = ck_refs[a * pages + k][...].astype(BF16)
            kpbuf[a, :, k * page:(k + 1) * page] = kp_refs[a * pages + k][...].astype(BF16)
        kv = kbuf[a]
        s = _dot_nt(qls[a], kv) + _dot(qps[a], kpbuf[a])
        _softmax_step(s, kv, m_s.at[a], l_s.at[a], acc.at[a])

    @pl.when(c == pl.num_programs(1) - 1)
    def _():
        for a in range(seqs):
            o = acc[a] * _widen(1.0 / l_s[a], KV_LORA_RANK)
            o_ref[:, a] = o.reshape(N_HEADS, s_new, KV_LORA_RANK)


def _attn_paged(q, cnew, knew, cache_ckv, cache_kpe_t, page_table, pages, seqs):
    n_tiles, _, per_tile, s_new, _ = q.shape
    nb = n_tiles * per_tile
    assert s_new == SUBLANES and per_tile % seqs == 0
    groups = per_tile // seqs
    n_pages = page_table.shape[1]
    page = cache_ckv.shape[1]
    c = KV_LORA_RANK
    r = QK_ROPE_DIM
    per_g = lambda g, j, pt: (g, 0, 0)
    q_blk = lambda w: pl.BlockSpec((None, N_HEADS, seqs, s_new, w),
                                   lambda g, j, pt: (g // groups, 0, g % groups, 0, 0))

    in_specs = [q_blk(QK_WIDTH), pl.BlockSpec((seqs, s_new, c), per_g), pl.BlockSpec((seqs, s_new, r), per_g),
                pl.BlockSpec(memory_space=pl.ANY), pl.BlockSpec(memory_space=pl.ANY)]
    rows = N_HEADS * s_new
    assert n_pages % pages == 0
    return pl.pallas_call(
        functools.partial(_attn_paged_kernel, seqs=seqs, pages=pages, s_new=s_new),
        grid_spec=pltpu.PrefetchScalarGridSpec(
            num_scalar_prefetch=1,
            grid=(nb // seqs, n_pages // pages),
            in_specs=in_specs,
            out_specs=q_blk(c),
            scratch_shapes=[pltpu.VMEM((seqs, rows, LANES), F32), pltpu.VMEM((seqs, rows, LANES), F32),
                            pltpu.VMEM((seqs, rows, c), F32), pltpu.VMEM((seqs, pages * page, c), BF16),
                            pltpu.VMEM((seqs, r, pages * page), BF16),
                            pltpu.VMEM((2, seqs * pages, page, c), F32), pltpu.VMEM((2, seqs * pages, r, page), F32),
                            pltpu.SemaphoreType.DMA((2, 2))],
        ),
        out_shape=jax.ShapeDtypeStruct(q.shape[:-1] + (c,), F32),
        compiler_params=_params("arbitrary", "arbitrary"),
        name="attn_paged",
    )(page_table.reshape(-1), q, cnew, knew, cache_ckv, cache_kpe_t)


def _mla_out_kernel(o_ref, wuv_ref, wo_ref, res_ref, g_ref, b_ref, out_ref):
    tm = res_ref.shape[0]
    head = lambda h: o_ref[h * tm:(h + 1) * tm, :].astype(BF16)
    v = jnp.concatenate(
        [_dot(jnp.concatenate([head(2 * j), head(2 * j + 1)], axis=1), wuv_ref[j]) for j in range(N_HEADS // 2)],
        axis=1).astype(BF16)
    y = _dot(v, wo_ref[...])
    out_ref[...] = _layer_norm(DEEPNORM_ALPHA * res_ref[...] + y, g_ref[...], b_ref[...])


def _mla_out(o, wuv, wo, res, g, b, tm):
    t, n = res.shape
    return pl.pallas_call(
        _mla_out_kernel,
        grid=(t // tm,),
        in_specs=[pl.BlockSpec((N_HEADS * tm, o.shape[1]), lambda i: (i, 0)), _full(wuv.shape), _full(wo.shape),
                  pl.BlockSpec((tm, n), lambda i: (i, 0)), _full((1, n)), _full((1, n))],
        out_specs=pl.BlockSpec((tm, n), lambda i: (i, 0)),
        out_shape=jax.ShapeDtypeStruct((t, n), F32),
        compiler_params=_params("parallel"),
        name="mla_out",
    )(o, wuv, wo, res, g, b)


def _router_kernel(x_ref, w_ref, idx_ref, wt_ref):
    x = x_ref[...]
    w = w_ref[...]
    xh = x.astype(BF16)
    xl = (x - xh.astype(F32)).astype(BF16)
    wh = w.astype(BF16)
    wl = (w - wh.astype(F32)).astype(BF16)
    lg = (_dot_nt(wh, xh) + _dot_nt(wh, xl) + _dot_nt(wl, xh))[:N_EXPERTS]
    e = lax.broadcasted_iota(jnp.int32, lg.shape, 0).astype(F32)
    none = float(N_EXPERTS)
    v1 = jnp.max(lg, 0, keepdims=True)
    i1 = jnp.min(jnp.where(lg == v1, e, none), 0, keepdims=True)
    lg2 = jnp.where(e == i1, -jnp.inf, lg)
    v2 = jnp.max(lg2, 0, keepdims=True)
    i2 = jnp.min(jnp.where(lg2 == v2, e, none), 0, keepdims=True)
    ex = jnp.exp(v2 - v1)
    den = 1.0 + ex
    idx_ref[...] = jnp.concatenate([i1, i2], axis=0).astype(jnp.int32)
    wt_ref[...] = jnp.concatenate([1.0 / den, ex / den], axis=0)


def _router(x, w_t, tm):
    t, d = x.shape
    return pl.pallas_call(
        _router_kernel,
        grid=(t // tm,),
        in_specs=[pl.BlockSpec((tm, d), lambda i: (i, 0)), _full(w_t.shape)],
        out_specs=[pl.BlockSpec((2, tm), lambda i: (0, i))] * 2,
        out_shape=[jax.ShapeDtypeStruct((2, t), jnp.int32), jax.ShapeDtypeStruct((2, t), F32)],
        compiler_params=_params("parallel"),
        name="router",
    )(x, w_t)


def _dispatch_kernel(pos_ref, fill_ref, x_ref, xs_hbm, zeros, sem, fill_sem, *, rows, tile):
    i = pl.program_id(0)
    t = pl.num_programs(0) * rows
    n_fill = fill_ref.shape[0] // 2
    fill = lambda k: pltpu.make_async_copy(
        zeros, xs_hbm.at[pl.ds(pl.multiple_of(fill_ref[2 * k + 1], tile), tile), :], fill_sem)

    @pl.when(i == 0)
    def _():
        zeros[...] = jnp.zeros_like(zeros)
        for k in range(n_fill):
            @pl.when(fill_ref[2 * k] == 1)
            def _(k=k):
                fill(k).start()
        for k in range(n_fill):
            @pl.when(fill_ref[2 * k] == 1)
            def _(k=k):
                fill(k).wait()

    base = i * rows

    def issue(r, carry):
        for k in range(2):
            slot = pos_ref[k * t + base + r]
            pltpu.make_async_copy(x_ref.at[pl.ds(r, 1), :], xs_hbm.at[pl.ds(slot, 1), :], sem.at[k]).start()
        return carry

    lax.fori_loop(0, rows, issue, 0, unroll=4)
    for k in range(2):
        pltpu.make_async_copy(x_ref, xs_hbm.at[pl.ds(0, rows), :], sem.at[k]).wait()


def _dispatch(pos, fill, x, n_slots, rows, tile):
    t, d = x.shape
    return pl.pallas_call(
        functools.partial(_dispatch_kernel, rows=rows, tile=tile),
        grid_spec=pltpu.PrefetchScalarGridSpec(
            num_scalar_prefetch=2,
            grid=(t // rows,),
            in_specs=[pl.BlockSpec((rows, d), lambda i, pos, fill: (i, 0))],
            out_specs=pl.BlockSpec(memory_space=pl.ANY),
            scratch_shapes=[pltpu.VMEM((tile, d), x.dtype), pltpu.SemaphoreType.DMA((2,)),
                            pltpu.SemaphoreType.DMA],
        ),
        out_shape=jax.ShapeDtypeStruct((n_slots, d), x.dtype),
        compiler_params=_params("arbitrary"),
        name="moe_dispatch",
    )(pos, fill, x)


def _combine_kernel(pos_ref, y_hbm, x_ref, w1_ref, w2_ref, g_ref, b_ref, o_ref, y1, y2, sem):
    rows = o_ref.shape[0]
    i = pl.program_id(0)
    n = pl.num_programs(0)
    t = n * rows

    def gather(step, slot):
        base = step * rows

        def issue(r, carry):
            p1 = pos_ref[base + r]
            p2 = pos_ref[t + base + r]
            pltpu.make_async_copy(y_hbm.at[pl.ds(p1, 1), :], y1.at[slot, pl.ds(r, 1), :], sem.at[slot, 0]).start()
            pltpu.make_async_copy(y_hbm.at[pl.ds(p2, 1), :], y2.at[slot, pl.ds(r, 1), :], sem.at[slot, 1]).start()
            return carry

        lax.fori_loop(0, rows, issue, 0, unroll=8)

    @pl.when(i == 0)
    def _():
        gather(0, 0)

    @pl.when(i + 1 < n)
    def _():
        gather(i + 1, (i + 1) % 2)

    slot = i % 2
    pltpu.make_async_copy(y_hbm.at[pl.ds(0, rows), :], y1.at[slot], sem.at[slot, 0]).wait()
    pltpu.make_async_copy(y_hbm.at[pl.ds(0, rows), :], y2.at[slot], sem.at[slot, 1]).wait()
    ff = w1_ref[...] * y1[slot] + w2_ref[...] * y2[slot]
    o_ref[...] = _layer_norm(DEEPNORM_ALPHA * x_ref[...] + ff, g_ref[...], b_ref[...])


def _combine(pos, y, x, w1, w2, g, b, rows):
    t, d = x.shape
    row = pl.BlockSpec((rows, d), lambda i, p: (i, 0))
    col = pl.BlockSpec((rows, 1), lambda i, p: (i, 0))
    vec = pl.BlockSpec((1, d), lambda i, p: (0, 0))
    return pl.pallas_call(
        _combine_kernel,
        grid_spec=pltpu.PrefetchScalarGridSpec(
            num_scalar_prefetch=1,
            grid=(t // rows,),
            in_specs=[pl.BlockSpec(memory_space=pl.ANY), row, col, col, vec, vec],
            out_specs=row,
            scratch_shapes=[pltpu.VMEM((2, rows, d), F32), pltpu.VMEM((2, rows, d), F32),
                            pltpu.SemaphoreType.DMA((2, 2))],
        ),
        out_shape=jax.ShapeDtypeStruct((t, d), F32),
        compiler_params=_params("arbitrary"),
        name="moe_combine",
    )(pos, y, x, w1, w2, g, b)


def _moe(x, w_router_t, wg, wu, wd, g, b, tm, tile, tf):
    t, d = x.shape
    idx, wts = _router(x, w_router_t, tm)
    e_flat = idx.reshape(-1)
    onehot = (e_flat[:, None] == jnp.arange(N_EXPERTS, dtype=jnp.int32)[None, :]).astype(jnp.int32)
    csum = jnp.cumsum(onehot, axis=0)
    rank = jnp.sum(csum * onehot, axis=1) - 1
    counts = csum[-1]
    padded = ((counts + tile - 1) // tile) * tile
    ends = jnp.cumsum(padded)
    starts = ends - padded
    pos = (starts[e_flat] + rank).astype(jnp.int32)
    assert (2 * t) % tile == 0
    n_slots = 2 * t + N_EXPERTS * tile
    tile_start =jnp.arange(n_slots // tile, dtype=jnp.int32) * tile
    tile_expert = jnp.minimum(
        jnp.sum((tile_start[:, None] >= ends[None, :]).astype(jnp.int32), axis=1), N_EXPERTS - 1)
    tail = ends[-1] + jnp.arange(N_EXPERTS, dtype=jnp.int32) * tile
    fill_start = jnp.concatenate([ends - tile, tail])
    fill_valid = jnp.concatenate([padded > counts, tail < n_slots])
    fill = jnp.stack([fill_valid.astype(jnp.int32), jnp.where(fill_valid, fill_start, 0).astype(jnp.int32)],
                     axis=1).reshape(-1)

    xs = _dispatch(pos, fill, x, n_slots, tm, tile)
    ys = _ffn(tile_expert, xs, wg, wu, wd, None, tile, tf)
    return _combine(pos, ys, x, wts[0][:, None], wts[1][:, None], g, b, tm)


def _swap_halves(w):
    half = w.shape[-1] // 2
    return jnp.concatenate([w[..., half:], w[..., :half]], axis=-1)


def _prep_weights(w):
    out = {}
    row = lambda v: v.reshape(1, -1).astype(F32)
    out["ln_mix"] = [(row(w["ln_mix_g"][i]), row(w["ln_mix_b"][i])) for i in range(DEPTH)]
    out["ln_ffn"] = [(row(w["ln_ffn_g"][i]), row(w["ln_ffn_b"][i])) for i in range(DEPTH)]
    out["rg_wg"] = w["rg_w_gate"][0].astype(BF16)
    out["rg_wx"] = w["rg_w_x"][0].astype(BF16)
    out["rg"] = {
        "conv_w": w["rg_conv_w"][0], "conv_b": row(w["rg_conv_b"][0]),
        "w_a": w["rg_w_a"][0].astype(BF16), "b_a": row(w["rg_b_a"][0]),
        "w_i": w["rg_w_i"][0].astype(BF16), "b_i": row(w["rg_b_i"][0]),
        "lam": row(w["rg_lambda"][0]),
    }
    out["rg_wout"] = w["rg_w_out"][0].astype(BF16)
    out["ffn"] = (w["ffn_w_gate"].astype(BF16), w["ffn_w_up"].astype(BF16), w["ffn_w_down"].astype(BF16))
    out["moe"] = (w["moe_w_gate"][0].astype(BF16), w["moe_w_up"][0].astype(BF16), w["moe_w_down"][0].astype(BF16))
    e = w["moe_w_router"].shape[-1]
    out["router_t"] = jnp.concatenate(
        [w["moe_w_router"][0].T, jnp.zeros((2 * SUBLANES - e, w["moe_w_router"].shape[1]), F32)], axis=0)
    out["ple_wg"] = w["ple_w_gate"].astype(BF16)
    out["ple_wp"] = w["ple_w_proj"].astype(BF16)
    kv = w["kv_w_a"]
    out["kv_wc"] = kv[:, :KV_LORA_RANK].astype(BF16)
    out["kv_wr"] = kv[:, KV_LORA_RANK:].astype(BF16)
    out["kv_wrs"] = _swap_halves(kv[:, KV_LORA_RANK:]).astype(BF16)
    out["kv_g"] = row(w["kv_norm_g"])
    out["q_wa"] = w["mla_w_q_a"][0].astype(BF16)
    out["q_g"] = row(w["mla_q_norm_g"][0])
    qb = w["mla_w_q_b"][0].reshape(-1, N_HEADS, QK_NOPE_DIM + QK_ROPE_DIM)
    lora = qb.shape[0]
    q_nope = qb[:, :, :QK_NOPE_DIM].reshape(lora, -1)
    q_rope = qb[:, :, QK_NOPE_DIM:]
    out["q_wb"] = jnp.concatenate(
        [q_nope, q_rope.reshape(lora, -1), _swap_halves(q_rope).reshape(lora, -1)], axis=1).astype(BF16)
    uk = jnp.transpose(w["kv_w_uk"], (1, 2, 0))
    zk = jnp.zeros_like(uk[0])
    out["wuk"] = jnp.stack([
        jnp.concatenate([jnp.concatenate([uk[2 * j], zk], axis=1), jnp.concatenate([zk, uk[2 * j + 1]], axis=1)],
                        axis=0) for j in range(N_HEADS // 2)]).astype(BF16)
    uv = jnp.transpose(w["kv_w_uv"], (1, 0, 2))
    zv = jnp.zeros_like(uv[0])
    out["wuv"] = jnp.stack([
        jnp.concatenate([jnp.concatenate([uv[2 * j], zv], axis=1), jnp.concatenate([zv, uv[2 * j + 1]], axis=1)],
                        axis=0) for j in range(N_HEADS // 2)]).astype(BF16)
    out["wo"] = w["mla_w_o"][0].astype(BF16)
    return out


def _rope_tables(pos, repeat):
    inv = ROPE_THETA ** (-jnp.arange(0, QK_ROPE_DIM, 2, dtype=F32) / QK_ROPE_DIM)
    ang = pos[:, None] * inv[None, :]
    cos, sin = jnp.cos(ang), jnp.sin(ang)
    cos_k = jnp.tile(jnp.concatenate([cos, cos], axis=-1), (repeat, 1))
    sin_k = jnp.tile(jnp.concatenate([-sin, sin], axis=-1), (repeat, 1))
    return cos_k, sin_k, jnp.tile(cos_k, (1, N_HEADS)), jnp.tile(sin_k, (1, N_HEADS))


def _tile_rows(t, want):
    tm = min(want, t)
    assert t % tm == 0
    return tm


def _trunk(x3, p4, conv0, rnn0, pos, pw, paged):
    nb, s, d = x3.shape
    t = nb * s
    x = x3.reshape(t, d)
    p = p4.reshape(DEPTH, t, -1)
    tm = _tile_rows(t, 512)

    gate, u = _rg_in(x, pw["rg_wg"], pw["rg_wx"], tm)
    if paged is None:
        conv_pad = jnp.concatenate(
            [jnp.zeros((nb, SUBLANES - (CONV_WIDTH - 1), D_RNN), F32), conv0], axis=1)
        hg, h_last = _rglru_seq(gate, u, conv_pad, rnn0[:, None, :], pw["rg"], nb, s, min(s, 256), 512)
        h_last = h_last[:, 0, :]
        tabs = _rope_tables(pos, 1)
    else:
        to_tm = lambda a: jnp.transpose(a.reshape(nb, s, -1), (1, 0, 2))
        hg, h_last = _rglru_step(to_tm(gate), to_tm(u), jnp.transpose(conv0, (1, 0, 2)), rnn0, pw["rg"], 512)
        hg = jnp.transpose(hg, (1, 0, 2)).reshape(t, -1)
        tabs = _rope_tables(pos, nb)
    conv_state = u.reshape(nb, s, -1)[:, s - (CONV_WIDTH - 1):, :]
    x = _mm_ln(hg, pw["rg_wout"], x, *pw["ln_mix"][0], tm)
    x = _ffn(jnp.zeros((t // tm,), jnp.int32), x, *pw["ffn"], pw["ln_ffn"][0], tm, pw["ffn"][0].shape[2])
    x = _ple(x, p[0], pw["ple_wg"][0], pw["ple_wp"][0], tm)
    cos_k, sin_k, cos_q, sin_q = tabs
    kv_args = (x, pw["kv_wc"], pw["kv_wr"], pw["kv_wrs"], pw["kv_g"], cos_k, sin_k, tm)

    tq = _tile_rows(t, 256)
    if paged is None:
        ckv, kpe, kv, vt = _kva(*kv_args, attn_operands=True)
        q = _q_proj(x, pw["q_wa"], pw["q_g"], pw["q_wb"], pw["wuk"], cos_q, sin_q, tq, BF16)
        o = _attn_prompt(q, kv, vt, nb, s, tq, 256)
    else:
        cache_ckv, cache_kpe, page_table = paged
        ckv, kpe = _kva(*kv_args)
        q = _q_proj(x, pw["q_wa"], pw["q_g"], pw["q_wb"], pw["wuk"], cos_q, sin_q, tq, F32)
        o = _attn_paged(q.reshape(t // tq, N_HEADS, tq // s, s, QK_WIDTH), ckv.reshape(nb, s, -1),
                        kpe.reshape(nb, s, -1), cache_ckv, jnp.swapaxes(cache_kpe, 1, 2), page_table, 32, 2)
        o = o.reshape(t * N_HEADS, KV_LORA_RANK)
    x = _mla_out(o, pw["wuv"], pw["wo"], x, *pw["ln_mix"][1], tq)
    x = _moe(x, pw["router_t"], *pw["moe"], *pw["ln_ffn"][1], tq, 512,
             pw["moe"][0].shape[2] // 2)
    x = _ple(x, p[1], pw["ple_wg"][1], pw["ple_wp"][1], tm)
    return (x.reshape(nb, s, d), conv_state[None], h_last[None], ckv.reshape(nb, s, -1), kpe.reshape(nb, s, -1))


def kernel(x_prompt, x_sample, p_prompt, p_sample, state_conv, state_rnn, cache_ckv, cache_kpe, page_table, ln_mix_g, ln_mix_b, ln_ffn_g, ln_ffn_b, rg_w_gate, rg_w_x, rg_conv_w, rg_conv_b, rg_w_a, rg_b_a, rg_w_i, rg_b_i, rg_lambda, rg_w_out, mla_w_q_a, mla_q_norm_g, mla_w_q_b, mla_w_o, kv_w_a, kv_norm_g, kv_w_uk, kv_w_uv, ffn_w_gate, ffn_w_up, ffn_w_down, moe_w_router, moe_w_gate, moe_w_up, moe_w_down, ple_w_gate, ple_w_proj):
    w = dict(
        ln_mix_g=ln_mix_g, ln_mix_b=ln_mix_b, ln_ffn_g=ln_ffn_g, ln_ffn_b=ln_ffn_b,
        rg_w_gate=rg_w_gate, rg_w_x=rg_w_x, rg_conv_w=rg_conv_w, rg_conv_b=rg_conv_b, rg_w_a=rg_w_a,
        rg_b_a=rg_b_a, rg_w_i=rg_w_i, rg_b_i=rg_b_i, rg_lambda=rg_lambda, rg_w_out=rg_w_out,
        mla_w_q_a=mla_w_q_a, mla_q_norm_g=mla_q_norm_g, mla_w_q_b=mla_w_q_b, mla_w_o=mla_w_o,
        kv_w_a=kv_w_a, kv_norm_g=kv_norm_g, kv_w_uk=kv_w_uk, kv_w_uv=kv_w_uv,
        ffn_w_gate=ffn_w_gate, ffn_w_up=ffn_w_up, ffn_w_down=ffn_w_down,
        moe_w_router=moe_w_router, moe_w_gate=moe_w_gate, moe_w_up=moe_w_up, moe_w_down=moe_w_down,
        ple_w_gate=ple_w_gate, ple_w_proj=ple_w_proj)
    pw = _prep_weights(w)
    nb, s = x_prompt.shape[:2]
    dec_s = x_sample.shape[1]
    past_len = page_table.shape[1] * cache_ckv.shape[1]
    n_a = state_conv.shape[0]
    assert n_a == 1
    conv0_p = jnp.zeros((nb, CONV_WIDTH - 1, D_RNN), state_conv.dtype)
    rnn0_p = jnp.zeros((nb, D_RNN), state_rnn.dtype)
    y_p, conv_p, rnn_p, ckv_p, kpe_p = _trunk(
        x_prompt, p_prompt, conv0_p, rnn0_p, jnp.arange(s, dtype=F32), pw, None)
    y_s, conv_s, rnn_s, ckv_s, kpe_s = _trunk(
        x_sample, p_sample, state_conv[0], state_rnn[0], past_len + jnp.arange(dec_s, dtype=F32), pw,
        (cache_ckv, cache_kpe, page_table))
    return (y_p, y_s, conv_p, rnn_p, ckv_p, kpe_p, conv_s, rnn_s, ckv_s, kpe_s)
```

```python
import functools

import jax
import jax.numpy as jnp
from jax import lax
from jax.experimental import pallas as pl
from jax.experimental.pallas import tpu as pltpu

F32 = jnp.float32
BF16 = jnp.bfloat16

D_RNN = 1536
RNN_BLOCK = 128
CONV_WIDTH = 4
LRU_C = 8.0
N_HEADS = 16
KV_LORA_RANK = 256
QK_NOPE_DIM = 64
QK_ROPE_DIM = 32
V_HEAD_DIM = 64
ROPE_THETA = 10000.0
SOFTMAX_SCALE = (QK_NOPE_DIM + QK_ROPE_DIM) ** -0.5
N_EXPERTS = 8
LN_EPS = 1e-5
RMS_EPS = 1e-6
DEPTH = 2
DEEPNORM_ALPHA = (2.0 * DEPTH) ** 0.25

QK_WIDTH = KV_LORA_RANK + QK_ROPE_DIM
QK_SCALE = SOFTMAX_SCALE * 1.4426950408889634

SUBLANES = 8
LANES = 128
VMEM_LIMIT = 48 * 1024 * 1024


def _params(*sem):
    return pltpu.CompilerParams(dimension_semantics=sem, vmem_limit_bytes=VMEM_LIMIT)


def _dot(a, b):
    return jnp.dot(a, b, preferred_element_type=F32)


def _dot_nt(a, b):
    return lax.dot_general(a, b, (((1,), (1,)), ((), ())), preferred_element_type=F32)


def _layer_norm(z, g, b):
    mu = jnp.mean(z, -1, keepdims=True)
    d = z - mu
    var = jnp.mean(d * d, -1, keepdims=True)
    return d * lax.rsqrt(var + LN_EPS) * g + b


def _rms_norm(z, g):
    return z * lax.rsqrt(jnp.mean(z * z, -1, keepdims=True) + RMS_EPS) * g


def _full(shape):
    n = len(shape)
    return pl.BlockSpec(shape, lambda *_: (0,) * n)


def _rg_in_kernel(x_ref, wg_ref, wx_ref, gate_ref, u_ref):
    xb = x_ref[...].astype(BF16)
    gate_ref[...] = jax.nn.gelu(_dot(xb, wg_ref[...]))
    u_ref[...] = _dot(xb, wx_ref[...])


def _rg_in(x, wg, wx, tm):
    t, d = x.shape
    n = wg.shape[1]
    return pl.pallas_call(
        _rg_in_kernel,
        grid=(t // tm,),
        in_specs=[pl.BlockSpec((tm, d), lambda i: (i, 0)), _full((d, n)), _full((d, n))],
        out_specs=[pl.BlockSpec((tm, n), lambda i: (i, 0))] * 2,
        out_shape=[jax.ShapeDtypeStruct((t, n), F32)] * 2,
        compiler_params=_params("parallel"),
        name="rg_in",
    )(x, wg, wx)


def _rglru_gates(conv, wa_ref, ba, wi_ref, bi, lam):
    nblk = conv.shape[1] // RNN_BLOCK
    cb = conv.astype(BF16)
    ra = jnp.concatenate(
        [_dot(cb[:, n * RNN_BLOCK:(n + 1) * RNN_BLOCK], wa_ref[n]) for n in range(nblk)], axis=1)
    ia = jnp.concatenate(
        [_dot(cb[:, n * RNN_BLOCK:(n + 1) * RNN_BLOCK], wi_ref[n]) for n in range(nblk)], axis=1)
    r = jax.nn.sigmoid(ra + ba)
    i = jax.nn.sigmoid(ia + bi)
    z = -lam
    softplus = jnp.maximum(z, 0.0) + jnp.log1p(jnp.exp(-jnp.abs(z)))
    log_a = -LRU_C * r * softplus
    a = jnp.exp(log_a)
    v = 1.0 - a * a
    b = jnp.where(v > 0.0, v * lax.rsqrt(v), 0.0) * (i * conv)
    return a, b


def _rglru_seq_kernel(gate_ref, u_ref, conv0_ref, h0_ref, cw_ref, cb_ref, wa_ref, ba_ref, wi_ref, bi_ref,
                      lam_ref, hg_ref, hlast_ref, ubuf, hcar, abuf, bbuf):
    c = pl.program_id(2)
    tc, db = u_ref.shape
    halo = SUBLANES

    @pl.when(c == 0)
    def _():
        ubuf[0:halo, :] = conv0_ref[0]
        hcar[...] = jnp.broadcast_to(h0_ref[0], hcar.shape)

    ubuf[halo:halo + tc, :] = u_ref[...]
    cw = cw_ref[...]
    conv = cb_ref[...]
    for k in range(CONV_WIDTH):
        off = halo - (CONV_WIDTH - 1) + k
        conv = conv + ubuf[off:off + tc, :] * cw[k:k + 1, :]
    ubuf[0:halo, :] = ubuf[tc:tc + halo, :]

    a, b = _rglru_gates(conv, wa_ref, ba_ref[...], wi_ref, bi_ref[...], lam_ref[...])

    groups = tc // SUBLANES
    a = a.reshape(groups, SUBLANES, db)
    b = b.reshape(groups, SUBLANES, db)
    row = lax.broadcasted_iota(jnp.int32, a.shape, 1)
    shift = 1
    while shift < SUBLANES:
        a_prev = pltpu.roll(a, shift, 1)
        b_prev = pltpu.roll(b, shift, 1)
        keep = row >= shift
        b = jnp.where(keep, a * b_prev + b, b)
        a = jnp.where(keep, a * a_prev, a)
        shift *= 2
    abuf[...] = a.reshape(tc, db)
    bbuf[...] = b.reshape(tc, db)

    def group(g, h):
        off = pl.multiple_of(g * SUBLANES, SUBLANES)
        hb = abuf[pl.ds(off, SUBLANES), :] * h + bbuf[pl.ds(off, SUBLANES), :]
        bbuf[pl.ds(off, SUBLANES), :] = hb
        return jnp.broadcast_to(hb[SUBLANES - 1:SUBLANES, :], hb.shape)

    h = lax.fori_loop(0, tc // SUBLANES, group, hcar[...])
    hcar[...] = h
    hg_ref[...] = (bbuf[...] * gate_ref[...]).astype(BF16)

    @pl.when(c == pl.num_programs(2) - 1)
    def _():
        hlast_ref[0] = h[0:1, :]


def _rglru_seq(gate, u, conv0, h0, rw, nb, s, tc, db):
    t, d = u.shape
    nc = s // tc
    kb = db // RNN_BLOCK
    row = lambda b, j, c: (b * nc + c, j)
    vec = pl.BlockSpec((1, db), lambda b, j, c: (0, j))
    blk = pl.BlockSpec((kb, RNN_BLOCK, RNN_BLOCK), lambda b, j, c: (j, 0, 0))
    return pl.pallas_call(
        _rglru_seq_kernel,
        grid=(nb, d // db, nc),
        in_specs=[
            pl.BlockSpec((tc, db), row), pl.BlockSpec((tc, db), row),
            pl.BlockSpec((1, SUBLANES, db), lambda b, j, c: (b, 0, j)),
            pl.BlockSpec((1, 1, db), lambda b, j, c: (b, 0, j)),
            pl.BlockSpec((CONV_WIDTH, db), lambda b, j, c: (0, j)), vec, blk, vec, blk, vec, vec,
        ],
        out_specs=[pl.BlockSpec((tc, db), row), pl.BlockSpec((1, 1, db), lambda b, j, c: (b, 0, j))],
        out_shape=[jax.ShapeDtypeStruct((t, d), BF16), jax.ShapeDtypeStruct((nb, 1, d), F32)],
        scratch_shapes=[pltpu.VMEM((tc + SUBLANES, db), F32), pltpu.VMEM((SUBLANES, db), F32),
                        pltpu.VMEM((tc, db), F32), pltpu.VMEM((tc, db), F32)],
        compiler_params=_params("parallel", "parallel", "arbitrary"),
        name="rglru_seq",
    )(gate, u, conv0, h0, rw["conv_w"], rw["conv_b"], rw["w_a"], rw["b_a"], rw["w_i"], rw["b_i"], rw["lam"])


def _rglru_step_kernel(gate_ref, u_ref, conv0_ref, h0_ref, cw_ref, cb_ref, wa_ref, ba_ref, wi_ref, bi_ref,
                       lam_ref, hg_ref, hlast_ref):
    s, nb, db = u_ref.shape
    ue = jnp.concatenate([conv0_ref[...], u_ref[...]], axis=0)
    cw = cw_ref[...]
    conv = cb_ref[...][None]
    for k in range(CONV_WIDTH):
        conv = conv + ue[k:k + s] * cw[k:k + 1, :][None]
    a, b = _rglru_gates(conv.reshape(s * nb, db), wa_ref, ba_ref[...], wi_ref, bi_ref[...], lam_ref[...])
    a = a.reshape(s, nb, db)
    b = b.reshape(s, nb, db)
    h = h0_ref[...]
    for t in range(s):
        h = a[t] * h + b[t]
        hg_ref[t] = (h * gate_ref[t]).astype(BF16)
    hlast_ref[...] = h


def _rglru_step(gate, u, conv0, h0, rw, db):
    s, nb, d = u.shape
    kb = db // RNN_BLOCK
    cube = pl.BlockSpec((s, nb, db), lambda j: (0, 0, j))
    vec = pl.BlockSpec((1, db), lambda j: (0, j))
    blk = pl.BlockSpec((kb, RNN_BLOCK, RNN_BLOCK), lambda j: (j, 0, 0))
    return pl.pallas_call(
        _rglru_step_kernel,
        grid=(d // db,),
        in_specs=[cube, cube, pl.BlockSpec((CONV_WIDTH - 1, nb, db), lambda j: (0, 0, j)),
                  pl.BlockSpec((nb, db), lambda j: (0, j)),
                  pl.BlockSpec((CONV_WIDTH, db), lambda j: (0, j)), vec, blk, vec, blk, vec, vec],
        out_specs=[cube, pl.BlockSpec((nb, db), lambda j: (0, j))],
        out_shape=[jax.ShapeDtypeStruct((s, nb, d), BF16), jax.ShapeDtypeStruct((nb, d), F32)],
        compiler_params=_params("parallel"),
        name="rglru_step",
    )(gate, u, conv0, h0, rw["conv_w"], rw["conv_b"], rw["w_a"], rw["b_a"], rw["w_i"], rw["b_i"], rw["lam"])


def _mm_ln_kernel(x_ref, w_ref, res_ref, g_ref, b_ref, o_ref):
    y = _dot(x_ref[...], w_ref[...])
    o_ref[...] = _layer_norm(DEEPNORM_ALPHA * res_ref[...] + y, g_ref[...], b_ref[...])


def _mm_ln(x, w, res, g, b, tm):
    t, k = x.shape
    n = w.shape[1]
    return pl.pallas_call(
        _mm_ln_kernel,
        grid=(t // tm,),
        in_specs=[pl.BlockSpec((tm, k), lambda i: (i, 0)), _full((k, n)),
                  pl.BlockSpec((tm, n), lambda i: (i, 0)), _full((1, n)), _full((1, n))],
        out_specs=pl.BlockSpec((tm, n), lambda i: (i, 0)),
        out_shape=jax.ShapeDtypeStruct((t, n), F32),
        compiler_params=_params("parallel"),
        name="mm_ln",
    )(x, w, res, g, b)


def _ffn_kernel(te_ref, x_ref, wg_ref, wu_ref, wd_ref, *rest, post_norm):
    del te_ref
    if post_norm:
        g_ref, b_ref, o_ref, xb_ref, acc_ref = rest
    else:
        o_ref, xb_ref, acc_ref = rest
    c = pl.program_id(1)

    @pl.when(c == 0)
    def _():
        xb_ref[...] = x_ref[...].astype(BF16)
        acc_ref[...] = jnp.zeros_like(acc_ref)

    xb = xb_ref[...]
    h = (jax.nn.silu(_dot(xb, wg_ref[0])) * _dot(xb, wu_ref[0])).astype(BF16)
    acc_ref[...] += _dot(h, wd_ref[0])

    @pl.when(c == pl.num_programs(1) - 1)
    def _():
        if post_norm:
            o_ref[...] = _layer_norm(DEEPNORM_ALPHA * x_ref[...] + acc_ref[...], g_ref[...], b_ref[...])
        else:
            o_ref[...] = acc_ref[...]


def _ffn(tile_expert, x, wg, wu, wd, norm, tm, tf):
    r, d = x.shape
    f = wg.shape[2]
    once = dict(pipeline_mode=pl.Buffered(1)) if (wg.shape[0] == 1 and tf == f) else {}
    in_specs = [
        pl.BlockSpec((tm, d), lambda i, c, te: (i, 0)),
        pl.BlockSpec((1, d, tf), lambda i, c, te: (te[i], 0, c), **once),
        pl.BlockSpec((1, d, tf), lambda i, c, te: (te[i], 0, c), **once),
        pl.BlockSpec((1, tf, d), lambda i, c, te: (te[i], c, 0), **once),
    ]
    args = [x, wg, wu, wd]
    if norm is not None:
        in_specs += [pl.BlockSpec((1, d), lambda i, c, te: (0, 0))] * 2
        args += list(norm)
    return pl.pallas_call(
        functools.partial(_ffn_kernel, post_norm=norm is not None),
        grid_spec=pltpu.PrefetchScalarGridSpec(
            num_scalar_prefetch=1,
            grid=(r // tm, f // tf),
            in_specs=in_specs,
            out_specs=pl.BlockSpec((tm, d), lambda i, c, te: (i, 0)),
            scratch_shapes=[pltpu.VMEM((tm, d), BF16), pltpu.VMEM((tm, d), F32)],
        ),
        out_shape=jax.ShapeDtypeStruct((r, d), F32),
        compiler_params=_params("parallel", "arbitrary"),
        name="ffn",
    )(tile_expert, *args)


def _ple_kernel(x_ref, p_ref, wg_ref, wp_ref, o_ref):
    x = x_ref[...]
    a = _dot(x.astype(BF16), wg_ref[...])
    c = _dot(p_ref[...].astype(BF16), wp_ref[...])
    o_ref[...] = x + jax.nn.sigmoid(a) * c


def _ple(x, p, wg, wp, tm):
    t, d = x.shape
    dp = p.shape[1]
    return pl.pallas_call(
        _ple_kernel,
        grid=(t // tm,),
        in_specs=[pl.BlockSpec((tm, d), lambda i: (i, 0)), pl.BlockSpec((tm, dp), lambda i: (i, 0)),
                  _full((d, d)), _full((dp, d))],
        out_specs=pl.BlockSpec((tm, d), lambda i: (i, 0)),
        out_shape=jax.ShapeDtypeStruct((t, d), F32),
        compiler_params=_params("parallel"),
        name="ple",
    )(x, p, wg, wp)


def _kva_kernel(x_ref, wc_ref, wr_ref, wrs_ref, g_ref, cos_ref, sin_ref, *rest, attn_operands):
    xb = x_ref[...].astype(BF16)
    ckv = _rms_norm(_dot(xb, wc_ref[...]), g_ref[...])
    kpe = _dot(xb, wr_ref[...]) * cos_ref[...] + _dot(xb, wrs_ref[...]) * sin_ref[...]
    if not attn_operands:
        ckv_ref, kpe_ref = rest
        ckv_ref[...] = ckv
        kpe_ref[...] = kpe
        return
    wk_ref, place_ref, ckv_ref, kpe_ref, kh_ref, vt_ref = rest
    ckv_ref[...] = ckv
    kpe_ref[...] = kpe
    nope = _dot(ckv.astype(BF16), wk_ref[...])
    rope = _dot(kpe.astype(BF16), place_ref[...])
    for h in range(N_HEADS):
        kh_ref[h] = (nope[:, h * LANES:(h + 1) * LANES] + rope).astype(BF16)
    vt_ref[...] = ckv.T.astype(BF16)


def _kva(x, wc, wr, wrs, g, cos, sin, tm, heads=None):
    t, d = x.shape
    nper = cos.shape[0] // tm
    tab = pl.BlockSpec((tm, QK_ROPE_DIM), lambda i: (i % nper, 0))
    out = lambda n: pl.BlockSpec((tm, n), lambda i: (i, 0))
    in_specs = [pl.BlockSpec((tm, d), lambda i: (i, 0)), _full(wc.shape), _full(wr.shape), _full(wrs.shape),
                _full(g.shape), tab, tab]
    args = [x, wc, wr, wrs, g, cos, sin]
    out_specs = [out(KV_LORA_RANK), out(QK_ROPE_DIM)]
    out_shape = [jax.ShapeDtypeStruct((t, KV_LORA_RANK), F32), jax.ShapeDtypeStruct((t, QK_ROPE_DIM), F32)]
    if heads is not None:
        in_specs += [_full(w.shape) for w in heads]
        args += list(heads)
        out_specs += [pl.BlockSpec((N_HEADS, tm, LANES), lambda i: (0, i, 0)),
                      pl.BlockSpec((KV_LORA_RANK, tm), lambda i: (0, i))]
        out_shape += [jax.ShapeDtypeStruct((N_HEADS, t, LANES), BF16), jax.ShapeDtypeStruct((KV_LORA_RANK, t), BF16)]
    return pl.pallas_call(
        functools.partial(_kva_kernel, attn_operands=heads is not None),
        grid=(t // tm,),
        in_specs=in_specs,
        out_specs=out_specs,
        out_shape=out_shape,
        compiler_params=_params("parallel"),
        name="kv_latent",
    )(*args)


def _q_heads_kernel(x_ref, wqa_ref, qg_ref, wqb_ref, cos_ref, sin_ref, q_ref):
    tm = x_ref.shape[0]
    wide = N_HEADS * LANES
    cq = _rms_norm(_dot(x_ref[...].astype(BF16), wqa_ref[...]), qg_ref[...])
    q = _dot(cq.astype(BF16), wqb_ref[...])
    q = (q[:, :wide] * cos_ref[...] + q[:, wide:] * sin_ref[...]) * QK_SCALE
    for h in range(N_HEADS):
        q_ref[h * tm:(h + 1) * tm, :] = q[:, h * LANES:(h + 1) * LANES].astype(BF16)


def _q_heads(x, wqa, qg, wqb, cos, sin, tm):
    t, d = x.shape
    nper = cos.shape[0] // tm
    tab = pl.BlockSpec((tm, N_HEADS * LANES), lambda i: (i % nper, 0))
    return pl.pallas_call(
        _q_heads_kernel,
        grid=(t // tm,),
        in_specs=[pl.BlockSpec((tm, d), lambda i: (i, 0)), _full(wqa.shape), _full(qg.shape), _full(wqb.shape),
                  tab, tab],
        out_specs=pl.BlockSpec((N_HEADS * tm, LANES), lambda i: (i, 0)),
        out_shape=jax.ShapeDtypeStruct((N_HEADS * t, LANES), BF16),
        compiler_params=_params("parallel"),
        name="q_heads",
    )(x, wqa, qg, wqb, cos, sin)


def _q_kernel(x_ref, wqa_ref, qg_ref, wqb_ref, wuk_ref, cos_ref, sin_ref, q_ref):
    tm = x_ref.shape[0]
    nope = N_HEADS * QK_NOPE_DIM
    pe = N_HEADS * QK_ROPE_DIM
    c = KV_LORA_RANK
    r = QK_ROPE_DIM
    cq = _rms_norm(_dot(x_ref[...].astype(BF16), wqa_ref[...]), qg_ref[...])
    q = _dot(cq.astype(BF16), wqb_ref[...])
    q_pe = (q[:, nope:nope + pe] * cos_ref[...] + q[:, nope + pe:nope + 2 * pe] * sin_ref[...]) * QK_SCALE
    qn = q[:, :nope].astype(BF16)
    pair = 2 * QK_NOPE_DIM
    for j in range(N_HEADS // 2):
        ql = _dot(qn[:, j * pair:(j + 1) * pair], wuk_ref[j]) * QK_SCALE
        for k in range(2):
            h = 2 * j + k
            q_ref[h * tm:(h + 1) * tm, :c] = ql[:, k * c:(k + 1) * c].astype(q_ref.dtype)
            q_ref[h * tm:(h + 1) * tm, c:] = q_pe[:, h * r:(h + 1) * r].astype(q_ref.dtype)


def _q_proj(x, wqa, qg, wqb, wuk, cos, sin, tm, dtype):
    t, d = x.shape
    nper = cos.shape[0] // tm
    pe = N_HEADS * QK_ROPE_DIM
    tab = pl.BlockSpec((tm, pe), lambda i: (i % nper, 0))
    return pl.pallas_call(
        _q_kernel,
        grid=(t // tm,),
        in_specs=[pl.BlockSpec((tm, d), lambda i: (i, 0)), _full(wqa.shape), _full(qg.shape), _full(wqb.shape),
                  _full(wuk.shape), tab, tab],
        out_specs=pl.BlockSpec((N_HEADS * tm, QK_WIDTH), lambda i: (i, 0)),
        out_shape=jax.ShapeDtypeStruct((N_HEADS * t, QK_WIDTH), dtype),
        compiler_params=_params("parallel"),
        name="q_proj",
    )(x, wqa, qg, wqb, wuk, cos, sin)


def _widen(stat, width):
    return jnp.concatenate([stat] * (width // LANES), axis=1)


def _softmax_step(s, kv, m_s, l_s, acc):
    m_prev = m_s[...]
    m_new = jnp.maximum(m_prev, jnp.max(s, -1, keepdims=True))
    corr = jnp.exp2(m_prev - m_new)
    p = jnp.exp2(s - _widen(m_new, s.shape[1]))
    l_s[...] = l_s[...] * corr + jnp.sum(p, -1, keepdims=True)
    acc[...] = acc[...] * _widen(corr, KV_LORA_RANK) + _dot(p.astype(BF16), kv)
    m_s[...] = m_new


def _attn_kernel(qi_ref, kj_ref, q_ref, kh_ref, vt_ref, o_ref, m_s, l_s, acc, *, tq, tk):
    p = pl.program_id(1)
    i = qi_ref[p]
    j = kj_ref[p]

    @pl.when(j == 0)
    def _():
        m_s[...] = jnp.full_like(m_s, -jnp.inf)
        l_s[...] = jnp.zeros_like(l_s)
        acc[...] = jnp.zeros_like(acc)

    def step(masked):
        st = jnp.concatenate(
            [_dot_nt(kh_ref[h], q_ref[h * tq:(h + 1) * tq, :]) for h in range(N_HEADS)], axis=1)
        if masked:
            k_pos = j * tk + lax.broadcasted_iota(jnp.int32, st.shape, 0)
            q_pos = i * tq + (lax.broadcasted_iota(jnp.int32, st.shape, 1) & (tq - 1))
            st = jnp.where(k_pos <= q_pos, st, -jnp.inf)
        m_prev = m_s[...]
        m_new = jnp.maximum(m_prev, jnp.max(st, 0, keepdims=True))
        corr = jnp.exp2(m_prev - m_new)
        pt = jnp.exp2(st - m_new)
        l_s[...] = l_s[...] * corr + jnp.sum(pt, 0, keepdims=True)
        acc[...] = acc[...] * corr + _dot(vt_ref[...], pt.astype(BF16))
        m_s[...] = m_new

    crosses_diagonal = j * tk + tk - 1 > i * tq

    @pl.when(jnp.logical_not(crosses_diagonal))
    def _():
        step(False)

    @pl.when(crosses_diagonal)
    def _():
        step(True)

    @pl.when(j == (i * tq + tq - 1) // tk)
    def _():
        o_ref[...] = (acc[...] * (1.0 / l_s[...])).T.astype(o_ref.dtype)


def _attn_prompt(q, kh, vt, nb, s, tq, tk):
    nq, nk = s // tq, s // tk
    assert tq & (tq - 1) == 0
    rows = N_HEADS * tq
    pairs = [(i, j) for i in range(nq) for j in range((i * tq + tq - 1) // tk + 1)]
    qi = jnp.asarray([i for i, _ in pairs], jnp.int32)
    kj = jnp.asarray([j for _, j in pairs], jnp.int32)
    q_row = lambda b, p, qi, kj: (b * nq + qi[p], 0)
    return pl.pallas_call(
        functools.partial(_attn_kernel, tq=tq, tk=tk),
        grid_spec=pltpu.PrefetchScalarGridSpec(
            num_scalar_prefetch=2,
            grid=(nb, len(pairs)),
            in_specs=[pl.BlockSpec((rows, LANES), q_row),
                      pl.BlockSpec((N_HEADS, tk, LANES), lambda b, p, qi, kj: (0, b * nk + kj[p], 0)),
                      pl.BlockSpec((KV_LORA_RANK, tk), lambda b, p, qi, kj: (0, b * nk + kj[p]))],
            out_specs=pl.BlockSpec((rows, KV_LORA_RANK), q_row),
            scratch_shapes=[pltpu.VMEM((1, rows), F32), pltpu.VMEM((1, rows), F32),
                            pltpu.VMEM((KV_LORA_RANK, rows), F32)],
        ),
        out_shape=jax.ShapeDtypeStruct((q.shape[0], KV_LORA_RANK), BF16),
        compiler_params=_params("parallel", "arbitrary"),
        name="attn_prompt",
    )(qi, kj, q, kh, vt)


def _attn_paged_kernel(pt_ref, q_ref, cnew_ref, knew_ref, ckv_hbm, kpe_hbm, o_ref, m_s, l_s, acc, kbuf, kpbuf,
                       ck_in, kp_in, sem, *, seqs, pages, s_new):
    g = pl.program_id(0)
    c = pl.program_id(1)
    nc = pl.num_programs(1)
    n_pages = nc * pages
    step = g * nc + c
    rows = N_HEADS * s_new
    page = ck_in.shape[2]

    def page_copies(group, chunk, slot):
        copies = []
        for a in range(seqs):
            for k in range(pages):
                pid = pt_ref[(group * seqs + a) * n_pages + chunk * pages + k]
                copies.append(pltpu.make_async_copy(ckv_hbm.at[pid], ck_in.at[slot, a * pages + k], sem.at[slot, 0]))
                copies.append(pltpu.make_async_copy(kpe_hbm.at[pid], kp_in.at[slot, a * pages + k], sem.at[slot, 1]))
        return copies

    @pl.when(step == 0)
    def _():
        for cp in page_copies(0, 0, 0):
            cp.start()

    @pl.when(step + 1 < pl.num_programs(0) * nc)
    def _():
        nxt = step + 1
        for cp in page_copies(nxt // nc, nxt % nc, nxt % 2):
            cp.start()

    slot = step % 2
    pltpu.make_async_copy(ckv_hbm.at[pl.ds(0, seqs * pages)], ck_in.at[slot], sem.at[slot, 0]).wait()
    pltpu.make_async_copy(kpe_hbm.at[pl.ds(0, seqs * pages)], kp_in.at[slot], sem.at[slot, 1]).wait()
    ck_refs = [ck_in.at[slot, i] for i in range(seqs * pages)]
    kp_refs = [kp_in.at[slot, i] for i in range(seqs * pages)]

    qs = [q_ref[:, a].reshape(rows, QK_WIDTH).astype(BF16) for a in range(seqs)]
    qls = [q[:, :KV_LORA_RANK] for q in qs]
    qps = [q[:, KV_LORA_RANK:] for q in qs]

    @pl.when(c == 0)
    def _():
        for a in range(seqs):
            pad = 2 * SUBLANES - s_new
            cn = jnp.concatenate([cnew_ref[a], jnp.zeros((pad, KV_LORA_RANK), F32)], axis=0).astype(BF16)
            kn = jnp.concatenate([knew_ref[a], jnp.zeros((pad, QK_ROPE_DIM), F32)], axis=0).astype(BF16)
            s = _dot_nt(qls[a], cn) + _dot_nt(qps[a], kn)
            tok = lax.broadcasted_iota(jnp.int32, s.shape, 0) & (s_new - 1)
            key = lax.broadcasted_iota(jnp.int32, s.shape, 1)
            s = jnp.where(key <= tok, s, -jnp.inf)
            m = jnp.max(s, -1, keepdims=True)
            p = jnp.exp2(s - m)
            m_s[a] = jnp.broadcast_to(m, m_s.shape[1:])
            l_s[a] = jnp.broadcast_to(jnp.sum(p, -1, keepdims=True), l_s.shape[1:])
            acc[a] = _dot(p.astype(BF16), cn)

    for a in range(seqs):
        for k in range(pages):
            kbuf[a, k * page:(k + 1) * page, :] = ck_refs[a * pages + k][...].astype(BF16)
            kpbuf[a, :, k * page:(k + 1) * page] = kp_refs[a * pages + k][...].astype(BF16)
        kv = kbuf[a]
        s = _dot_nt(qls[a], kv) + _dot(qps[a], kpbuf[a])
        _softmax_step(s, kv, m_s.at[a], l_s.at[a], acc.at[a])

    @pl.when(c == pl.num_programs(1) - 1)
    def _():
        for a in range(seqs):
            o = acc[a] * _widen(1.0 / l_s[a], KV_LORA_RANK)
            o_ref[:, a] = o.reshape(N_HEADS, s_new, KV_LORA_RANK)


def _attn_paged(q, cnew, knew, cache_ckv, cache_kpe_t, page_table, pages, seqs):
    n_tiles, _, per_tile, s_new, _ = q.shape
    nb = n_tiles * per_tile
    assert s_new == SUBLANES and per_tile % seqs == 0
    groups = per_tile // seqs
    n_pages = page_table.shape[1]
    page = cache_ckv.shape[1]
    c = KV_LORA_RANK
    r = QK_ROPE_DIM
    per_g = lambda g, j, pt: (g, 0, 0)
    q_blk = lambda w: pl.BlockSpec((None, N_HEADS, seqs, s_new, w),
                                   lambda g, j, pt: (g // groups, 0, g % groups, 0, 0))

    in_specs = [q_blk(QK_WIDTH), pl.BlockSpec((seqs, s_new, c), per_g), pl.BlockSpec((seqs, s_new, r), per_g),
                pl.BlockSpec(memory_space=pl.ANY), pl.BlockSpec(memory_space=pl.ANY)]
    rows = N_HEADS * s_new
    assert n_pages % pages == 0
    return pl.pallas_call(
        functools.partial(_attn_paged_kernel, seqs=seqs, pages=pages, s_new=s_new),
        grid_spec=pltpu.PrefetchScalarGridSpec(
            num_scalar_prefetch=1,
            grid=(nb // seqs, n_pages // pages),
            in_specs=in_specs,
            out_specs=q_blk(c),
            scratch_shapes=[pltpu.VMEM((seqs, rows, LANES), F32), pltpu.VMEM((seqs, rows, LANES), F32),
                            pltpu.VMEM((seqs, rows, c), F32), pltpu.VMEM((seqs, pages * page, c), BF16),
                            pltpu.VMEM((seqs, r, pages * page), BF16),
                            pltpu.VMEM((2, seqs * pages, page, c), F32), pltpu.VMEM((2, seqs * pages, r, page), F32),
                            pltpu.SemaphoreType.DMA((2, 2))],
        ),
        out_shape=jax.ShapeDtypeStruct(q.shape[:-1] + (c,), F32),
        compiler_params=_params("arbitrary", "arbitrary"),
        name="attn_paged",
    )(page_table.reshape(-1), q, cnew, knew, cache_ckv, cache_kpe_t)


def _mla_out_kernel(o_ref, wuv_ref, wo_ref, res_ref, g_ref, b_ref, out_ref):
    tm = res_ref.shape[0]
    head = lambda h: o_ref[h * tm:(h + 1) * tm, :].astype(BF16)
    v = jnp.concatenate(
        [_dot(jnp.concatenate([head(2 * j), head(2 * j + 1)], axis=1), wuv_ref[j]) for j in range(N_HEADS // 2)],
        axis=1).astype(BF16)
    y = _dot(v, wo_ref[...])
    out_ref[...] = _layer_norm(DEEPNORM_ALPHA * res_ref[...] + y, g_ref[...], b_ref[...])


def _mla_out(o, wuv, wo, res, g, b, tm):
    t, n = res.shape
    return pl.pallas_call(
        _mla_out_kernel,
        grid=(t // tm,),
        in_specs=[pl.BlockSpec((N_HEADS * tm, o.shape[1]), lambda i: (i, 0)), _full(wuv.shape), _full(wo.shape),
                  pl.BlockSpec((tm, n), lambda i: (i, 0)), _full((1, n)), _full((1, n))],
        out_specs=pl.BlockSpec((tm, n), lambda i: (i, 0)),
        out_shape=jax.ShapeDtypeStruct((t, n), F32),
        compiler_params=_params("parallel"),
        name="mla_out",
    )(o, wuv, wo, res, g, b)


def _router_kernel(x_ref, w_ref, idx_ref, wt_ref):
    x = x_ref[...]
    w = w_ref[...]
    xh = x.astype(BF16)
    xl = (x - xh.astype(F32)).astype(BF16)
    wh = w.astype(BF16)
    wl = (w - wh.astype(F32)).astype(BF16)
    lg = (_dot_nt(wh, xh) + _dot_nt(wh, xl) + _dot_nt(wl, xh))[:N_EXPERTS]
    e = lax.broadcasted_iota(jnp.int32, lg.shape, 0).astype(F32)
    none = float(N_EXPERTS)
    v1 = jnp.max(lg, 0, keepdims=True)
    i1 = jnp.min(jnp.where(lg == v1, e, none), 0, keepdims=True)
    lg2 = jnp.where(e == i1, -jnp.inf, lg)
    v2 = jnp.max(lg2, 0, keepdims=True)
    i2 = jnp.min(jnp.where(lg2 == v2, e, none), 0, keepdims=True)
    ex = jnp.exp(v2 - v1)
    den = 1.0 + ex
    idx_ref[...] = jnp.concatenate([i1, i2], axis=0).astype(jnp.int32)
    wt_ref[...] = jnp.concatenate([1.0 / den, ex / den], axis=0)


def _router(x, w_t, tm):
    t, d = x.shape
    return pl.pallas_call(
        _router_kernel,
        grid=(t // tm,),
        in_specs=[pl.BlockSpec((tm, d), lambda i: (i, 0)), _full(w_t.shape)],
        out_specs=[pl.BlockSpec((2, tm), lambda i: (0, i))] * 2,
        out_shape=[jax.ShapeDtypeStruct((2, t), jnp.int32), jax.ShapeDtypeStruct((2, t), F32)],
        compiler_params=_params("parallel"),
        name="router",
    )(x, w_t)


def _dispatch_kernel(pos_ref, fill_ref, x_ref, xs_hbm, zeros, sem, fill_sem, *, rows, tile):
    i = pl.program_id(0)
    t = pl.num_programs(0) * rows
    n_fill = fill_ref.shape[0] // 2
    fill = lambda k: pltpu.make_async_copy(
        zeros, xs_hbm.at[pl.ds(pl.multiple_of(fill_ref[2 * k + 1], tile), tile), :], fill_sem)

    @pl.when(i == 0)
    def _():
        zeros[...] = jnp.zeros_like(zeros)
        for k in range(n_fill):
            @pl.when(fill_ref[2 * k] == 1)
            def _(k=k):
                fill(k).start()
        for k in range(n_fill):
            @pl.when(fill_ref[2 * k] == 1)
            def _(k=k):
                fill(k).wait()

    base = i * rows

    def issue(r, carry):
        for k in range(2):
            slot = pos_ref[k * t + base + r]
            pltpu.make_async_copy(x_ref.at[pl.ds(r, 1), :], xs_hbm.at[pl.ds(slot, 1), :], sem.at[k]).start()
        return carry

    lax.fori_loop(0, rows, issue, 0, unroll=4)
    for k in range(2):
        pltpu.make_async_copy(x_ref, xs_hbm.at[pl.ds(0, rows), :], sem.at[k]).wait()


def _dispatch(pos, fill, x, n_slots, rows, tile):
    t, d = x.shape
    return pl.pallas_call(
        functools.partial(_dispatch_kernel, rows=rows, tile=tile),
        grid_spec=pltpu.PrefetchScalarGridSpec(
            num_scalar_prefetch=2,
            grid=(t // rows,),
            in_specs=[pl.BlockSpec((rows, d), lambda i, pos, fill: (i, 0))],
            out_specs=pl.BlockSpec(memory_space=pl.ANY),
            scratch_shapes=[pltpu.VMEM((tile, d), x.dtype), pltpu.SemaphoreType.DMA((2,)),
                            pltpu.SemaphoreType.DMA],
        ),
        out_shape=jax.ShapeDtypeStruct((n_slots, d), x.dtype),
        compiler_params=_params("arbitrary"),
        name="moe_dispatch",
    )(pos, fill, x)


def _combine_kernel(pos_ref, y_hbm, x_ref, w1_ref, w2_ref, g_ref, b_ref, o_ref, y1, y2, sem):
    rows = o_ref.shape[0]
    i = pl.program_id(0)
    n = pl.num_programs(0)
    t = n * rows

    def gather(step, slot):
        base = step * rows

        def issue(r, carry):
            p1 = pos_ref[base + r]
            p2 = pos_ref[t + base + r]
            pltpu.make_async_copy(y_hbm.at[pl.ds(p1, 1), :], y1.at[slot, pl.ds(r, 1), :], sem.at[slot, 0]).start()
            pltpu.make_async_copy(y_hbm.at[pl.ds(p2, 1), :], y2.at[slot, pl.ds(r, 1), :], sem.at[slot, 1]).start()
            return carry

        lax.fori_loop(0, rows, issue, 0, unroll=8)

    @pl.when(i == 0)
    def _():
        gather(0, 0)

    @pl.when(i + 1 < n)
    def _():
        gather(i + 1, (i + 1) % 2)

    slot = i % 2
    pltpu.make_async_copy(y_hbm.at[pl.ds(0, rows), :], y1.at[slot], sem.at[slot, 0]).wait()
    pltpu.make_async_copy(y_hbm.at[pl.ds(0, rows), :], y2.at[slot], sem.at[slot, 1]).wait()
    ff = w1_ref[...] * y1[slot] + w2_ref[...] * y2[slot]
    o_ref[...] = _layer_norm(DEEPNORM_ALPHA * x_ref[...] + ff, g_ref[...], b_ref[...])


def _combine(pos, y, x, w1, w2, g, b, rows):
    t, d = x.shape
    row = pl.BlockSpec((rows, d), lambda i, p: (i, 0))
    col = pl.BlockSpec((rows, 1), lambda i, p: (i, 0))
    vec = pl.BlockSpec((1, d), lambda i, p: (0, 0))
    return pl.pallas_call(
        _combine_kernel,
        grid_spec=pltpu.PrefetchScalarGridSpec(
            num_scalar_prefetch=1,
            grid=(t // rows,),
            in_specs=[pl.BlockSpec(memory_space=pl.ANY), row, col, col, vec, vec],
            out_specs=row,
            scratch_shapes=[pltpu.VMEM((2, rows, d), F32), pltpu.VMEM((2, rows, d), F32),
                            pltpu.SemaphoreType.DMA((2, 2))],
        ),
        out_shape=jax.ShapeDtypeStruct((t, d), F32),
        compiler_params=_params("arbitrary"),
        name="moe_combine",
    )(pos, y, x, w1, w2, g, b)


def _moe(x, w_router_t, wg, wu, wd, g, b, tm, tile, tf):
    t, d = x.shape
    idx, wts = _router(x, w_router_t, tm)
    e_flat = idx.reshape(-1)
    onehot = (e_flat[:, None] == jnp.arange(N_EXPERTS, dtype=jnp.int32)[None, :]).astype(jnp.int32)
    csum = jnp.cumsum(onehot, axis=0)
    rank = jnp.sum(csum * onehot, axis=1) - 1
    counts = csum[-1]
    padded = ((counts + tile - 1) // tile) * tile
    ends = jnp.cumsum(padded)
    starts = ends - padded
    pos = (starts[e_flat] + rank).astype(jnp.int32)
    assert (2 * t) % tile == 0
    n_slots = 2 * t + N_EXPERTS * tile
    tile_start =jnp.arange(n_slots // tile, dtype=jnp.int32) * tile
    tile_expert = jnp.minimum(
        jnp.sum((tile_start[:, None] >= ends[None, :]).astype(jnp.int32), axis=1), N_EXPERTS - 1)
    tail = ends[-1] + jnp.arange(N_EXPERTS, dtype=jnp.int32) * tile
    fill_start = jnp.concatenate([ends - tile, tail])
    fill_valid = jnp.concatenate([padded > counts, tail < n_slots])
    fill = jnp.stack([fill_valid.astype(jnp.int32), jnp.where(fill_valid, fill_start, 0).astype(jnp.int32)],
                     axis=1).reshape(-1)

    xs = _dispatch(pos, fill, x, n_slots, tm, tile)
    ys = _ffn(tile_expert, xs, wg, wu, wd, None, tile, tf)
    return _combine(pos, ys, x, wts[0][:, None], wts[1][:, None], g, b, tm)


def _swap_halves(w):
    half = w.shape[-1] // 2
    return jnp.concatenate([w[..., half:], w[..., :half]], axis=-1)


def _prep_weights(w):
    out = {}
    row = lambda v: v.reshape(1, -1).astype(F32)
    out["ln_mix"] = [(row(w["ln_mix_g"][i]), row(w["ln_mix_b"][i])) for i in range(DEPTH)]
    out["ln_ffn"] = [(row(w["ln_ffn_g"][i]), row(w["ln_ffn_b"][i])) for i in range(DEPTH)]
    out["rg_wg"] = w["rg_w_gate"][0].astype(BF16)
    out["rg_wx"] = w["rg_w_x"][0].astype(BF16)
    out["rg"] = {
        "conv_w": w["rg_conv_w"][0], "conv_b": row(w["rg_conv_b"][0]),
        "w_a": w["rg_w_a"][0].astype(BF16), "b_a": row(w["rg_b_a"][0]),
        "w_i": w["rg_w_i"][0].astype(BF16), "b_i": row(w["rg_b_i"][0]),
        "lam": row(w["rg_lambda"][0]),
    }
    out["rg_wout"] = w["rg_w_out"][0].astype(BF16)
    out["ffn"] = (w["ffn_w_gate"].astype(BF16), w["ffn_w_up"].astype(BF16), w["ffn_w_down"].astype(BF16))
    out["moe"] = (w["moe_w_gate"][0].astype(BF16), w["moe_w_up"][0].astype(BF16), w["moe_w_down"][0].astype(BF16))
    e = w["moe_w_router"].shape[-1]
    out["router_t"] = jnp.concatenate(
        [w["moe_w_router"][0].T, jnp.zeros((2 * SUBLANES - e, w["moe_w_router"].shape[1]), F32)], axis=0)
    out["ple_wg"] = w["ple_w_gate"].astype(BF16)
    out["ple_wp"] = w["ple_w_proj"].astype(BF16)
    kv = w["kv_w_a"]
    out["kv_wc"] = kv[:, :KV_LORA_RANK].astype(BF16)
    out["kv_wr"] = kv[:, KV_LORA_RANK:].astype(BF16)
    out["kv_wrs"] = _swap_halves(kv[:, KV_LORA_RANK:]).astype(BF16)
    out["kv_g"] = row(w["kv_norm_g"])
    out["q_wa"] = w["mla_w_q_a"][0].astype(BF16)
    out["q_g"] = row(w["mla_q_norm_g"][0])
    qb = w["mla_w_q_b"][0].reshape(-1, N_HEADS, QK_NOPE_DIM + QK_ROPE_DIM)
    lora = qb.shape[0]
    q_nope = qb[:, :, :QK_NOPE_DIM].reshape(lora, -1)
    q_rope = qb[:, :, QK_NOPE_DIM:]
    out["q_wb"] = jnp.concatenate(
        [q_nope, q_rope.reshape(lora, -1), _swap_halves(q_rope).reshape(lora, -1)], axis=1).astype(BF16)
    uk = jnp.transpose(w["kv_w_uk"], (1, 2, 0))
    zk = jnp.zeros_like(uk[0])
    out["wuk"] = jnp.stack([
        jnp.concatenate([jnp.concatenate([uk[2 * j], zk], axis=1), jnp.concatenate([zk, uk[2 * j + 1]], axis=1)],
                        axis=0) for j in range(N_HEADS // 2)]).astype(BF16)
    uv = jnp.transpose(w["kv_w_uv"], (1, 0, 2))
    zv = jnp.zeros_like(uv[0])
    out["wuv"] = jnp.stack([
        jnp.concatenate([jnp.concatenate([uv[2 * j], zv], axis=1), jnp.concatenate([zv, uv[2 * j + 1]], axis=1)],
                        axis=0) for j in range(N_HEADS // 2)]).astype(BF16)
    out["wo"] = w["mla_w_o"][0].astype(BF16)
    gap = LANES - QK_NOPE_DIM - QK_ROPE_DIM
    tiles = lambda nope, rope: jnp.concatenate(
        [nope, rope, jnp.zeros(rope.shape[:-1] + (gap,), F32)], axis=-1).reshape(nope.shape[0], -1)
    q_nope3 = qb[:, :, :QK_NOPE_DIM]
    out["q_wb_heads"] = jnp.concatenate(
        [tiles(q_nope3, q_rope), tiles(jnp.zeros_like(q_nope3), _swap_halves(q_rope))], axis=1).astype(BF16)
    uk = w["kv_w_uk"]
    out["k_w_heads"] = tiles(uk, jnp.zeros(uk.shape[:2] + (QK_ROPE_DIM,), F32)).astype(BF16)
    out["k_place"] = tiles(jnp.zeros((QK_ROPE_DIM, 1, QK_NOPE_DIM), F32),
                           jnp.eye(QK_ROPE_DIM, dtype=F32)[:, None, :]).astype(BF16)
    return out


def _rope_tables(pos, repeat):
    inv = ROPE_THETA ** (-jnp.arange(0, QK_ROPE_DIM, 2, dtype=F32) / QK_ROPE_DIM)
    ang = pos[:, None] * inv[None, :]
    cos, sin = jnp.cos(ang), jnp.sin(ang)
    cos_k = jnp.tile(jnp.concatenate([cos, cos], axis=-1), (repeat, 1))
    sin_k = jnp.tile(jnp.concatenate([-sin, sin], axis=-1), (repeat, 1))
    return cos_k, sin_k, jnp.tile(cos_k, (1, N_HEADS)), jnp.tile(sin_k, (1, N_HEADS))


def _head_tile_tables(cos_k, sin_k):
    n = cos_k.shape[0]
    gap = jnp.zeros((n, LANES - QK_NOPE_DIM - QK_ROPE_DIM), F32)
    cos_t = jnp.concatenate([jnp.ones((n, QK_NOPE_DIM), F32), cos_k, gap], axis=1)
    sin_t = jnp.concatenate([jnp.zeros((n, QK_NOPE_DIM), F32), sin_k, gap], axis=1)
    return jnp.tile(cos_t, (1, N_HEADS)), jnp.tile(sin_t, (1, N_HEADS))


def _tile_rows(t, want):
    tm = min(want, t)
    assert t % tm == 0
    return tm


def _trunk(x3, p4, conv0, rnn0, pos, pw, paged):
    nb, s, d = x3.shape
    t = nb * s
    x = x3.reshape(t, d)
    p = p4.reshape(DEPTH, t, -1)
    tm = _tile_rows(t, 512)

    gate, u = _rg_in(x, pw["rg_wg"], pw["rg_wx"], tm)
    if paged is None:
        conv_pad = jnp.concatenate(
            [jnp.zeros((nb, SUBLANES - (CONV_WIDTH - 1), D_RNN), F32), conv0], axis=1)
        hg, h_last = _rglru_seq(gate, u, conv_pad, rnn0[:, None, :], pw["rg"], nb, s, min(s, 256), 512)
        h_last = h_last[:, 0, :]
        tabs = _rope_tables(pos, 1)
    else:
        to_tm = lambda a: jnp.transpose(a.reshape(nb, s, -1), (1, 0, 2))
        hg, h_last = _rglru_step(to_tm(gate), to_tm(u), jnp.transpose(conv0, (1, 0, 2)), rnn0, pw["rg"], 512)
        hg = jnp.transpose(hg, (1, 0, 2)).reshape(t, -1)
        tabs = _rope_tables(pos, nb)
    conv_state = u.reshape(nb, s, -1)[:, s - (CONV_WIDTH - 1):, :]
    x = _mm_ln(hg, pw["rg_wout"], x, *pw["ln_mix"][0], tm)
    x = _ffn(jnp.zeros((t // tm,), jnp.int32), x, *pw["ffn"], pw["ln_ffn"][0], tm, pw["ffn"][0].shape[2])
    x = _ple(x, p[0], pw["ple_wg"][0], pw["ple_wp"][0], tm)
    cos_k, sin_k, cos_q, sin_q = tabs
    kv_args = (x, pw["kv_wc"], pw["kv_wr"], pw["kv_wrs"], pw["kv_g"], cos_k, sin_k, tm)

    tq = _tile_rows(t, 256)
    if paged is None:
        ckv, kpe, kh, vt = _kva(*kv_args, heads=(pw["k_w_heads"], pw["k_place"]))
        q = _q_heads(x, pw["q_wa"], pw["q_g"], pw["q_wb_heads"], *_head_tile_tables(cos_k, sin_k), tq)
        o = _attn_prompt(q, kh, vt, nb, s, tq, 256)
    else:
        cache_ckv, cache_kpe, page_table = paged
        ckv, kpe = _kva(*kv_args)
        q = _q_proj(x, pw["q_wa"], pw["q_g"], pw["q_wb"], pw["wuk"], cos_q, sin_q, tq, F32)
        o = _attn_paged(q.reshape(t // tq, N_HEADS, tq // s, s, QK_WIDTH), ckv.reshape(nb, s, -1),
                        kpe.reshape(nb, s, -1), cache_ckv, jnp.swapaxes(cache_kpe, 1, 2), page_table, 32, 2)
        o = o.reshape(t * N_HEADS, KV_LORA_RANK)
    x = _mla_out(o, pw["wuv"], pw["wo"], x, *pw["ln_mix"][1], tq)
    x = _moe(x, pw["router_t"], *pw["moe"], *pw["ln_ffn"][1], tq, 512 if paged is None else 256,
             pw["moe"][0].shape[2] // 2)
    x = _ple(x, p[1], pw["ple_wg"][1], pw["ple_wp"][1], tm)
    return (x.reshape(nb, s, d), conv_state[None], h_last[None], ckv.reshape(nb, s, -1), kpe.reshape(nb, s, -1))


def kernel(x_prompt, x_sample, p_prompt, p_sample, state_conv, state_rnn, cache_ckv, cache_kpe, page_table, ln_mix_g, ln_mix_b, ln_ffn_g, ln_ffn_b, rg_w_gate, rg_w_x, rg_conv_w, rg_conv_b, rg_w_a, rg_b_a, rg_w_i, rg_b_i, rg_lambda, rg_w_out, mla_w_q_a, mla_q_norm_g, mla_w_q_b, mla_w_o, kv_w_a, kv_norm_g, kv_w_uk, kv_w_uv, ffn_w_gate, ffn_w_up, ffn_w_down, moe_w_router, moe_w_gate, moe_w_up, moe_w_down, ple_w_gate, ple_w_proj):
    w = dict(
        ln_mix_g=ln_mix_g, ln_mix_b=ln_mix_b, ln_ffn_g=ln_ffn_g, ln_ffn_b=ln_ffn_b,
        rg_w_gate=rg_w_gate, rg_w_x=rg_w_x, rg_conv_w=rg_conv_w, rg_conv_b=rg_conv_b, rg_w_a=rg_w_a,
        rg_b_a=rg_b_a, rg_w_i=rg_w_i, rg_b_i=rg_b_i, rg_lambda=rg_lambda, rg_w_out=rg_w_out,
        mla_w_q_a=mla_w_q_a, mla_q_norm_g=mla_q_norm_g, mla_w_q_b=mla_w_q_b, mla_w_o=mla_w_o,
        kv_w_a=kv_w_a, kv_norm_g=kv_norm_g, kv_w_uk=kv_w_uk, kv_w_uv=kv_w_uv,
        ffn_w_gate=ffn_w_gate, ffn_w_up=ffn_w_up, ffn_w_down=ffn_w_down,
        moe_w_router=moe_w_router, moe_w_gate=moe_w_gate, moe_w_up=moe_w_up, moe_w_down=moe_w_down,
        ple_w_gate=ple_w_gate, ple_w_proj=ple_w_proj)
    pw = _prep_weights(w)
    nb, s = x_prompt.shape[:2]
    dec_s = x_sample.shape[1]
    past_len = page_table.shape[1] * cache_ckv.shape[1]
    n_a = state_conv.shape[0]
    assert n_a == 1
    conv0_p = jnp.zeros((nb, CONV_WIDTH - 1, D_RNN), state_conv.dtype)
    rnn0_p = jnp.zeros((nb, D_RNN), state_rnn.dtype)
    y_p, conv_p, rnn_p, ckv_p, kpe_p = _trunk(
        x_prompt, p_prompt, conv0_p, rnn0_p, jnp.arange(s, dtype=F32), pw, None)
    y_s, conv_s, rnn_s, ckv_s, kpe_s = _trunk(
        x_sample, p_sample, state_conv[0], state_rnn[0], past_len + jnp.arange(dec_s, dtype=F32), pw,
        (cache_ckv, cache_kpe, page_table))
    return (y_p, y_s, conv_p, rnn_p, ckv_p, kpe_p, conv_s, rnn_s, ckv_s, kpe_s)
```

```python
import functools

import jax
import jax.numpy as jnp
from jax import lax
from jax.experimental import pallas as pl
from jax.experimental.pallas import tpu as pltpu

F32 = jnp.float32
BF16 = jnp.bfloat16

D_RNN = 1536
RNN_BLOCK = 128
CONV_WIDTH = 4
LRU_C = 8.0
N_HEADS = 16
KV_LORA_RANK = 256
QK_NOPE_DIM = 64
QK_ROPE_DIM = 32
V_HEAD_DIM = 64
ROPE_THETA = 10000.0
SOFTMAX_SCALE = (QK_NOPE_DIM + QK_ROPE_DIM) ** -0.5
N_EXPERTS = 8
LN_EPS = 1e-5
RMS_EPS = 1e-6
DEPTH = 2
DEEPNORM_ALPHA = (2.0 * DEPTH) ** 0.25

QK_WIDTH = KV_LORA_RANK + QK_ROPE_DIM
QK_SCALE = SOFTMAX_SCALE * 1.4426950408889634

SUBLANES = 8
LANES = 128
VMEM_LIMIT = 48 * 1024 * 1024


def _params(*sem):
    return pltpu.CompilerParams(dimension_semantics=sem, vmem_limit_bytes=VMEM_LIMIT)


def _dot(a, b):
    return jnp.dot(a, b, preferred_element_type=F32)


def _dot_nt(a, b):
    return lax.dot_general(a, b, (((1,), (1,)), ((), ())), preferred_element_type=F32)


def _layer_norm(z, g, b):
    mu = jnp.mean(z, -1, keepdims=True)
    d = z - mu
    var = jnp.mean(d * d, -1, keepdims=True)
    return d * lax.rsqrt(var + LN_EPS) * g + b


def _rms_norm(z, g):
    return z * lax.rsqrt(jnp.mean(z * z, -1, keepdims=True) + RMS_EPS) * g


def _full(shape):
    n = len(shape)
    return pl.BlockSpec(shape, lambda *_: (0,) * n)


def _rg_in_kernel(x_ref, wg_ref, wx_ref, gate_ref, u_ref):
    xb = x_ref[...].astype(BF16)
    gate_ref[...] = jax.nn.gelu(_dot(xb, wg_ref[...]))
    u_ref[...] = _dot(xb, wx_ref[...])


def _rg_in(x, wg, wx, tm):
    t, d = x.shape
    n = wg.shape[1]
    return pl.pallas_call(
        _rg_in_kernel,
        grid=(t // tm,),
        in_specs=[pl.BlockSpec((tm, d), lambda i: (i, 0)), _full((d, n)), _full((d, n))],
        out_specs=[pl.BlockSpec((tm, n), lambda i: (i, 0))] * 2,
        out_shape=[jax.ShapeDtypeStruct((t, n), F32)] * 2,
        compiler_params=_params("parallel"),
        name="rg_in",
    )(x, wg, wx)


def _rglru_gates(conv, wa_ref, ba, wi_ref, bi, lam):
    nblk = conv.shape[1] // RNN_BLOCK
    cb = conv.astype(BF16)
    ra = jnp.concatenate(
        [_dot(cb[:, n * RNN_BLOCK:(n + 1) * RNN_BLOCK], wa_ref[n]) for n in range(nblk)], axis=1)
    ia = jnp.concatenate(
        [_dot(cb[:, n * RNN_BLOCK:(n + 1) * RNN_BLOCK], wi_ref[n]) for n in range(nblk)], axis=1)
    r = jax.nn.sigmoid(ra + ba)
    i = jax.nn.sigmoid(ia + bi)
    z = -lam
    softplus = jnp.maximum(z, 0.0) + jnp.log1p(jnp.exp(-jnp.abs(z)))
    log_a = -LRU_C * r * softplus
    a = jnp.exp(log_a)
    v = 1.0 - a * a
    b = jnp.where(v > 0.0, v * lax.rsqrt(v), 0.0) * (i * conv)
    return a, b


def _rglru_seq_kernel(gate_ref, u_ref, conv0_ref, h0_ref, cw_ref, cb_ref, wa_ref, ba_ref, wi_ref, bi_ref,
                      lam_ref, hg_ref, hlast_ref, ubuf, hcar, abuf, bbuf):
    c = pl.program_id(2)
    tc, db = u_ref.shape
    halo = SUBLANES

    @pl.when(c == 0)
    def _():
        ubuf[0:halo, :] = conv0_ref[0]
        hcar[...] = jnp.broadcast_to(h0_ref[0], hcar.shape)

    ubuf[halo:halo + tc, :] = u_ref[...]
    cw = cw_ref[...]
    conv = cb_ref[...]
    for k in range(CONV_WIDTH):
        off = halo - (CONV_WIDTH - 1) + k
        conv = conv + ubuf[off:off + tc, :] * cw[k:k + 1, :]
    ubuf[0:halo, :] = ubuf[tc:tc + halo, :]

    a, b = _rglru_gates(conv, wa_ref, ba_ref[...], wi_ref, bi_ref[...], lam_ref[...])

    groups = tc // SUBLANES
    a = a.reshape(groups, SUBLANES, db)
    b = b.reshape(groups, SUBLANES, db)
    row = lax.broadcasted_iota(jnp.int32, a.shape, 1)
    shift = 1
    while shift < SUBLANES:
        a_prev = pltpu.roll(a, shift, 1)
        b_prev = pltpu.roll(b, shift, 1)
        keep = row >= shift
        b = jnp.where(keep, a * b_prev + b, b)
        a = jnp.where(keep, a * a_prev, a)
        shift *= 2
    abuf[...] = a.reshape(tc, db)
    bbuf[...] = b.reshape(tc, db)

    def group(g, h):
        off = pl.multiple_of(g * SUBLANES, SUBLANES)
        hb = abuf[pl.ds(off, SUBLANES), :] * h + bbuf[pl.ds(off, SUBLANES), :]
        bbuf[pl.ds(off, SUBLANES), :] = hb
        return jnp.broadcast_to(hb[SUBLANES - 1:SUBLANES, :], hb.shape)

    h = lax.fori_loop(0, tc // SUBLANES, group, hcar[...])
    hcar[...] = h
    hg_ref[...] = (bbuf[...] * gate_ref[...]).astype(BF16)

    @pl.when(c == pl.num_programs(2) - 1)
    def _():
        hlast_ref[0] = h[0:1, :]


def _rglru_seq(gate, u, conv0, h0, rw, nb, s, tc, db):
    t, d = u.shape
    nc = s // tc
    kb = db // RNN_BLOCK
    row = lambda b, j, c: (b * nc + c, j)
    vec = pl.BlockSpec((1, db), lambda b, j, c: (0, j))
    blk = pl.BlockSpec((kb, RNN_BLOCK, RNN_BLOCK), lambda b, j, c: (j, 0, 0))
    return pl.pallas_call(
        _rglru_seq_kernel,
        grid=(nb, d // db, nc),
        in_specs=[
            pl.BlockSpec((tc, db), row), pl.BlockSpec((tc, db), row),
            pl.BlockSpec((1, SUBLANES, db), lambda b, j, c: (b, 0, j)),
            pl.BlockSpec((1, 1, db), lambda b, j, c: (b, 0, j)),
            pl.BlockSpec((CONV_WIDTH, db), lambda b, j, c: (0, j)), vec, blk, vec, blk, vec, vec,
        ],
        out_specs=[pl.BlockSpec((tc, db), row), pl.BlockSpec((1, 1, db), lambda b, j, c: (b, 0, j))],
        out_shape=[jax.ShapeDtypeStruct((t, d), BF16), jax.ShapeDtypeStruct((nb, 1, d), F32)],
        scratch_shapes=[pltpu.VMEM((tc + SUBLANES, db), F32), pltpu.VMEM((SUBLANES, db), F32),
                        pltpu.VMEM((tc, db), F32), pltpu.VMEM((tc, db), F32)],
        compiler_params=_params("parallel", "parallel", "arbitrary"),
        name="rglru_seq",
    )(gate, u, conv0, h0, rw["conv_w"], rw["conv_b"], rw["w_a"], rw["b_a"], rw["w_i"], rw["b_i"], rw["lam"])


def _rglru_step_kernel(gate_ref, u_ref, conv0_ref, h0_ref, cw_ref, cb_ref, wa_ref, ba_ref, wi_ref, bi_ref,
                       lam_ref, hg_ref, hlast_ref):
    s, nb, db = u_ref.shape
    ue = jnp.concatenate([conv0_ref[...], u_ref[...]], axis=0)
    cw = cw_ref[...]
    conv = cb_ref[...][None]
    for k in range(CONV_WIDTH):
        conv = conv + ue[k:k + s] * cw[k:k + 1, :][None]
    a, b = _rglru_gates(conv.reshape(s * nb, db), wa_ref, ba_ref[...], wi_ref, bi_ref[...], lam_ref[...])
    a = a.reshape(s, nb, db)
    b = b.reshape(s, nb, db)
    h = h0_ref[...]
    for t in range(s):
        h = a[t] * h + b[t]
        hg_ref[t] = (h * gate_ref[t]).astype(BF16)
    hlast_ref[...] = h


def _rglru_step(gate, u, conv0, h0, rw, db):
    s, nb, d = u.shape
    kb = db // RNN_BLOCK
    cube = pl.BlockSpec((s, nb, db), lambda j: (0, 0, j))
    vec = pl.BlockSpec((1, db), lambda j: (0, j))
    blk = pl.BlockSpec((kb, RNN_BLOCK, RNN_BLOCK), lambda j: (j, 0, 0))
    return pl.pallas_call(
        _rglru_step_kernel,
        grid=(d // db,),
        in_specs=[cube, cube, pl.BlockSpec((CONV_WIDTH - 1, nb, db), lambda j: (0, 0, j)),
                  pl.BlockSpec((nb, db), lambda j: (0, j)),
                  pl.BlockSpec((CONV_WIDTH, db), lambda j: (0, j)), vec, blk, vec, blk, vec, vec],
        out_specs=[cube, pl.BlockSpec((nb, db), lambda j: (0, j))],
        out_shape=[jax.ShapeDtypeStruct((s, nb, d), BF16), jax.ShapeDtypeStruct((nb, d), F32)],
        compiler_params=_params("parallel"),
        name="rglru_step",
    )(gate, u, conv0, h0, rw["conv_w"], rw["conv_b"], rw["w_a"], rw["b_a"], rw["w_i"], rw["b_i"], rw["lam"])


def _mm_ln_kernel(x_ref, w_ref, res_ref, g_ref, b_ref, o_ref):
    y = _dot(x_ref[...], w_ref[...])
    o_ref[...] = _layer_norm(DEEPNORM_ALPHA * res_ref[...] + y, g_ref[...], b_ref[...])


def _mm_ln(x, w, res, g, b, tm):
    t, k = x.shape
    n = w.shape[1]
    return pl.pallas_call(
        _mm_ln_kernel,
        grid=(t // tm,),
        in_specs=[pl.BlockSpec((tm, k), lambda i: (i, 0)), _full((k, n)),
                  pl.BlockSpec((tm, n), lambda i: (i, 0)), _full((1, n)), _full((1, n))],
        out_specs=pl.BlockSpec((tm, n), lambda i: (i, 0)),
        out_shape=jax.ShapeDtypeStruct((t, n), F32),
        compiler_params=_params("parallel"),
        name="mm_ln",
    )(x, w, res, g, b)


def _ffn_kernel(te_ref, x_ref, wg_ref, wu_ref, wd_ref, *rest, post_norm):
    del te_ref
    if post_norm:
        g_ref, b_ref, o_ref, xb_ref, acc_ref = rest
    else:
        o_ref, xb_ref, acc_ref = rest
    c = pl.program_id(1)

    @pl.when(c == 0)
    def _():
        xb_ref[...] = x_ref[...].astype(BF16)
        acc_ref[...] = jnp.zeros_like(acc_ref)

    xb = xb_ref[...]
    h = (jax.nn.silu(_dot(xb, wg_ref[0])) * _dot(xb, wu_ref[0])).astype(BF16)
    acc_ref[...] += _dot(h, wd_ref[0])

    @pl.when(c == pl.num_programs(1) - 1)
    def _():
        if post_norm:
            o_ref[...] = _layer_norm(DEEPNORM_ALPHA * x_ref[...] + acc_ref[...], g_ref[...], b_ref[...])
        else:
            o_ref[...] = acc_ref[...]


def _ffn(tile_expert, x, wg, wu, wd, norm, tm, tf):
    r, d = x.shape
    f = wg.shape[2]
    once = dict(pipeline_mode=pl.Buffered(1)) if (wg.shape[0] == 1 and tf == f) else {}
    in_specs = [
        pl.BlockSpec((tm, d), lambda i, c, te: (i, 0)),
        pl.BlockSpec((1, d, tf), lambda i, c, te: (te[i], 0, c), **once),
        pl.BlockSpec((1, d, tf), lambda i, c, te: (te[i], 0, c), **once),
        pl.BlockSpec((1, tf, d), lambda i, c, te: (te[i], c, 0), **once),
    ]
    args = [x, wg, wu, wd]
    if norm is not None:
        in_specs += [pl.BlockSpec((1, d), lambda i, c, te: (0, 0))] * 2
        args += list(norm)
    return pl.pallas_call(
        functools.partial(_ffn_kernel, post_norm=norm is not None),
        grid_spec=pltpu.PrefetchScalarGridSpec(
            num_scalar_prefetch=1,
            grid=(r // tm, f // tf),
            in_specs=in_specs,
            out_specs=pl.BlockSpec((tm, d), lambda i, c, te: (i, 0)),
            scratch_shapes=[pltpu.VMEM((tm, d), BF16), pltpu.VMEM((tm, d), F32)],
        ),
        out_shape=jax.ShapeDtypeStruct((r, d), F32),
        compiler_params=_params("parallel", "arbitrary"),
        name="ffn",
    )(tile_expert, *args)


def _ple_kernel(x_ref, p_ref, wg_ref, wp_ref, o_ref):
    x = x_ref[...]
    a = _dot(x.astype(BF16), wg_ref[...])
    c = _dot(p_ref[...].astype(BF16), wp_ref[...])
    o_ref[...] = x + jax.nn.sigmoid(a) * c


def _ple(x, p, wg, wp, tm):
    t, d = x.shape
    dp = p.shape[1]
    return pl.pallas_call(
        _ple_kernel,
        grid=(t // tm,),
        in_specs=[pl.BlockSpec((tm, d), lambda i: (i, 0)), pl.BlockSpec((tm, dp), lambda i: (i, 0)),
                  _full((d, d)), _full((dp, d))],
        out_specs=pl.BlockSpec((tm, d), lambda i: (i, 0)),
        out_shape=jax.ShapeDtypeStruct((t, d), F32),
        compiler_params=_params("parallel"),
        name="ple",
    )(x, p, wg, wp)


def _kva_kernel(x_ref, wc_ref, wr_ref, wrs_ref, g_ref, cos_ref, sin_ref, *rest, attn_operands):
    xb = x_ref[...].astype(BF16)
    ckv = _rms_norm(_dot(xb, wc_ref[...]), g_ref[...])
    kpe = _dot(xb, wr_ref[...]) * cos_ref[...] + _dot(xb, wrs_ref[...]) * sin_ref[...]
    if not attn_operands:
        ckv_ref, kpe_ref = rest
        ckv_ref[...] = ckv
        kpe_ref[...] = kpe
        return
    wk_ref, place_ref, wv_ref, ckv_ref, kpe_ref, kh_ref, vt_ref = rest
    ckv_ref[...] = ckv
    kpe_ref[...] = kpe
    cb = ckv.astype(BF16)
    nope = _dot(cb, wk_ref[...])
    rope = _dot(kpe.astype(BF16), place_ref[...])
    for h in range(N_HEADS):
        kh_ref[h] = (nope[:, h * LANES:(h + 1) * LANES] + rope).astype(BF16)
    vt_ref[...] = _dot(cb, wv_ref[...]).T.astype(BF16)


def _kva(x, wc, wr, wrs, g, cos, sin, tm, heads=None):
    t, d = x.shape
    nper = cos.shape[0] // tm
    tab = pl.BlockSpec((tm, QK_ROPE_DIM), lambda i: (i % nper, 0))
    out = lambda n: pl.BlockSpec((tm, n), lambda i: (i, 0))
    in_specs = [pl.BlockSpec((tm, d), lambda i: (i, 0)), _full(wc.shape), _full(wr.shape), _full(wrs.shape),
                _full(g.shape), tab, tab]
    args = [x, wc, wr, wrs, g, cos, sin]
    out_specs = [out(KV_LORA_RANK), out(QK_ROPE_DIM)]
    out_shape = [jax.ShapeDtypeStruct((t, KV_LORA_RANK), F32), jax.ShapeDtypeStruct((t, QK_ROPE_DIM), F32)]
    if heads is not None:
        in_specs += [_full(w.shape) for w in heads]
        args += list(heads)
        hv = N_HEADS * V_HEAD_DIM
        out_specs += [pl.BlockSpec((N_HEADS, tm, LANES), lambda i: (0, i, 0)), pl.BlockSpec((hv, tm), lambda i: (0, i))]
        out_shape += [jax.ShapeDtypeStruct((N_HEADS, t, LANES), BF16), jax.ShapeDtypeStruct((hv, t), BF16)]
    return pl.pallas_call(
        functools.partial(_kva_kernel, attn_operands=heads is not None),
        grid=(t // tm,),
        in_specs=in_specs,
        out_specs=out_specs,
        out_shape=out_shape,
        compiler_params=_params("parallel"),
        name="kv_latent",
    )(*args)


def _q_heads_kernel(x_ref, wqa_ref, qg_ref, wqb_ref, cos_ref, sin_ref, q_ref):
    tm = x_ref.shape[0]
    wide = N_HEADS * LANES
    cq = _rms_norm(_dot(x_ref[...].astype(BF16), wqa_ref[...]), qg_ref[...])
    q = _dot(cq.astype(BF16), wqb_ref[...])
    q = (q[:, :wide] * cos_ref[...] + q[:, wide:] * sin_ref[...]) * QK_SCALE
    for h in range(N_HEADS):
        q_ref[h * tm:(h + 1) * tm, :] = q[:, h * LANES:(h + 1) * LANES].astype(BF16)


def _q_heads(x, wqa, qg, wqb, cos, sin, tm):
    t, d = x.shape
    nper = cos.shape[0] // tm
    tab = pl.BlockSpec((tm, N_HEADS * LANES), lambda i: (i % nper, 0))
    return pl.pallas_call(
        _q_heads_kernel,
        grid=(t // tm,),
        in_specs=[pl.BlockSpec((tm, d), lambda i: (i, 0)), _full(wqa.shape), _full(qg.shape), _full(wqb.shape),
                  tab, tab],
        out_specs=pl.BlockSpec((N_HEADS * tm, LANES), lambda i: (i, 0)),
        out_shape=jax.ShapeDtypeStruct((N_HEADS * t, LANES), BF16),
        compiler_params=_params("parallel"),
        name="q_heads",
    )(x, wqa, qg, wqb, cos, sin)


def _q_kernel(x_ref, wqa_ref, qg_ref, wqb_ref, wuk_ref, cos_ref, sin_ref, q_ref):
    tm = x_ref.shape[0]
    nope = N_HEADS * QK_NOPE_DIM
    pe = N_HEADS * QK_ROPE_DIM
    c = KV_LORA_RANK
    r = QK_ROPE_DIM
    cq = _rms_norm(_dot(x_ref[...].astype(BF16), wqa_ref[...]), qg_ref[...])
    q = _dot(cq.astype(BF16), wqb_ref[...])
    q_pe = (q[:, nope:nope + pe] * cos_ref[...] + q[:, nope + pe:nope + 2 * pe] * sin_ref[...]) * QK_SCALE
    qn = q[:, :nope].astype(BF16)
    pair = 2 * QK_NOPE_DIM
    for j in range(N_HEADS // 2):
        ql = _dot(qn[:, j * pair:(j + 1) * pair], wuk_ref[j]) * QK_SCALE
        for k in range(2):
            h = 2 * j + k
            q_ref[h * tm:(h + 1) * tm, :c] = ql[:, k * c:(k + 1) * c].astype(q_ref.dtype)
            q_ref[h * tm:(h + 1) * tm, c:] = q_pe[:, h * r:(h + 1) * r].astype(q_ref.dtype)


def _q_proj(x, wqa, qg, wqb, wuk, cos, sin, tm, dtype):
    t, d = x.shape
    nper = cos.shape[0] // tm
    pe = N_HEADS * QK_ROPE_DIM
    tab = pl.BlockSpec((tm, pe), lambda i: (i % nper, 0))
    return pl.pallas_call(
        _q_kernel,
        grid=(t // tm,),
        in_specs=[pl.BlockSpec((tm, d), lambda i: (i, 0)), _full(wqa.shape), _full(qg.shape), _full(wqb.shape),
                  _full(wuk.shape), tab, tab],
        out_specs=pl.BlockSpec((N_HEADS * tm, QK_WIDTH), lambda i: (i, 0)),
        out_shape=jax.ShapeDtypeStruct((N_HEADS * t, QK_WIDTH), dtype),
        compiler_params=_params("parallel"),
        name="q_proj",
    )(x, wqa, qg, wqb, wuk, cos, sin)


def _widen(stat, width):
    return jnp.concatenate([stat] * (width // LANES), axis=1)


def _softmax_step(s, kv, m_s, l_s, acc):
    m_prev = m_s[...]
    m_new = jnp.maximum(m_prev, jnp.max(s, -1, keepdims=True))
    corr = jnp.exp2(m_prev - m_new)
    p = jnp.exp2(s - _widen(m_new, s.shape[1]))
    l_s[...] = l_s[...] * corr + jnp.sum(p, -1, keepdims=True)
    acc[...] = acc[...] * _widen(corr, KV_LORA_RANK) + _dot(p.astype(BF16), kv)
    m_s[...] = m_new


def _attn_kernel(qi_ref, kj_ref, q_ref, kh_ref, vt_ref, o_ref, m_s, l_s, acc, *, tq, tk):
    p = pl.program_id(1)
    i = qi_ref[p]
    j = kj_ref[p]

    @pl.when(j == 0)
    def _():
        m_s[...] = jnp.full_like(m_s, -jnp.inf)
        l_s[...] = jnp.zeros_like(l_s)
        acc[...] = jnp.zeros_like(acc)

    def step(masked):
        st = jnp.concatenate(
            [_dot_nt(kh_ref[h], q_ref[h * tq:(h + 1) * tq, :]) for h in range(N_HEADS)], axis=1)
        if masked:
            k_pos = j * tk + lax.broadcasted_iota(jnp.int32, st.shape, 0)
            q_pos = i * tq + (lax.broadcasted_iota(jnp.int32, st.shape, 1) & (tq - 1))
            st = jnp.where(k_pos <= q_pos, st, -jnp.inf)
        m_prev = m_s[...]
        m_new = jnp.maximum(m_prev, jnp.max(st, 0, keepdims=True))
        corr = jnp.exp2(m_prev - m_new)
        pt = jnp.exp2(st - m_new)
        l_s[...] = l_s[...] * corr + jnp.sum(pt, 0, keepdims=True)
        pb = pt.astype(BF16)
        vd = V_HEAD_DIM
        acc_prev = acc[...]
        acc[...] = jnp.concatenate(
            [acc_prev[h * vd:(h + 1) * vd, :] * corr[:, h * tq:(h + 1) * tq]
             + _dot(vt_ref[h * vd:(h + 1) * vd, :], pb[:, h * tq:(h + 1) * tq]) for h in range(N_HEADS)], axis=0)
        m_s[...] = m_new

    crosses_diagonal = j * tk + tk - 1 > i * tq

    @pl.when(jnp.logical_not(crosses_diagonal))
    def _():
        step(False)

    @pl.when(crosses_diagonal)
    def _():
        step(True)

    @pl.when(j == (i * tq + tq - 1) // tk)
    def _():
        vd = V_HEAD_DIM
        inv = 1.0 / l_s[...]
        out = jnp.concatenate(
            [acc[h * vd:(h + 1) * vd, :] * inv[:, h * tq:(h + 1) * tq] for h in range(N_HEADS)], axis=0)
        o_ref[...] = out.T.astype(o_ref.dtype)


def _attn_prompt(q, kh, vt, nb, s, tq, tk):
    hv = N_HEADS * V_HEAD_DIM
    nq, nk = s // tq, s // tk
    assert tq & (tq - 1) == 0
    rows = N_HEADS * tq
    pairs = [(i, j) for i in range(nq) for j in range((i * tq + tq - 1) // tk + 1)]
    qi = jnp.asarray([i for i, _ in pairs], jnp.int32)
    kj = jnp.asarray([j for _, j in pairs], jnp.int32)
    q_row = lambda b, p, qi, kj: (b * nq + qi[p], 0)
    return pl.pallas_call(
        functools.partial(_attn_kernel, tq=tq, tk=tk),
        grid_spec=pltpu.PrefetchScalarGridSpec(
            num_scalar_prefetch=2,
            grid=(nb, len(pairs)),
            in_specs=[pl.BlockSpec((rows, LANES), q_row),
                      pl.BlockSpec((N_HEADS, tk, LANES), lambda b, p, qi, kj: (0, b * nk + kj[p], 0)),
                      pl.BlockSpec((hv, tk), lambda b, p, qi, kj: (0, b * nk + kj[p]))],
            out_specs=pl.BlockSpec((tq, hv), q_row),
            scratch_shapes=[pltpu.VMEM((1, rows), F32), pltpu.VMEM((1, rows), F32), pltpu.VMEM((hv, tq), F32)],
        ),
        out_shape=jax.ShapeDtypeStruct((nb * s, hv), BF16),
        compiler_params=_params("parallel", "arbitrary"),
        name="attn_prompt",
    )(qi, kj, q, kh, vt)


def _attn_paged_kernel(pt_ref, q_ref, cnew_ref, knew_ref, ckv_hbm, kpe_hbm, o_ref, m_s, l_s, acc, kbuf, kpbuf,
                       ck_in, kp_in, sem, *, seqs, pages, s_new):
    g = pl.program_id(0)
    c = pl.program_id(1)
    nc = pl.num_programs(1)
    n_pages = nc * pages
    step = g * nc + c
    rows = N_HEADS * s_new
    page = ck_in.shape[2]

    def page_copies(group, chunk, slot):
        copies = []
        for a in range(seqs):
            for k in range(pages):
                pid = pt_ref[(group * seqs + a) * n_pages + chunk * pages + k]
                copies.append(pltpu.make_async_copy(ckv_hbm.at[pid], ck_in.at[slot, a * pages + k], sem.at[slot, 0]))
                copies.append(pltpu.make_async_copy(kpe_hbm.at[pid], kp_in.at[slot, a * pages + k], sem.at[slot, 1]))
        return copies

    @pl.when(step == 0)
    def _():
        for cp in page_copies(0, 0, 0):
            cp.start()

    @pl.when(step + 1 < pl.num_programs(0) * nc)
    def _():
        nxt = step + 1
        for cp in page_copies(nxt // nc, nxt % nc, nxt % 2):
            cp.start()

    slot = step % 2
    pltpu.make_async_copy(ckv_hbm.at[pl.ds(0, seqs * pages)], ck_in.at[slot], sem.at[slot, 0]).wait()
    pltpu.make_async_copy(kpe_hbm.at[pl.ds(0, seqs * pages)], kp_in.at[slot], sem.at[slot, 1]).wait()
    ck_refs = [ck_in.at[slot, i] for i in range(seqs * pages)]
    kp_refs = [kp_in.at[slot, i] for i in range(seqs * pages)]

    qs = [q_ref[:, a].reshape(rows, QK_WIDTH).astype(BF16) for a in range(seqs)]
    qls = [q[:, :KV_LORA_RANK] for q in qs]
    qps = [q[:, KV_LORA_RANK:] for q in qs]

    @pl.when(c == 0)
    def _():
        for a in range(seqs):
            pad = 2 * SUBLANES - s_new
            cn = jnp.concatenate([cnew_ref[a], jnp.zeros((pad, KV_LORA_RANK), F32)], axis=0).astype(BF16)
            kn = jnp.concatenate([knew_ref[a], jnp.zeros((pad, QK_ROPE_DIM), F32)], axis=0).astype(BF16)
            s = _dot_nt(qls[a], cn) + _dot_nt(qps[a], kn)
            tok = lax.broadcasted_iota(jnp.int32, s.shape, 0) & (s_new - 1)
            key = lax.broadcasted_iota(jnp.int32, s.shape, 1)
            s = jnp.where(key <= tok, s, -jnp.inf)
            m = jnp.max(s, -1, keepdims=True)
            p = jnp.exp2(s - m)
            m_s[a] = jnp.broadcast_to(m, m_s.shape[1:])
            l_s[a] = jnp.broadcast_to(jnp.sum(p, -1, keepdims=True), l_s.shape[1:])
            acc[a] = _dot(p.astype(BF16), cn)

    for a in range(seqs):
        for k in range(pages):
            kbuf[a, k * page:(k + 1) * page, :] = ck_refs[a * pages + k][...].astype(BF16)
            kpbuf[a, :, k * page:(k + 1) * page] = kp_refs[a * pages + k][...].astype(BF16)
        kv = kbuf[a]
        s = _dot_nt(qls[a], kv) + _dot(qps[a], kpbuf[a])
        _softmax_step(s, kv, m_s.at[a], l_s.at[a], acc.at[a])

    @pl.when(c == pl.num_programs(1) - 1)
    def _():
        for a in range(seqs):
            o = acc[a] * _widen(1.0 / l_s[a], KV_LORA_RANK)
            o_ref[:, a] = o.reshape(N_HEADS, s_new, KV_LORA_RANK)


def _attn_paged(q, cnew, knew, cache_ckv, cache_kpe_t, page_table, pages, seqs):
    n_tiles, _, per_tile, s_new, _ = q.shape
    nb = n_tiles * per_tile
    assert s_new == SUBLANES and per_tile % seqs == 0
    groups = per_tile // seqs
    n_pages = page_table.shape[1]
    page = cache_ckv.shape[1]
    c = KV_LORA_RANK
    r = QK_ROPE_DIM
    per_g = lambda g, j, pt: (g, 0, 0)
    q_blk = lambda w: pl.BlockSpec((None, N_HEADS, seqs, s_new, w),
                                   lambda g, j, pt: (g // groups, 0, g % groups, 0, 0))

    in_specs = [q_blk(QK_WIDTH), pl.BlockSpec((seqs, s_new, c), per_g), pl.BlockSpec((seqs, s_new, r), per_g),
                pl.BlockSpec(memory_space=pl.ANY), pl.BlockSpec(memory_space=pl.ANY)]
    rows = N_HEADS * s_new
    assert n_pages % pages == 0
    return pl.pallas_call(
        functools.partial(_attn_paged_kernel, seqs=seqs, pages=pages, s_new=s_new),
        grid_spec=pltpu.PrefetchScalarGridSpec(
            num_scalar_prefetch=1,
            grid=(nb // seqs, n_pages // pages),
            in_specs=in_specs,
            out_specs=q_blk(c),
            scratch_shapes=[pltpu.VMEM((seqs, rows, LANES), F32), pltpu.VMEM((seqs, rows, LANES), F32),
                            pltpu.VMEM((seqs, rows, c), F32), pltpu.VMEM((seqs, pages * page, c), BF16),
                            pltpu.VMEM((seqs, r, pages * page), BF16),
                            pltpu.VMEM((2, seqs * pages, page, c), F32), pltpu.VMEM((2, seqs * pages, r, page), F32),
                            pltpu.SemaphoreType.DMA((2, 2))],
        ),
        out_shape=jax.ShapeDtypeStruct(q.shape[:-1] + (c,), F32),
        compiler_params=_params("arbitrary", "arbitrary"),
        name="attn_paged",
    )(page_table.reshape(-1), q, cnew, knew, cache_ckv, cache_kpe_t)


def _mla_out_kernel(o_ref, wuv_ref, wo_ref, res_ref, g_ref, b_ref, out_ref):
    tm = res_ref.shape[0]
    head = lambda h: o_ref[h * tm:(h + 1) * tm, :].astype(BF16)
    v = jnp.concatenate(
        [_dot(jnp.concatenate([head(2 * j), head(2 * j + 1)], axis=1), wuv_ref[j]) for j in range(N_HEADS // 2)],
        axis=1).astype(BF16)
    y = _dot(v, wo_ref[...])
    out_ref[...] = _layer_norm(DEEPNORM_ALPHA * res_ref[...] + y, g_ref[...], b_ref[...])


def _mla_out(o, wuv, wo, res, g, b, tm):
    t, n = res.shape
    return pl.pallas_call(
        _mla_out_kernel,
        grid=(t // tm,),
        in_specs=[pl.BlockSpec((N_HEADS * tm, o.shape[1]), lambda i: (i, 0)), _full(wuv.shape), _full(wo.shape),
                  pl.BlockSpec((tm, n), lambda i: (i, 0)), _full((1, n)), _full((1, n))],
        out_specs=pl.BlockSpec((tm, n), lambda i: (i, 0)),
        out_shape=jax.ShapeDtypeStruct((t, n), F32),
        compiler_params=_params("parallel"),
        name="mla_out",
    )(o, wuv, wo, res, g, b)


def _router_kernel(x_ref, w_ref, idx_ref, wt_ref):
    x = x_ref[...]
    w = w_ref[...]
    xh = x.astype(BF16)
    xl = (x - xh.astype(F32)).astype(BF16)
    wh = w.astype(BF16)
    wl = (w - wh.astype(F32)).astype(BF16)
    lg = (_dot_nt(wh, xh) + _dot_nt(wh, xl) + _dot_nt(wl, xh))[:N_EXPERTS]
    e = lax.broadcasted_iota(jnp.int32, lg.shape, 0).astype(F32)
    none = float(N_EXPERTS)
    v1 = jnp.max(lg, 0, keepdims=True)
    i1 = jnp.min(jnp.where(lg == v1, e, none), 0, keepdims=True)
    lg2 = jnp.where(e == i1, -jnp.inf, lg)
    v2 = jnp.max(lg2, 0, keepdims=True)
    i2 = jnp.min(jnp.where(lg2 == v2, e, none), 0, keepdims=True)
    ex = jnp.exp(v2 - v1)
    den = 1.0 + ex
    idx_ref[...] = jnp.concatenate([i1, i2], axis=0).astype(jnp.int32)
    wt_ref[...] = jnp.concatenate([1.0 / den, ex / den], axis=0)


def _router(x, w_t, tm):
    t, d = x.shape
    return pl.pallas_call(
        _router_kernel,
        grid=(t // tm,),
        in_specs=[pl.BlockSpec((tm, d), lambda i: (i, 0)), _full(w_t.shape)],
        out_specs=[pl.BlockSpec((2, tm), lambda i: (0, i))] * 2,
        out_shape=[jax.ShapeDtypeStruct((2, t), jnp.int32), jax.ShapeDtypeStruct((2, t), F32)],
        compiler_params=_params("parallel"),
        name="router",
    )(x, w_t)


def _dispatch_kernel(pos_ref, fill_ref, x_ref, xs_hbm, zeros, sem, fill_sem, *, rows, tile):
    i = pl.program_id(0)
    t = pl.num_programs(0) * rows
    n_fill = fill_ref.shape[0] // 2
    fill = lambda k: pltpu.make_async_copy(
        zeros, xs_hbm.at[pl.ds(pl.multiple_of(fill_ref[2 * k + 1], tile), tile), :], fill_sem)

    @pl.when(i == 0)
    def _():
        zeros[...] = jnp.zeros_like(zeros)
        for k in range(n_fill):
            @pl.when(fill_ref[2 * k] == 1)
            def _(k=k):
                fill(k).start()
        for k in range(n_fill):
            @pl.when(fill_ref[2 * k] == 1)
            def _(k=k):
                fill(k).wait()

    base = i * rows

    def issue(r, carry):
        for k in range(2):
            slot = pos_ref[k * t + base + r]
            pltpu.make_async_copy(x_ref.at[pl.ds(r, 1), :], xs_hbm.at[pl.ds(slot, 1), :], sem.at[k]).start()
        return carry

    lax.fori_loop(0, rows, issue, 0, unroll=4)
    for k in range(2):
        pltpu.make_async_copy(x_ref, xs_hbm.at[pl.ds(0, rows), :], sem.at[k]).wait()


def _dispatch(pos, fill, x, n_slots, rows, tile):
    t, d = x.shape
    return pl.pallas_call(
        functools.partial(_dispatch_kernel, rows=rows, tile=tile),
        grid_spec=pltpu.PrefetchScalarGridSpec(
            num_scalar_prefetch=2,
            grid=(t // rows,),
            in_specs=[pl.BlockSpec((rows, d), lambda i, pos, fill: (i, 0))],
            out_specs=pl.BlockSpec(memory_space=pl.ANY),
            scratch_shapes=[pltpu.VMEM((tile, d), x.dtype), pltpu.SemaphoreType.DMA((2,)),
                            pltpu.SemaphoreType.DMA],
        ),
        out_shape=jax.ShapeDtypeStruct((n_slots, d), x.dtype),
        compiler_params=_params("arbitrary"),
        name="moe_dispatch",
    )(pos, fill, x)


def _combine_kernel(pos_ref, y_hbm, x_ref, w1_ref, w2_ref, g_ref, b_ref, o_ref, y1, y2, sem):
    rows = o_ref.shape[0]
    i = pl.program_id(0)
    n = pl.num_programs(0)
    t = n * rows

    def gather(step, slot):
        base = step * rows

        def issue(r, carry):
            p1 = pos_ref[base + r]
            p2 = pos_ref[t + base + r]
            pltpu.make_async_copy(y_hbm.at[pl.ds(p1, 1), :], y1.at[slot, pl.ds(r, 1), :], sem.at[slot, 0]).start()
            pltpu.make_async_copy(y_hbm.at[pl.ds(p2, 1), :], y2.at[slot, pl.ds(r, 1), :], sem.at[slot, 1]).start()
            return carry

        lax.fori_loop(0, rows, issue, 0, unroll=8)

    @pl.when(i == 0)
    def _():
        gather(0, 0)

    @pl.when(i + 1 < n)
    def _():
        gather(i + 1, (i + 1) % 2)

    slot = i % 2
    pltpu.make_async_copy(y_hbm.at[pl.ds(0, rows), :], y1.at[slot], sem.at[slot, 0]).wait()
    pltpu.make_async_copy(y_hbm.at[pl.ds(0, rows), :], y2.at[slot], sem.at[slot, 1]).wait()
    ff = w1_ref[...] * y1[slot] + w2_ref[...] * y2[slot]
    o_ref[...] = _layer_norm(DEEPNORM_ALPHA * x_ref[...] + ff, g_ref[...], b_ref[...])


def _combine(pos, y, x, w1, w2, g, b, rows):
    t, d = x.shape
    row = pl.BlockSpec((rows, d), lambda i, p: (i, 0))
    col = pl.BlockSpec((rows, 1), lambda i, p: (i, 0))
    vec = pl.BlockSpec((1, d), lambda i, p: (0, 0))
    return pl.pallas_call(
        _combine_kernel,
        grid_spec=pltpu.PrefetchScalarGridSpec(
            num_scalar_prefetch=1,
            grid=(t // rows,),
            in_specs=[pl.BlockSpec(memory_space=pl.ANY), row, col, col, vec, vec],
            out_specs=row,
            scratch_shapes=[pltpu.VMEM((2, rows, d), F32), pltpu.VMEM((2, rows, d), F32),
                            pltpu.SemaphoreType.DMA((2, 2))],
        ),
        out_shape=jax.ShapeDtypeStruct((t, d), F32),
        compiler_params=_params("arbitrary"),
        name="moe_combine",
    )(pos, y, x, w1, w2, g, b)


def _moe(x, w_router_t, wg, wu, wd, g, b, tm, tile, tf):
    t, d = x.shape
    idx, wts = _router(x, w_router_t, tm)
    e_flat = idx.reshape(-1)
    onehot = (e_flat[:, None] == jnp.arange(N_EXPERTS, dtype=jnp.int32)[None, :]).astype(jnp.int32)
    csum = jnp.cumsum(onehot, axis=0)
    rank = jnp.sum(csum * onehot, axis=1) - 1
    counts = csum[-1]
    padded = ((counts + tile - 1) // tile) * tile
    ends = jnp.cumsum(padded)
    starts = ends - padded
    pos = (starts[e_flat] + rank).astype(jnp.int32)
    assert (2 * t) % tile == 0
    n_slots = 2 * t + N_EXPERTS * tile
    tile_start =jnp.arange(n_slots // tile, dtype=jnp.int32) * tile
    tile_expert = jnp.minimum(
        jnp.sum((tile_start[:, None] >= ends[None, :]).astype(jnp.int32), axis=1), N_EXPERTS - 1)
    tail = ends[-1] + jnp.arange(N_EXPERTS, dtype=jnp.int32) * tile
    fill_start = jnp.concatenate([ends - tile, tail])
    fill_valid = jnp.concatenate([padded > counts, tail < n_slots])
    fill = jnp.stack([fill_valid.astype(jnp.int32), jnp.where(fill_valid, fill_start, 0).astype(jnp.int32)],
                     axis=1).reshape(-1)

    xs = _dispatch(pos, fill, x, n_slots, tm, tile)
    ys = _ffn(tile_expert, xs, wg, wu, wd, None, tile, tf)
    return _combine(pos, ys, x, wts[0][:, None], wts[1][:, None], g, b, tm)


def _swap_halves(w):
    half = w.shape[-1] // 2
    return jnp.concatenate([w[..., half:], w[..., :half]], axis=-1)


def _prep_weights(w):
    out = {}
    row = lambda v: v.reshape(1, -1).astype(F32)
    out["ln_mix"] = [(row(w["ln_mix_g"][i]), row(w["ln_mix_b"][i])) for i in range(DEPTH)]
    out["ln_ffn"] = [(row(w["ln_ffn_g"][i]), row(w["ln_ffn_b"][i])) for i in range(DEPTH)]
    out["rg_wg"] = w["rg_w_gate"][0].astype(BF16)
    out["rg_wx"] = w["rg_w_x"][0].astype(BF16)
    out["rg"] = {
        "conv_w": w["rg_conv_w"][0], "conv_b": row(w["rg_conv_b"][0]),
        "w_a": w["rg_w_a"][0].astype(BF16), "b_a": row(w["rg_b_a"][0]),
        "w_i": w["rg_w_i"][0].astype(BF16), "b_i": row(w["rg_b_i"][0]),
        "lam": row(w["rg_lambda"][0]),
    }
    out["rg_wout"] = w["rg_w_out"][0].astype(BF16)
    out["ffn"] = (w["ffn_w_gate"].astype(BF16), w["ffn_w_up"].astype(BF16), w["ffn_w_down"].astype(BF16))
    out["moe"] = (w["moe_w_gate"][0].astype(BF16), w["moe_w_up"][0].astype(BF16), w["moe_w_down"][0].astype(BF16))
    e = w["moe_w_router"].shape[-1]
    out["router_t"] = jnp.concatenate(
        [w["moe_w_router"][0].T, jnp.zeros((2 * SUBLANES - e, w["moe_w_router"].shape[1]), F32)], axis=0)
    out["ple_wg"] = w["ple_w_gate"].astype(BF16)
    out["ple_wp"] = w["ple_w_proj"].astype(BF16)
    kv = w["kv_w_a"]
    out["kv_wc"] = kv[:, :KV_LORA_RANK].astype(BF16)
    out["kv_wr"] = kv[:, KV_LORA_RANK:].astype(BF16)
    out["kv_wrs"] = _swap_halves(kv[:, KV_LORA_RANK:]).astype(BF16)
    out["kv_g"] = row(w["kv_norm_g"])
    out["q_wa"] = w["mla_w_q_a"][0].astype(BF16)
    out["q_g"] = row(w["mla_q_norm_g"][0])
    qb = w["mla_w_q_b"][0].reshape(-1, N_HEADS, QK_NOPE_DIM + QK_ROPE_DIM)
    lora = qb.shape[0]
    q_nope = qb[:, :, :QK_NOPE_DIM].reshape(lora, -1)
    q_rope = qb[:, :, QK_NOPE_DIM:]
    out["q_wb"] = jnp.concatenate(
        [q_nope, q_rope.reshape(lora, -1), _swap_halves(q_rope).reshape(lora, -1)], axis=1).astype(BF16)
    uk = jnp.transpose(w["kv_w_uk"], (1, 2, 0))
    zk = jnp.zeros_like(uk[0])
    out["wuk"] = jnp.stack([
        jnp.concatenate([jnp.concatenate([uk[2 * j], zk], axis=1), jnp.concatenate([zk, uk[2 * j + 1]], axis=1)],
                        axis=0) for j in range(N_HEADS // 2)]).astype(BF16)
    uv = jnp.transpose(w["kv_w_uv"], (1, 0, 2))
    zv = jnp.zeros_like(uv[0])
    out["wuv"] = jnp.stack([
        jnp.concatenate([jnp.concatenate([uv[2 * j], zv], axis=1), jnp.concatenate([zv, uv[2 * j + 1]], axis=1)],
                        axis=0) for j in range(N_HEADS // 2)]).astype(BF16)
    out["wo"] = w["mla_w_o"][0].astype(BF16)
    gap = LANES - QK_NOPE_DIM - QK_ROPE_DIM
    tiles = lambda nope, rope: jnp.concatenate(
        [nope, rope, jnp.zeros(rope.shape[:-1] + (gap,), F32)], axis=-1).reshape(nope.shape[0], -1)
    q_nope3 = qb[:, :, :QK_NOPE_DIM]
    out["q_wb_heads"] = jnp.concatenate(
        [tiles(q_nope3, q_rope), tiles(jnp.zeros_like(q_nope3), _swap_halves(q_rope))], axis=1).astype(BF16)
    uk = w["kv_w_uk"]
    out["k_w_heads"] = tiles(uk, jnp.zeros(uk.shape[:2] + (QK_ROPE_DIM,), F32)).astype(BF16)
    out["k_place"] = tiles(jnp.zeros((QK_ROPE_DIM, 1, QK_NOPE_DIM), F32),
                           jnp.eye(QK_ROPE_DIM, dtype=F32)[:, None, :]).astype(BF16)
    out["v_w_heads"] = w["kv_w_uv"].reshape(KV_LORA_RANK, -1).astype(BF16)
    return out


def _rope_tables(pos, repeat):
    inv = ROPE_THETA ** (-jnp.arange(0, QK_ROPE_DIM, 2, dtype=F32) / QK_ROPE_DIM)
    ang = pos[:, None] * inv[None, :]
    cos, sin = jnp.cos(ang), jnp.sin(ang)
    cos_k = jnp.tile(jnp.concatenate([cos, cos], axis=-1), (repeat, 1))
    sin_k = jnp.tile(jnp.concatenate([-sin, sin], axis=-1), (repeat, 1))
    return cos_k, sin_k, jnp.tile(cos_k, (1, N_HEADS)), jnp.tile(sin_k, (1, N_HEADS))


def _head_tile_tables(cos_k, sin_k):
    n = cos_k.shape[0]
    gap = jnp.zeros((n, LANES - QK_NOPE_DIM - QK_ROPE_DIM), F32)
    cos_t = jnp.concatenate([jnp.ones((n, QK_NOPE_DIM), F32), cos_k, gap], axis=1)
    sin_t = jnp.concatenate([jnp.zeros((n, QK_NOPE_DIM), F32), sin_k, gap], axis=1)
    return jnp.tile(cos_t, (1, N_HEADS)), jnp.tile(sin_t, (1, N_HEADS))


def _tile_rows(t, want):
    tm = min(want, t)
    assert t % tm == 0
    return tm


def _trunk(x3, p4, conv0, rnn0, pos, pw, paged):
    nb, s, d = x3.shape
    t = nb * s
    x = x3.reshape(t, d)
    p = p4.reshape(DEPTH, t, -1)
    tm = _tile_rows(t, 512)

    gate, u = _rg_in(x, pw["rg_wg"], pw["rg_wx"], tm)
    if paged is None:
        conv_pad = jnp.concatenate(
            [jnp.zeros((nb, SUBLANES - (CONV_WIDTH - 1), D_RNN), F32), conv0], axis=1)
        hg, h_last = _rglru_seq(gate, u, conv_pad, rnn0[:, None, :], pw["rg"], nb, s, min(s, 256), 512)
        h_last = h_last[:, 0, :]
        tabs = _rope_tables(pos, 1)
    else:
        to_tm = lambda a: jnp.transpose(a.reshape(nb, s, -1), (1, 0, 2))
        hg, h_last = _rglru_step(to_tm(gate), to_tm(u), jnp.transpose(conv0, (1, 0, 2)), rnn0, pw["rg"], 512)
        hg = jnp.transpose(hg, (1, 0, 2)).reshape(t, -1)
        tabs = _rope_tables(pos, nb)
    conv_state = u.reshape(nb, s, -1)[:, s - (CONV_WIDTH - 1):, :]
    x = _mm_ln(hg, pw["rg_wout"], x, *pw["ln_mix"][0], tm)
    x = _ffn(jnp.zeros((t // tm,), jnp.int32), x, *pw["ffn"], pw["ln_ffn"][0], tm, pw["ffn"][0].shape[2])
    x = _ple(x, p[0], pw["ple_wg"][0], pw["ple_wp"][0], tm)
    cos_k, sin_k, cos_q, sin_q = tabs
    kv_args = (x, pw["kv_wc"], pw["kv_wr"], pw["kv_wrs"], pw["kv_g"], cos_k, sin_k, tm)

    tq = _tile_rows(t, 256)
    if paged is None:
        ckv, kpe, kh, vt = _kva(*kv_args, heads=(pw["k_w_heads"], pw["k_place"], pw["v_w_heads"]))
        q = _q_heads(x, pw["q_wa"], pw["q_g"], pw["q_wb_heads"], *_head_tile_tables(cos_k, sin_k), tq)
        v = _attn_prompt(q, kh, vt, nb, s, tq, 256)
        x = _mm_ln(v, pw["wo"], x, *pw["ln_mix"][1], tm)
    else:
        cache_ckv, cache_kpe, page_table = paged
        ckv, kpe = _kva(*kv_args)
        q = _q_proj(x, pw["q_wa"], pw["q_g"], pw["q_wb"], pw["wuk"], cos_q, sin_q, tq, F32)
        o = _attn_paged(q.reshape(t // tq, N_HEADS, tq // s, s, QK_WIDTH), ckv.reshape(nb, s, -1),
                        kpe.reshape(nb, s, -1), cache_ckv, jnp.swapaxes(cache_kpe, 1, 2), page_table, 32, 2)
        x = _mla_out(o.reshape(t * N_HEADS, KV_LORA_RANK), pw["wuv"], pw["wo"], x, *pw["ln_mix"][1], tq)
    x = _moe(x, pw["router_t"], *pw["moe"], *pw["ln_ffn"][1], tq, 512 if paged is None else 256,
             pw["moe"][0].shape[2] // 2)
    x = _ple(x, p[1], pw["ple_wg"][1], pw["ple_wp"][1], tm)
    return (x.reshape(nb, s, d), conv_state[None], h_last[None], ckv.reshape(nb, s, -1), kpe.reshape(nb, s, -1))


def kernel(x_prompt, x_sample, p_prompt, p_sample, state_conv, state_rnn, cache_ckv, cache_kpe, page_table, ln_mix_g, ln_mix_b, ln_ffn_g, ln_ffn_b, rg_w_gate, rg_w_x, rg_conv_w, rg_conv_b, rg_w_a, rg_b_a, rg_w_i, rg_b_i, rg_lambda, rg_w_out, mla_w_q_a, mla_q_norm_g, mla_w_q_b, mla_w_o, kv_w_a, kv_norm_g, kv_w_uk, kv_w_uv, ffn_w_gate, ffn_w_up, ffn_w_down, moe_w_router, moe_w_gate, moe_w_up, moe_w_down, ple_w_gate, ple_w_proj):
    w = dict(
        ln_mix_g=ln_mix_g, ln_mix_b=ln_mix_b, ln_ffn_g=ln_ffn_g, ln_ffn_b=ln_ffn_b,
        rg_w_gate=rg_w_gate, rg_w_x=rg_w_x, rg_conv_w=rg_conv_w, rg_conv_b=rg_conv_b, rg_w_a=rg_w_a,
        rg_b_a=rg_b_a, rg_w_i=rg_w_i, rg_b_i=rg_b_i, rg_lambda=rg_lambda, rg_w_out=rg_w_out,
        mla_w_q_a=mla_w_q_a, mla_q_norm_g=mla_q_norm_g, mla_w_q_b=mla_w_q_b, mla_w_o=mla_w_o,
        kv_w_a=kv_w_a, kv_norm_g=kv_norm_g, kv_w_uk=kv_w_uk, kv_w_uv=kv_w_uv,
        ffn_w_gate=ffn_w_gate, ffn_w_up=ffn_w_up, ffn_w_down=ffn_w_down,
        moe_w_router=moe_w_router, moe_w_gate=moe_w_gate, moe_w_up=moe_w_up, moe_w_down=moe_w_down,
        ple_w_gate=ple_w_gate, ple_w_proj=ple_w_proj)
    pw = _prep_weights(w)
    nb, s = x_prompt.shape[:2]
    dec_s = x_sample.shape[1]
    past_len = page_table.shape[1] * cache_ckv.shape[1]
    n_a = state_conv.shape[0]
    assert n_a == 1
    conv0_p = jnp.zeros((nb, CONV_WIDTH - 1, D_RNN), state_conv.dtype)
    rnn0_p = jnp.zeros((nb, D_RNN), state_rnn.dtype)
    y_p, conv_p, rnn_p, ckv_p, kpe_p = _trunk(
        x_prompt, p_prompt, conv0_p, rnn0_p, jnp.arange(s, dtype=F32), pw, None)
    y_s, conv_s, rnn_s, ckv_s, kpe_s = _trunk(
        x_sample, p_sample, state_conv[0], state_rnn[0], past_len + jnp.arange(dec_s, dtype=F32), pw,
        (cache_ckv, cache_kpe, page_table))
    return (y_p, y_s, conv_p, rnn_p, ckv_p, kpe_p, conv_s, rnn_s, ckv_s, kpe_s)
```
